```python
import math
import jax
import jax.numpy as jnp
from jax import lax
import numpy as np

D_MODEL = 2048
BATCH = 4
SEQ = 4096
DEPTH = 2

CTX_LEN = 256
GRID_W = 64
NORM_EPS = 1e-6

A_HEADS = 8
A_DK = 128
A_DV = 128
A_DIM = A_HEADS * A_DK
A_CHUNK = 64

B_HEADS = 16
B_N = 64
B_DIM = B_HEADS * B_N
B_W_LORA = 64
B_A_LORA = 64
B_G_LORA = 160
B_GN_EPS = 64e-5
B_DECAY_SCALE = math.exp(-0.5)

C_HEADS = 8
C_DK = 128
C_DV = 256
C_QK = C_HEADS * C_DK
C_V = C_HEADS * C_DV
C_CHUNK = 64
C_GATE_CAP = 15.0

FFN_HIDDEN = -(-8 * D_MODEL // (3 * 256)) * 256

CONV_CH = 3 * A_DIM + 3 * B_DIM
AB_Z0 = CONV_CH
AB_GATE0 = AB_Z0 + A_DIM
AB_WD0 = AB_GATE0 + 4 * A_HEADS
AB_AD0 = AB_WD0 + 2 * B_W_LORA
AB_GD0 = AB_AD0 + 2 * B_A_LORA
AB_IN = AB_GD0 + B_G_LORA
AB_MIX = A_DIM + B_DIM

C_O0 = 2 * C_QK + C_V
C_GATE0 = C_O0 + C_V
C_IN = C_GATE0 + 4 * C_HEADS

kernel_name = 'bidir_gdn_rwkv7_mlstm_dit_block'


def rms_norm(x, w, eps=NORM_EPS):
    xf = x.astype(jnp.float32)
    y = xf * lax.rsqrt(jnp.mean(xf * xf, axis=-1, keepdims=True) + eps)
    return (y * w.astype(jnp.float32)).astype(x.dtype)


def modulate(x, w, shift, scale):
    return rms_norm(x, w) * (1 + scale) + shift


def l2norm(t):
    return t * lax.rsqrt(jnp.sum(t * t, axis=-1, keepdims=True) + 1e-6)


def softcap(t, cap):
    return cap * jnp.tanh(t / cap)


def split_heads(t, n_heads):
    b, l, _ = t.shape
    return t.reshape(b, l, n_heads, -1).transpose(0, 2, 1, 3)


def merge_heads(t):
    b, h, l, d = t.shape
    return t.transpose(0, 2, 1, 3).reshape(b, l, h * d)


def to_chunks(t, size):
    b, h, l = t.shape[:3]
    return jnp.moveaxis(t.reshape(b, h, l // size, size, *t.shape[3:]), 2, 0)


def from_chunks(t):
    t = jnp.moveaxis(t, 0, 2)
    b, h, n, size = t.shape[:4]
    return t.reshape(b, h, n * size, *t.shape[4:])


def grid_dwconv3x3(t, w, rows):
    b, l, ch = t.shape
    cols = l // rows
    g = jnp.pad(t.reshape(b, rows, cols, ch), ((0, 0), (1, 1), (1, 1), (0, 0)))
    out = sum(g[:, dr:dr + rows, dc:dc + cols] * w[dr, dc] for dr in range(3) for dc in range(3))
    return out.reshape(b, l, ch)


def swiglu(h, w1, w3, w2):
    return jnp.einsum('blf,fd->bld', jax.nn.silu(h @ w1) * (h @ w3), w2)


def gdn_run(inputs, s0, with_out):
    q, k, v, g, beta = inputs
    dv = v.shape[-1]
    kc, vc, bc = to_chunks(k, A_CHUNK), to_chunks(v, A_CHUNK), to_chunks(beta, A_CHUNK)
    gc = jnp.cumsum(to_chunks(g, A_CHUNK), axis=-1)
    idx = jnp.arange(A_CHUNK)
    causal = idx[:, None] >= idx[None, :]
    decay = jnp.exp(jnp.where(causal, gc[..., :, None] - gc[..., None, :], -jnp.inf))
    kb = kc * bc[..., None]
    m = jnp.where(idx[:, None] > idx[None, :], jnp.einsum('nbhid,nbhjd->nbhij', kb, kc) * decay, 0.0)
    rhs = jnp.concatenate([vc * bc[..., None], kb * jnp.exp(gc)[..., None]], axis=-1)
    uw = lax.linalg.triangular_solve(m, rhs, left_side=True, lower=True, unit_diagonal=True)
    xs = (kc, gc, uw[..., :dv], uw[..., dv:])
    if with_out:
        qc = to_chunks(q, A_CHUNK)
        xs = xs + (qc * jnp.exp(gc)[..., None], jnp.einsum('nbhid,nbhjd->nbhij', qc, kc) * decay)

    def step(s, xc):
        k_c, g_c, u_c, w_c = xc[:4]
        v_new = u_c - jnp.einsum('bhcd,bhde->bhce', w_c, s)
        g_last = g_c[..., -1]
        s_next = s * jnp.exp(g_last)[..., None, None] + jnp.einsum(
            'bhcd,bhce->bhde', k_c * jnp.exp(g_last[..., None] - g_c)[..., None], v_new)
        if with_out:
            qg_c, qk_c = xc[4:]
            o = jnp.einsum('bhcd,bhde->bhce', qg_c, s) + jnp.einsum('bhij,bhje->bhie', qk_c, v_new)
            return s_next, o
        return s_next, None

    s_fin, o = lax.scan(step, s0, xs)
    return s_fin, (from_chunks(o) if with_out else None)


def rwkv7_run(inputs, s0, with_out):
    r, w, kk, kka, v, kt = inputs
    seq_major = lambda t: jnp.moveaxis(t, 2, 0)
    xs = tuple(seq_major(t) for t in (w, kk, kka, v, kt))
    if with_out:
        xs = xs + (seq_major(r),)

    def step(s, xt):
        w_t, kk_t, kka_t, v_t, k_t = xt[:5]
        s_kk = jnp.einsum('bhvk,bhk->bhv', s, kk_t)
        s = s * w_t[..., None, :] - s_kk[..., :, None] * kka_t[..., None, :] + v_t[..., :, None] * k_t[..., None, :]
        if with_out:
            return s, jnp.einsum('bhvk,bhk->bhv', s, xt[5])
        return s, None

    s_fin, y = lax.scan(step, s0, xs)
    return s_fin, (jnp.moveaxis(y, 0, 2) if with_out else None)


def mlstm_run(inputs, state0, with_out):
    q, k, v, ig, logf = inputs
    kc, vc, igc = to_chunks(k, C_CHUNK), to_chunks(v, C_CHUNK), to_chunks(ig, C_CHUNK)
    b = jnp.cumsum(to_chunks(logf, C_CHUNK), axis=-1)
    w_end = b[..., -1:] - b + igc
    xs = (kc, vc, b, w_end)
    if with_out:
        qc = to_chunks(q, C_CHUNK)
        idx = jnp.arange(C_CHUNK)
        causal = idx[:, None] >= idx[None, :]
        dmat = jnp.where(causal, b[..., :, None] - b[..., None, :] + igc[..., None, :], -jnp.inf)
        xs = xs + (qc, dmat, jnp.max(dmat, axis=-1), jnp.einsum('nbhid,nbhjd->nbhij', qc, kc))

    def step(carry, xc):
        c_st, n_st, m_st = carry
        k_c, v_c, b_c, w_c = xc[:4]
        b_last = b_c[..., -1]
        m_new = jnp.maximum(b_last + m_st, jnp.max(w_c, axis=-1))
        scale = jnp.exp(b_last + m_st - m_new)
        kw = k_c * jnp.exp(w_c - m_new[..., None])[..., None]
        c_next = c_st * scale[..., None, None] + jnp.einsum('bhcd,bhce->bhde', kw, v_c)
        n_next = n_st * scale[..., None] + jnp.sum(kw, axis=-2)
        if with_out:
            q_c, d_c, dmax_c, qk_c = xc[4:]
            m_row = jnp.maximum(b_c + m_st[..., None], dmax_c)
            inter = jnp.exp(b_c + m_st[..., None] - m_row)
            s = qk_c * jnp.exp(d_c - m_row[..., None])
            num = jnp.einsum('bhij,bhje->bhie', s, v_c) + inter[..., None] * jnp.einsum('bhid,bhde->bhie', q_c, c_st)
            den = jnp.sum(s, axis=-1) + inter * jnp.einsum('bhid,bhd->bhi', q_c, n_st)
            h = num / jnp.maximum(jnp.abs(den), jnp.exp(-m_row))[..., None]
            return (c_next, n_next, m_new), h
        return (c_next, n_next, m_new), None

    s_fin, h = lax.scan(step, state0, xs)
    return s_fin, (from_chunks(h) if with_out else None)


def flip_seq(ts):
    return tuple(jnp.flip(t, axis=2) for t in ts)


def scan_two_streams(run, ctx_in, lat_in, state0, ctx_out):
    s_ctx, y_ctx = run(ctx_in, state0, ctx_out)
    _, y_lat = run(lat_in, s_ctx, True)
    return y_ctx, y_lat


def bidirectional(run, ctx_fwd, lat_fwd, ctx_bwd, lat_bwd, state0, ctx_out):
    yc_f, yl_f = scan_two_streams(run, ctx_fwd, lat_fwd, state0, ctx_out)
    yc_b, yl_b = scan_two_streams(run, flip_seq(ctx_bwd), flip_seq(lat_bwd), state0, ctx_out)
    y_lat = yl_f + jnp.flip(yl_b, axis=2)
    y_ctx = (yc_f + jnp.flip(yc_b, axis=2)) if ctx_out else None
    return y_ctx, y_lat


def ab_features(h, rows, w_in, conv_w, a_log, dt_bias, w0, w_up, a0, a_up, g_up, k_k, k_a):
    b, l, _ = h.shape
    p = jnp.einsum('bld,dp->blp', h, w_in).astype(jnp.float32)
    cv = grid_dwconv3x3(p[..., :CONV_CH], conv_w, rows)
    qa, ka, va = jnp.split(jax.nn.silu(cv[..., :3 * A_DIM]), 3, axis=-1)
    qa = l2norm(split_heads(qa, A_HEADS)) * A_DK ** -0.5
    ka = l2norm(split_heads(ka, A_HEADS))
    va = split_heads(va, A_HEADS)
    z = p[..., AB_Z0:AB_GATE0]
    gl = p[..., AB_GATE0:AB_WD0].reshape(b, l, 2, 2, A_HEADS).transpose(2, 3, 0, 4, 1)
    log_alpha = -jnp.exp(a_log)[:, None, :, None] * jax.nn.softplus(gl[0] + dt_bias[:, None, :, None])
    beta = jax.nn.sigmoid(gl[1])
    gdn_f = (qa, ka, va, log_alpha[0], beta[0])
    gdn_b = (qa, ka, va, log_alpha[1], beta[1])
    r, kb, vb = jnp.split(cv[..., 3 * A_DIM:], 3, axis=-1)
    wd = p[..., AB_WD0:AB_AD0].reshape(b, l, 2, B_W_LORA)
    ad = p[..., AB_AD0:AB_GD0].reshape(b, l, 2, B_A_LORA)
    log_w = -B_DECAY_SCALE * jax.nn.sigmoid(w0 + jnp.einsum('bldr,drc->bldc', jnp.tanh(wd), w_up))
    a = jax.nn.sigmoid(a0 + jnp.einsum('bldr,drc->bldc', ad, a_up))
    g_out = jnp.einsum('blr,rc->blc', jax.nn.sigmoid(p[..., AB_GD0:AB_IN]), g_up)
    kk = l2norm(split_heads(kb * k_k, B_HEADS))
    kt = kb[:, :, None] * (1 + (a - 1) * k_a)
    rh, vh = split_heads(r, B_HEADS), split_heads(vb, B_HEADS)

    def direction(d):
        kka = kk * split_heads(a[:, :, d], B_HEADS)
        return (rh, split_heads(jnp.exp(log_w[:, :, d]), B_HEADS), kk, kka, vh, split_heads(kt[:, :, d], B_HEADS))

    post = (z, rh, split_heads(kt[:, :, 0] + kt[:, :, 1], B_HEADS), vh, g_out)
    return gdn_f, gdn_b, direction(0), direction(1), post


def ab_output(o_a, y_b, post, gdn_norm_w, r_k, ln_w, ln_b, w_out, dtype):
    z, rh, kt_sum, vh, g_out = post
    ya = merge_heads(rms_norm(o_a, gdn_norm_w)) * jax.nn.silu(z)
    mu = jnp.mean(y_b, axis=-1, keepdims=True)
    var = jnp.mean(jnp.square(y_b - mu), axis=-1, keepdims=True)
    yb = (y_b - mu) * lax.rsqrt(var + B_GN_EPS)
    bonus = jnp.sum(rh * kt_sum * r_k[:, None, :], axis=-1, keepdims=True) * vh
    yb = (merge_heads(yb) * ln_w + ln_b + merge_heads(bonus)) * g_out
    return jnp.einsum('blc,cd->bld', jnp.concatenate([ya, yb], axis=-1), w_out).astype(dtype)


def ab_mixer(h_c, h_l, rows, ctx_out, w_in, conv_w, a_log, dt_bias, gdn_norm_w, w0, w_up, a0, a_up, g_up,
             k_k, k_a, r_k, ln_w, ln_b, w_out):
    gc_f, gc_b, rc_f, rc_b, post_c = ab_features(h_c, 1, w_in, conv_w, a_log, dt_bias, w0, w_up, a0, a_up, g_up, k_k, k_a)
    gl_f, gl_b, rl_f, rl_b, post_l = ab_features(h_l, rows, w_in, conv_w, a_log, dt_bias, w0, w_up, a0, a_up, g_up, k_k, k_a)
    bsz = h_l.shape[0]
    s0_a = jnp.zeros((bsz, A_HEADS, A_DK, A_DV), jnp.float32)
    s0_b = jnp.zeros((bsz, B_HEADS, B_N, B_N), jnp.float32)
    oa_c, oa_l = bidirectional(gdn_run, gc_f, gl_f, gc_b, gl_b, s0_a, ctx_out)
    yb_c, yb_l = bidirectional(rwkv7_run, rc_f, rl_f, rc_b, rl_b, s0_b, ctx_out)
    out_l = ab_output(oa_l, yb_l, post_l, gdn_norm_w, r_k, ln_w, ln_b, w_out, h_l.dtype)
    out_c = ab_output(oa_c, yb_c, post_c, gdn_norm_w, r_k, ln_w, ln_b, w_out, h_c.dtype) if ctx_out else None
    return out_c, out_l


def mlstm_features(h, w_in, i_bias, f_bias):
    b, l, _ = h.shape
    p = jnp.einsum('bld,dp->blp', h, w_in).astype(jnp.float32)
    q = split_heads(p[..., :C_QK], C_HEADS) * C_DK ** -0.5
    k = split_heads(p[..., C_QK:2 * C_QK], C_HEADS)
    v = split_heads(p[..., 2 * C_QK:C_O0], C_HEADS)
    o = p[..., C_O0:C_GATE0]
    gl = p[..., C_GATE0:].reshape(b, l, 2, 2, C_HEADS).transpose(2, 3, 0, 4, 1)
    ig = softcap(gl[0] + i_bias[:, None, :, None], C_GATE_CAP)
    logf = jax.nn.log_sigmoid(softcap(gl[1] + f_bias[:, None, :, None], C_GATE_CAP))
    return (q, k, v, ig[0], logf[0]), (q, k, v, ig[1], logf[1]), o


def mlstm_output(hh, o, norm_w, w_out, dtype):
    y = merge_heads(rms_norm(hh, norm_w.reshape(C_HEADS, 1, C_DV))) * jax.nn.sigmoid(o)
    return jnp.einsum('blc,cd->bld', y, w_out).astype(dtype)


def mlstm_mixer(h_c, h_l, ctx_out, w_in, i_bias, f_bias, norm_w, w_out):
    c_f, c_b, o_c = mlstm_features(h_c, w_in, i_bias, f_bias)
    l_f, l_b, o_l = mlstm_features(h_l, w_in, i_bias, f_bias)
    bsz = h_l.shape[0]
    s0 = (jnp.zeros((bsz, C_HEADS, C_DK, C_DV), jnp.float32),
          jnp.zeros((bsz, C_HEADS, C_DK), jnp.float32),
          jnp.zeros((bsz, C_HEADS), jnp.float32))
    hc, hl = bidirectional(mlstm_run, c_f, l_f, c_b, l_b, s0, ctx_out)
    out_l = mlstm_output(hl, o_l, norm_w, w_out, h_l.dtype)
    out_c = mlstm_output(hc, o_c, norm_w, w_out, h_c.dtype) if ctx_out else None
    return out_c, out_l


def setup_inputs(seed: int = 0) -> dict:
    key = jax.random.key(seed)
    keys = iter(jax.random.split(key, 48))

    def nrm(shape, scale):
        return scale * jax.random.normal(next(keys), shape, jnp.float32)

    def unif(shape, lo, hi):
        return jax.random.uniform(next(keys), shape, jnp.float32, lo, hi)

    ne, no = (DEPTH + 1) // 2, DEPTH // 2
    d = D_MODEL
    dt = jnp.exp(unif((ne, 2, A_HEADS), math.log(1e-3), math.log(1e-1)))
    return {
        'x': nrm((BATCH, SEQ, d), 1.0),
        'c': nrm((BATCH, d), 1.0),
        'ctx': nrm((BATCH, CTX_LEN, d), 1.0),
        'c_ctx': nrm((d,), 1.0),
        'ada_w': nrm((DEPTH, d, 6 * d), 0.5 * d ** -0.5),
        'ada_b': nrm((DEPTH, 6 * d), 0.02),
        'norm_w': 1.0 + nrm((DEPTH, 2, d), 0.02),
        'ab_w_in': nrm((ne, d, AB_IN), d ** -0.5),
        'ab_conv_w': nrm((ne, 3, 3, CONV_CH), 1.0 / 3.0),
        'gdn_a_log': jnp.log(unif((ne, 2, A_HEADS), 1.0, 16.0)),
        'gdn_dt_bias': dt + jnp.log(-jnp.expm1(-dt)),
        'gdn_norm_w': 1.0 + nrm((ne, A_DV), 0.02),
        'rw_w0': unif((ne, 2, B_DIM), -3.0, 3.0),
        'rw_w_up': nrm((ne, 2, B_W_LORA, B_DIM), 0.5 * B_W_LORA ** -0.5),
        'rw_a0': nrm((ne, 2, B_DIM), 0.1),
        'rw_a_up': nrm((ne, 2, B_A_LORA, B_DIM), 0.5 * B_A_LORA ** -0.5),
        'rw_g_up': nrm((ne, B_G_LORA, B_DIM), B_G_LORA ** -0.5),
        'rw_k_k': 0.85 + nrm((ne, B_DIM), 0.02),
        'rw_k_a': 1.0 + nrm((ne, B_DIM), 0.02),
        'rw_r_k': nrm((ne, B_HEADS, B_N), 0.1),
        'rw_ln_w': 1.0 + nrm((ne, B_DIM), 0.02),
        'rw_ln_b': nrm((ne, B_DIM), 0.02),
        'ab_w_out': nrm((ne, AB_MIX, d), AB_MIX ** -0.5),
        'ml_w_in': nrm((no, d, C_IN), d ** -0.5),
        'ml_i_bias': nrm((no, 2, C_HEADS), 0.1),
        'ml_f_bias': jnp.linspace(3.0, 6.0, C_HEADS) + nrm((no, 2, C_HEADS), 0.1),
        'ml_norm_w': 1.0 + nrm((no, C_V), 0.02),
        'ml_w_out': nrm((no, C_V, d), C_V ** -0.5),
        'ffn_w1': nrm((DEPTH, d, FFN_HIDDEN), d ** -0.5),
        'ffn_w3': nrm((DEPTH, d, FFN_HIDDEN), d ** -0.5),
        'ffn_w2': nrm((DEPTH, FFN_HIDDEN, d), FFN_HIDDEN ** -0.5),
        'final_norm_w': 1.0 + nrm((d,), 0.02),
    }


def reference(x, c, ctx, c_ctx, ada_w, ada_b, norm_w, ab_w_in, ab_conv_w, gdn_a_log, gdn_dt_bias, gdn_norm_w,
              rw_w0, rw_w_up, rw_a0, rw_a_up, rw_g_up, rw_k_k, rw_k_a, rw_r_k, rw_ln_w, rw_ln_b, ab_w_out,
              ml_w_in, ml_i_bias, ml_f_bias, ml_norm_w, ml_w_out, ffn_w1, ffn_w3, ffn_w2, final_norm_w):
    rows = x.shape[1] // GRID_W
    cond_l = jax.nn.silu(c)[:, None, :]
    cond_c = jax.nn.silu(c_ctx)[None, None, :]
    xc = ctx
    for i in range(DEPTH):
        ctx_out = i < DEPTH - 1
        j = i // 2
        ml = jnp.split(jnp.einsum('bsd,de->bse', cond_l, ada_w[i]) + ada_b[i], 6, axis=-1)
        mc = jnp.split(jnp.einsum('bsd,de->bse', cond_c, ada_w[i]) + ada_b[i], 6, axis=-1)
        h_l = modulate(x, norm_w[i, 0], ml[0], ml[1])
        h_c = modulate(xc, norm_w[i, 0], mc[0], mc[1])
        if i % 2 == 0:
            o_c, o_l = ab_mixer(h_c, h_l, rows, ctx_out, ab_w_in[j], ab_conv_w[j], gdn_a_log[j], gdn_dt_bias[j],
                                gdn_norm_w[j], rw_w0[j], rw_w_up[j], rw_a0[j], rw_a_up[j], rw_g_up[j], rw_k_k[j],
                                rw_k_a[j], rw_r_k[j], rw_ln_w[j], rw_ln_b[j], ab_w_out[j])
        else:
            o_c, o_l = mlstm_mixer(h_c, h_l, ctx_out, ml_w_in[j], ml_i_bias[j], ml_f_bias[j], ml_norm_w[j], ml_w_out[j])
        x = x + ml[2] * o_l
        x = x + ml[5] * swiglu(modulate(x, norm_w[i, 1], ml[3], ml[4]), ffn_w1[i], ffn_w3[i], ffn_w2[i])
        if ctx_out:
            xc = xc + mc[2] * o_c
            xc = xc + mc[5] * swiglu(modulate(xc, norm_w[i, 1], mc[3], mc[4]), ffn_w1[i], ffn_w3[i], ffn_w2[i])
    return rms_norm(x, final_norm_w)
```

```python
import functools
import math

import numpy as np
import jax
import jax.numpy as jnp
from jax import lax
from jax.experimental import pallas as pl
from jax.experimental.pallas import tpu as pltpu

F32 = jnp.float32
BF16 = jnp.bfloat16

NORM_EPS = 1e-6
GRID_W = 64
LANES = 128
SUBLANES = 8
VMEM_LIMIT = 56 * 1024 * 1024

A_HEADS, A_DK = 8, 128
A_DIM = A_HEADS * A_DK
B_HEADS, B_N = 16, 64
B_DIM = B_HEADS * B_N
B_LORA = 64
B_G_LORA = 160
B_GN_EPS = 64e-5
B_DECAY_SCALE = math.exp(-0.5)
C_HEADS, C_DK, C_DV = 8, 128, 256
C_QK = C_HEADS * C_DK
C_V = C_HEADS * C_DV
C_GATE_CAP = 15.0
CONV_CH = 3 * A_DIM + 3 * B_DIM
CHUNK = 64
MOD_ROWS = 8


def _cparams(sem):
    return pltpu.CompilerParams(dimension_semantics=sem, vmem_limit_bytes=VMEM_LIMIT)


def _dot(a, b):
    return jnp.dot(a.astype(BF16), b.astype(BF16), preferred_element_type=F32)


def _dot_nt(a, b):
    return lax.dot_general(a.astype(BF16), b.astype(BF16), (((1,), (1,)), ((), ())),
                           preferred_element_type=F32)


def _dot_tn(a, b):
    return lax.dot_general(a.astype(BF16), b.astype(BF16), (((0,), (0,)), ((), ())),
                           preferred_element_type=F32)


def _split2(a):
    hi = a.astype(BF16)
    lo = (a - hi.astype(F32)).astype(BF16)
    return hi, lo


def _split3(a):
    hi = a.astype(BF16)
    r = a - hi.astype(F32)
    mid = r.astype(BF16)
    lo = (r - mid.astype(F32)).astype(BF16)
    return hi, mid, lo


def _dot_exact_rhs(a, b_exact):
    hi, mid, lo = _split3(a)
    b = b_exact.astype(BF16)
    d = lambda t: jnp.dot(t, b, preferred_element_type=F32)
    return d(hi) + d(mid) + d(lo)


def _dot_exact_lhs(a_exact, b):
    hi, mid, lo = _split3(b)
    a = a_exact.astype(BF16)
    d = lambda t: jnp.dot(a, t, preferred_element_type=F32)
    return d(hi) + d(mid) + d(lo)


def _dot3(a, b):
    ah, al = _split2(a)
    bh, bl = _split2(b)
    d = lambda s, t: jnp.dot(s, t, preferred_element_type=F32)
    return d(ah, bh) + d(al, bh) + d(ah, bl)


def _sigmoid(t):
    return 1.0 / (1.0 + jnp.exp(-t))


def _silu(t):
    return t * _sigmoid(t)


def _softplus(t):
    return jnp.maximum(t, 0.0) + jnp.log(1.0 + jnp.exp(-jnp.abs(t)))


def _unit_lower_inverse(n_mat, eye_f, steps):
    x = eye_f + n_mat
    pw = n_mat
    for _ in range(steps):
        pw = _dot(pw, pw)
        x = x + _dot(x, pw)
    return x


def _nilpotent_steps(c):
    return max(int(math.ceil(math.log2(c))) - 1, 0)


def _adaln_kernel(c_ref, w_ref, b_ref, o_ref):
    o_ref[...] = _dot3(_silu(c_ref[...]), w_ref[...]) + b_ref[...]


def adaln(cond, w, b):
    m, d = cond.shape
    e = w.shape[1]
    tn = _col_tile(e, 512)
    return pl.pallas_call(
        _adaln_kernel,
        grid=(e // tn,),
        in_specs=[pl.BlockSpec((m, d), lambda j: (0, 0)),
                  pl.BlockSpec((d, tn), lambda j: (0, j)),
                  pl.BlockSpec((1, tn), lambda j: (0, j))],
        out_specs=pl.BlockSpec((m, tn), lambda j: (0, j)),
        out_shape=jax.ShapeDtypeStruct((m, e), F32),
        compiler_params=_cparams(("parallel",)),
        name="adaln",
    )(cond, w, b)


def _modulated(x_ref, mod_ref, nw_ref, shift_idx, scale_idx):
    x = x_ref[...]
    ms = jnp.mean(x * x, axis=-1, keepdims=True)
    y = x * lax.rsqrt(ms + NORM_EPS) * nw_ref[...]
    return y * (1.0 + mod_ref[0, scale_idx:scale_idx + 1, :]) + mod_ref[0, shift_idx:shift_idx + 1, :]


def _modmm_kernel(x_ref, mod_ref, nw_ref, w_ref, o_ref, h_ref, *, shift_idx, scale_idx):
    @pl.when(pl.program_id(1) == 0)
    def _():
        h_ref[...] = _modulated(x_ref, mod_ref, nw_ref, shift_idx, scale_idx).astype(BF16)

    o_ref[...] = jnp.dot(h_ref[...], w_ref[...], preferred_element_type=F32)


def _row_tile(m, want):
    t = min(want, m)
    while m % t:
        t //= 2
    return t


def _col_tile(n, want):
    t = min(want, n)
    while n % t or t % LANES:
        t -= LANES
    return t


def modmm(x, mod, nw, w, *, shift_idx, scale_idx, rows_per_mod, tm=512, tn=512):
    m, d = x.shape
    n = w.shape[1]
    tm = _row_tile(min(m, rows_per_mod), tm)
    tn = _col_tile(n, tn)
    tpm = rows_per_mod // tm
    return pl.pallas_call(
        functools.partial(_modmm_kernel, shift_idx=shift_idx, scale_idx=scale_idx),
        grid=(m // tm, n // tn),
        in_specs=[pl.BlockSpec((tm, d), lambda i, j: (i, 0)),
                  pl.BlockSpec((1, MOD_ROWS, d), lambda i, j: (i // tpm, 0, 0)),
                  pl.BlockSpec((1, d), lambda i, j: (0, 0)),
                  pl.BlockSpec((d, tn), lambda i, j: (0, j))],
        out_specs=pl.BlockSpec((tm, tn), lambda i, j: (i, j)),
        out_shape=jax.ShapeDtypeStruct((m, n), F32),
        scratch_shapes=[pltpu.VMEM((tm, d), BF16)],
        compiler_params=_cparams(("parallel", "arbitrary")),
        name="modmm",
    )(x, mod, nw, w)


def _ffn_up_kernel(x_ref, mod_ref, nw_ref, w1_ref, w3_ref, o_ref, h_ref, *, shift_idx, scale_idx):
    @pl.when(pl.program_id(1) == 0)
    def _():
        h_ref[...] = _modulated(x_ref, mod_ref, nw_ref, shift_idx, scale_idx).astype(BF16)

    h = h_ref[...]
    a = jnp.dot(h, w1_ref[...], preferred_element_type=F32)
    b = jnp.dot(h, w3_ref[...], preferred_element_type=F32)
    o_ref[...] = (_silu(a) * b).astype(BF16)


def ffn_up(x, mod, nw, w1, w3, *, shift_idx, scale_idx, rows_per_mod, tm=512, tn=512):
    m, d = x.shape
    n = w1.shape[1]
    tm = _row_tile(min(m, rows_per_mod), tm)
    tn = _col_tile(n, tn)
    tpm = rows_per_mod // tm
    return pl.pallas_call(
        functools.partial(_ffn_up_kernel, shift_idx=shift_idx, scale_idx=scale_idx),
        grid=(m // tm, n // tn),
        in_specs=[pl.BlockSpec((tm, d), lambda i, j: (i, 0)),
                  pl.BlockSpec((1, MOD_ROWS, d), lambda i, j: (i // tpm, 0, 0)),
                  pl.BlockSpec((1, d), lambda i, j: (0, 0)),
                  pl.BlockSpec((d, tn), lambda i, j: (0, j)),
                  pl.BlockSpec((d, tn), lambda i, j: (0, j))],
        out_specs=pl.BlockSpec((tm, tn), lambda i, j: (i, j)),
        out_shape=jax.ShapeDtypeStruct((m, n), BF16),
        scratch_shapes=[pltpu.VMEM((tm, d), BF16)],
        compiler_params=_cparams(("parallel", "arbitrary")),
        name="ffn_up",
    )(x, mod, nw, w1, w3)


def _mm_res_kernel(a_ref, w_ref, res_ref, mod_ref, o_ref, *, gate_idx):
    acc = jnp.dot(a_ref[...], w_ref[...], preferred_element_type=F32)
    o_ref[...] = res_ref[...] + mod_ref[0, gate_idx:gate_idx + 1, :] * acc


def mm_res(a, w, res, mod, *, gate_idx, rows_per_mod, tm=512, tn=512):
    m, k = a.shape
    n = w.shape[1]
    tm = _row_tile(min(m, rows_per_mod), tm)
    tn = _col_tile(n, tn)
    tpm = rows_per_mod // tm
    return pl.pallas_call(
        functools.partial(_mm_res_kernel, gate_idx=gate_idx),
        grid=(m // tm, n // tn),
        in_specs=[pl.BlockSpec((tm, k), lambda i, j: (i, 0)),
                  pl.BlockSpec((k, tn), lambda i, j: (0, j)),
                  pl.BlockSpec((tm, tn), lambda i, j: (i, j)),
                  pl.BlockSpec((1, MOD_ROWS, tn), lambda i, j: (i // tpm, 0, j))],
        out_specs=pl.BlockSpec((tm, tn), lambda i, j: (i, j)),
        out_shape=jax.ShapeDtypeStruct((m, n), F32),
        compiler_params=_cparams(("parallel", "arbitrary")),
        name="mm_res",
    )(a, w, res, mod)


def _final_norm_kernel(x_ref, w_ref, o_ref):
    x = x_ref[...]
    ms = jnp.mean(x * x, axis=-1, keepdims=True)
    o_ref[...] = x * lax.rsqrt(ms + NORM_EPS) * w_ref[...]


def final_norm(x, w, tm=512):
    m, d = x.shape
    tm = _row_tile(m, tm)
    return pl.pallas_call(
        _final_norm_kernel,
        grid=(m // tm,),
        in_specs=[pl.BlockSpec((tm, d), lambda i: (i, 0)), pl.BlockSpec((1, d), lambda i: (0, 0))],
        out_specs=pl.BlockSpec((tm, d), lambda i: (i, 0)),
        out_shape=jax.ShapeDtypeStruct((m, d), F32),
        compiler_params=_cparams(("parallel",)),
        name="final_norm",
    )(x, w)


CONV_ROWS = 256


def _conv_kernel(p_ref, w_ref, o_ref, s_ref, *, seq, rows, cols, pad, mode):
    zeros = jnp.zeros((pad, LANES), F32)
    for t in range(3):
        s_ref[t, 0:pad, :] = zeros
        s_ref[t, pad + seq:pad + seq + pad, :] = zeros
    s_ref[1, pad:pad + seq, :] = p_ref[...]
    rc = min(CONV_ROWS, seq)
    for c in range(seq // rc):
        base = c * rc
        col = (lax.broadcasted_iota(jnp.int32, (rc, 1), 0) + base) & (cols - 1)
        left = s_ref[1, pad - 1 + base:pad - 1 + base + rc, :]
        right = s_ref[1, pad + 1 + base:pad + 1 + base + rc, :]
        s_ref[0, pad + base:pad + base + rc, :] = jnp.where(col == 0, 0.0, left)
        s_ref[2, pad + base:pad + base + rc, :] = jnp.where(col == cols - 1, 0.0, right)
    drs = (0, 1, 2) if rows > 1 else (1,)
    for c in range(seq // rc):
        base = c * rc
        acc = jnp.zeros((rc, LANES), F32)
        for dr in drs:
            for dc in range(3):
                start = pad + base + (dr - 1) * cols
                acc = acc + s_ref[dc, start:start + rc, :] * w_ref[dr * 3 + dc:dr * 3 + dc + 1, :]
        if mode != "raw":
            acc = _silu(acc)
        if mode in ("q", "k"):
            acc = acc * lax.rsqrt(jnp.sum(acc * acc, axis=-1, keepdims=True) + 1e-6)
        if mode == "q":
            acc = acc * (A_DK ** -0.5)
        o_ref[base:base + rc, :] = acc


def grid_conv(p, conv_w, *, batch, seq, rows, col0, ncol, mode):
    cols = seq // rows
    pad = cols if rows > 1 else SUBLANES
    t0 = col0 // LANES
    return pl.pallas_call(
        functools.partial(_conv_kernel, seq=seq, rows=rows, cols=cols, pad=pad, mode=mode),
        grid=(batch, ncol // LANES),
        in_specs=[pl.BlockSpec((seq, LANES), lambda b, j: (b, t0 + j)),
                  pl.BlockSpec((16, LANES), lambda b, j: (0, t0 + j))],
        out_specs=pl.BlockSpec((seq, LANES), lambda b, j: (b, j)),
        out_shape=jax.ShapeDtypeStruct((batch * seq, ncol), F32),
        scratch_shapes=[pltpu.VMEM((3, seq + 2 * pad, LANES), F32)],
        compiler_params=_cparams(("parallel", "parallel")),
        name="grid_conv_" + mode,
    )(p, conv_w)


def _chunk_masks(c, rev):
    ii = lax.broadcasted_iota(jnp.int32, (c, c), 0)
    jj = lax.broadcasted_iota(jnp.int32, (c, c), 1)
    eye = ii == jj
    incl = (jj >= ii) if rev else (jj <= ii)
    incl_t = (jj <= ii) if rev else (jj >= ii)
    return eye, incl, incl_t


def _col_to_row(col, eye):
    c = col.shape[0]
    return jnp.sum(jnp.where(eye, jnp.broadcast_to(col, (c, c)), 0.0), axis=0, keepdims=True)


def _cumsum_col_row(col, eye, incl, incl_t):
    c = col.shape[0]
    colb = jnp.broadcast_to(col, (c, c))
    row = jnp.sum(jnp.where(eye, colb, 0.0), axis=0, keepdims=True)
    cs_col = jnp.sum(jnp.where(incl, jnp.broadcast_to(row, (c, c)), 0.0), axis=1, keepdims=True)
    cs_row = jnp.sum(jnp.where(incl_t, colb, 0.0), axis=0, keepdims=True)
    return cs_col, cs_row


def _gdn_kernel(q_ref, k_ref, v_ref, g_ref, av_ref, dv_ref, s0_ref, o_ref, s_ref, *, rev, d, heads):
    c = q_ref.shape[0]
    dk = A_DK

    @pl.when(pl.program_id(2) == 0)
    def _():
        s_ref[...] = s0_ref[...]

    eye, incl, incl_t = _chunk_masks(c, rev)
    strict = jnp.logical_and(incl, jnp.logical_not(eye))
    eye_f = eye.astype(F32)
    gl = g_ref[...]
    log_alpha = -jnp.exp(av_ref[...]) * _softplus(gl + dv_ref[...])
    beta_all = _sigmoid(gl)
    hg = pl.program_id(1)
    steps = _nilpotent_steps(c)
    for hh in range(heads):
        sl = slice(hh * dk, (hh + 1) * dk)
        q = q_ref[:, sl]
        k = k_ref[:, sl]
        v = v_ref[:, sl]
        lane = lax.broadcasted_iota(jnp.int32, (1, LANES), 1)
        head = hg * heads + hh
        g = jnp.sum(jnp.where(lane == d * A_HEADS + head, log_alpha, 0.0), axis=1, keepdims=True)
        beta = jnp.sum(jnp.where(lane == 2 * A_HEADS + d * A_HEADS + head, beta_all, 0.0), axis=1, keepdims=True)
        gc_col, gc_row = _cumsum_col_row(g, eye, incl, incl_t)
        gtot = jnp.sum(g, axis=0, keepdims=True)
        decay = jnp.where(incl, jnp.exp(jnp.where(incl, gc_col - gc_row, 0.0)), 0.0)
        kb = k * beta
        m_mat = jnp.where(strict, _dot_nt(kb, k) * decay, 0.0)
        x = _unit_lower_inverse(-m_mat, eye_f, steps)
        egc = jnp.exp(gc_col)
        u = _dot(x, v * beta)
        w = _dot(x, kb * egc)
        s = s_ref[0, hh]
        v_new = u - _dot(w, s)
        qk = jnp.where(incl, _dot_nt(q, k) * decay, 0.0)
        o_ref[:, sl] = _dot(q * egc, s) + _dot(qk, v_new)
        s_ref[0, hh] = s * jnp.exp(gtot) + _dot_tn(k * jnp.exp(gtot - gc_col), v_new)


def gdn_scan(q, k, v, gates, a_vec, dt_vec, s0, *, batch, seq, rev, d, gate_blk, heads=4):
    nc = seq // CHUNK
    ng = A_HEADS // heads
    w = heads * A_DK

    def row(b, c):
        return b * nc + ((nc - 1 - c) if rev else c)

    tok = pl.BlockSpec((CHUNK, w), lambda b, h, c: (row(b, c), h))
    vec = pl.BlockSpec((1, LANES), lambda b, h, c: (0, 0))
    st = pl.BlockSpec((1, heads, A_DK, A_DK), lambda b, h, c: (b, h, 0, 0))
    return pl.pallas_call(
        functools.partial(_gdn_kernel, rev=rev, d=d, heads=heads),
        grid=(batch, ng, nc),
        in_specs=[tok, tok, tok,
                  pl.BlockSpec((CHUNK, LANES), lambda b, h, c: (row(b, c), gate_blk)),
                  vec, vec, st],
        out_specs=[tok, st],
        out_shape=[jax.ShapeDtypeStruct((batch * seq, A_DIM), F32),
                   jax.ShapeDtypeStruct((batch, A_HEADS, A_DK, A_DK), F32)],
        compiler_params=_cparams(("parallel", "parallel", "arbitrary")),
        name="gdn_scan",
    )(q, k, v, gates, a_vec, dt_vec, s0)


def _group_ones(scale):
    ii = lax.broadcasted_iota(jnp.int32, (LANES, LANES), 0)
    jj = lax.broadcasted_iota(jnp.int32, (LANES, LANES), 1)
    sh = int(math.log2(B_N))
    return jnp.where((ii >> sh) == (jj >> sh), scale, 0.0).astype(F32)


def _rw_feat_kernel(ps_ref, kb_ref, wup_ref, aup_ref, gup_ref, w0_ref, a0_ref, kk_w_ref, ka_w_ref,
                    logw_ref, kk_ref, kka_ref, kt_ref, g_ref):
    ps = ps_ref[...]
    kb = kb_ref[...]
    lw = w0_ref[...] + _dot(jnp.tanh(ps[:, 0:2 * B_LORA]), wup_ref[...])
    logw_ref[...] = -B_DECAY_SCALE * _sigmoid(lw)
    a = _sigmoid(a0_ref[...] + _dot(ps[:, 2 * B_LORA:4 * B_LORA], aup_ref[...]))
    g_ref[...] = _dot(_sigmoid(ps[:, 4 * B_LORA:]), gup_ref[...])
    kkw = kb * kk_w_ref[...]
    ones = _group_ones(1.0)
    for t in range(B_DIM // LANES):
        sl = slice(t * LANES, (t + 1) * LANES)
        x = kkw[:, sl]
        ss = _dot_exact_rhs(x * x, ones)
        kk_ref[:, sl] = x * lax.rsqrt(ss + 1e-6)
    kk = kk_ref[...]
    for dd in range(2):
        a_d = a[:, dd * B_DIM:(dd + 1) * B_DIM]
        kka_ref[:, dd * B_DIM:(dd + 1) * B_DIM] = kk * a_d
        kt_ref[:, dd * B_DIM:(dd + 1) * B_DIM] = kb * (1.0 + (a_d - 1.0) * ka_w_ref[...])


def rw_features(ps, rkv, wup, aup, gup, w0, a0, k_k, k_a, tm=256):
    m = ps.shape[0]
    tm = _row_tile(m, tm)
    full = lambda shp: pl.BlockSpec(shp, lambda i: (0, 0))
    two = jax.ShapeDtypeStruct((m, 2 * B_DIM), F32)
    one = jax.ShapeDtypeStruct((m, B_DIM), F32)
    return pl.pallas_call(
        _rw_feat_kernel,
        grid=(m // tm,),
        in_specs=[pl.BlockSpec((tm, 512), lambda i: (i, 0)),
                  pl.BlockSpec((tm, B_DIM), lambda i: (i, 1)),
                  full((2 * B_LORA, 2 * B_DIM)), full((2 * B_LORA, 2 * B_DIM)), full((256, B_DIM)),
                  full((1, 2 * B_DIM)), full((1, 2 * B_DIM)), full((1, B_DIM)), full((1, B_DIM))],
        out_specs=[pl.BlockSpec((tm, 2 * B_DIM), lambda i: (i, 0)),
                   pl.BlockSpec((tm, B_DIM), lambda i: (i, 0)),
                   pl.BlockSpec((tm, 2 * B_DIM), lambda i: (i, 0)),
                   pl.BlockSpec((tm, 2 * B_DIM), lambda i: (i, 0)),
                   pl.BlockSpec((tm, B_DIM), lambda i: (i, 0))],
        out_shape=[two, one, two, two, one],
        compiler_params=_cparams(("parallel",)),
        name="rw_features",
    )(ps, rkv, wup, aup, gup, w0, a0, k_k, k_a)


def _stack_heads(t, lane_lo):
    return jnp.concatenate([jnp.where(lane_lo, t, 0.0), jnp.where(lane_lo, 0.0, t)], axis=0)


def _rwkv_kernel(r_ref, lw_ref, kk_ref, kka_ref, v_ref, kt_ref, s0_ref, y_ref, s_ref, *, rev, pairs):
    c = r_ref.shape[0]
    c2 = 2 * c

    @pl.when(pl.program_id(2) == 0)
    def _():
        s_ref[...] = s0_ref[...]

    eye, incl, _ = _chunk_masks(c, rev)
    ii = lax.broadcasted_iota(jnp.int32, (c2, c2), 0)
    jj = lax.broadcasted_iota(jnp.int32, (c2, c2), 1)
    sh = int(math.log2(c))
    same = (ii >> sh) == (jj >> sh)
    ti, tj = ii & (c - 1), jj & (c - 1)
    order = (tj >= ti) if rev else (tj <= ti)
    incl2 = jnp.logical_and(same, order)
    strict2 = jnp.logical_and(incl2, ii != jj)
    eye2_f = (ii == jj).astype(F32)
    incl_f = incl.astype(F32)
    lane_lo = lax.broadcasted_iota(jnp.int32, (1, LANES), 1) < B_N
    steps = _nilpotent_steps(c)
    for pp in range(pairs):
        sl = slice(pp * LANES, (pp + 1) * LANES)
        lw = lw_ref[:, sl]
        g_in = _dot_exact_lhs(incl_f, lw)
        g_ex = g_in - lw
        gtot = jnp.sum(lw, axis=0, keepdims=True)
        e_neg = jnp.exp(-g_in)
        e_end = jnp.exp(gtot - g_in)
        kk = kk_ref[:, sl]
        kka = kka_ref[:, sl]
        kt = kt_ref[:, sl]
        a_s = _stack_heads(kk * jnp.exp(g_ex), lane_lo)
        b_s = _stack_heads(-kka * e_neg, lane_lo)
        k_s = _stack_heads(kt * e_neg, lane_lo)
        r_s = _stack_heads(r_ref[:, sl] * jnp.exp(g_in), lane_lo)
        v_s = _stack_heads(v_ref[:, sl], lane_lo)
        bt_s = _stack_heads(-kka * e_end, lane_lo)
        kt_s = _stack_heads(kt * e_end, lane_lo)
        s = s_ref[0, pp]
        l_ab = jnp.where(strict2, _dot_nt(a_s, b_s), 0.0)
        l_ak = jnp.where(strict2, _dot_nt(a_s, k_s), 0.0)
        x = _unit_lower_inverse(l_ab, eye2_f, steps)
        u = _dot(x, _dot_nt(a_s, s) + _dot(l_ak, v_s))
        y2 = (_dot_nt(r_s, s)
              + _dot(jnp.where(incl2, _dot_nt(r_s, b_s), 0.0), u)
              + _dot(jnp.where(incl2, _dot_nt(r_s, k_s), 0.0), v_s))
        y_ref[:, sl] = y2[0:c] + y2[c:c2]
        s_ref[0, pp] = s * jnp.exp(gtot) + _dot_tn(u, bt_s) + _dot_tn(v_s, kt_s)


def rwkv_scan(rkv, logw, kk, kka, kt, s0, *, batch, seq, rev, d, pairs=2):
    nc = seq // CHUNK
    npair = B_DIM // LANES
    ng = npair // pairs
    w = pairs * LANES

    def row(b, c):
        return b * nc + ((nc - 1 - c) if rev else c)

    def tok(off_blocks):
        return pl.BlockSpec((CHUNK, w), lambda b, h, c: (row(b, c), off_blocks + h))

    per_dir = d * ng
    st = pl.BlockSpec((1, pairs, LANES, LANES), lambda b, h, c: (b, h, 0, 0))
    return pl.pallas_call(
        functools.partial(_rwkv_kernel, rev=rev, pairs=pairs),
        grid=(batch, ng, nc),
        in_specs=[tok(0), tok(per_dir), tok(0), tok(per_dir), tok(2 * ng), tok(per_dir), st],
        out_specs=[tok(0), st],
        out_shape=[jax.ShapeDtypeStruct((batch * seq, B_DIM), F32),
                   jax.ShapeDtypeStruct((batch, npair, LANES, LANES), F32)],
        compiler_params=_cparams(("parallel", "parallel", "arbitrary")),
        name="rwkv_scan",
    )(rkv, logw, kk, kka, rkv, kt, s0)


def _ab_out_kernel(of_ref, ob_ref, yf_ref, yb_ref, z_ref, r_ref, v_ref, kt_ref, g_ref, x_ref, mod_ref,
                   gnw_ref, rk_ref, lnw_ref, lnb_ref, w_ref, o_ref, mix_ref, *, gate_idx):
    for h in range(A_HEADS):
        sl = slice(h * A_DK, (h + 1) * A_DK)
        o = of_ref[:, sl] + ob_ref[:, sl]
        ms = jnp.mean(o * o, axis=-1, keepdims=True)
        ya = o * lax.rsqrt(ms + NORM_EPS) * gnw_ref[...]
        mix_ref[:, sl] = (ya * _silu(z_ref[:, sl])).astype(BF16)
    avg = _group_ones(1.0 / B_N)
    ones = _group_ones(1.0)
    for t in range(B_DIM // LANES):
        sl = slice(t * LANES, (t + 1) * LANES)
        y = yf_ref[:, sl] + yb_ref[:, sl]
        mu = _dot_exact_rhs(y, avg)
        dlt = y - mu
        var = _dot_exact_rhs(dlt * dlt, avg)
        yn = dlt * lax.rsqrt(var + B_GN_EPS)
        kt_sum = kt_ref[:, sl] + kt_ref[:, B_DIM + t * LANES:B_DIM + (t + 1) * LANES]
        bonus = _dot_exact_rhs(r_ref[:, sl] * kt_sum * rk_ref[:, sl], ones) * v_ref[:, sl]
        yb = (yn * lnw_ref[:, sl] + lnb_ref[:, sl] + bonus) * g_ref[:, sl]
        mix_ref[:, A_DIM + t * LANES:A_DIM + (t + 1) * LANES] = yb.astype(BF16)
    acc = jnp.dot(mix_ref[...], w_ref[...], preferred_element_type=F32)
    o_ref[...] = x_ref[...] + mod_ref[0, gate_idx:gate_idx + 1, :] * acc


def ab_out(o_f, o_b, y_f, y_b, p_main, rkv, kt, g_out, x, mod, gnw, rk, lnw, lnb, w_out, *,
           gate_idx, rows_per_mod, tm=128):
    m, dm = x.shape
    tm = _row_tile(min(m, rows_per_mod), tm)
    tpm = rows_per_mod // tm
    zblk = CONV_CH // A_DIM
    row = lambda width, blk=0: pl.BlockSpec((tm, width), lambda i: (i, blk))
    full = lambda shp: pl.BlockSpec(shp, lambda i: (0, 0))
    return pl.pallas_call(
        functools.partial(_ab_out_kernel, gate_idx=gate_idx),
        grid=(m // tm,),
        in_specs=[row(A_DIM), row(A_DIM), row(B_DIM), row(B_DIM), row(A_DIM, zblk),
                  row(B_DIM, 0), row(B_DIM, 2), row(2 * B_DIM), row(B_DIM), row(dm),
                  pl.BlockSpec((1, MOD_ROWS, dm), lambda i: (i // tpm, 0, 0)),
                  full((1, A_DK)), full((1, B_DIM)), full((1, B_DIM)), full((1, B_DIM)),
                  full((A_DIM + B_DIM, dm))],
        out_specs=row(dm),
        out_shape=jax.ShapeDtypeStruct((m, dm), F32),
        scratch_shapes=[pltpu.VMEM((tm, A_DIM + B_DIM), BF16)],
        compiler_params=_cparams(("parallel",)),
        name="ab_out",
    )(o_f, o_b, y_f, y_b, p_main, rkv, rkv, kt, g_out, x, mod, gnw, rk, lnw, lnb, w_out)


def _mlstm_kernel(q_ref, k_ref, v_ref, g_ref, ib_ref, fb_ref, c0_ref, n0_ref, m0_ref,
                  h_ref, c_ref, n_ref, m_ref, *, rev, d, heads):
    c = q_ref.shape[0]

    @pl.when(pl.program_id(2) == 0)
    def _():
        c_ref[...] = c0_ref[...]
        n_ref[...] = n0_ref[...]
        m_ref[...] = m0_ref[...]

    eye, incl, incl_t = _chunk_masks(c, rev)
    gl = g_ref[...]
    cap = C_GATE_CAP
    ig_all = cap * jnp.tanh((gl + ib_ref[...]) / cap)
    fz = cap * jnp.tanh((gl + fb_ref[...]) / cap)
    logf_all = jnp.minimum(fz, 0.0) - jnp.log(1.0 + jnp.exp(-jnp.abs(fz)))
    lane = lax.broadcasted_iota(jnp.int32, (1, LANES), 1)
    hg = pl.program_id(1)
    for hh in range(heads):
        head = hg * heads + hh
        qs = slice(hh * C_DK, (hh + 1) * C_DK)
        vs = slice(hh * C_DV, (hh + 1) * C_DV)
        q = q_ref[:, qs] * (C_DK ** -0.5)
        k = k_ref[:, qs]
        v = v_ref[:, vs]
        ig = jnp.sum(jnp.where(lane == d * C_HEADS + head, ig_all, 0.0), axis=1, keepdims=True)
        logf = jnp.sum(jnp.where(lane == 2 * C_HEADS + d * C_HEADS + head, logf_all, 0.0), axis=1, keepdims=True)
        b_col, b_row = _cumsum_col_row(logf, eye, incl, incl_t)
        ig_row = _col_to_row(ig, eye)
        btot = jnp.sum(logf, axis=0, keepdims=True)
        m_st = m_ref[0, hh, 0:1, 0:1]
        c_st = c_ref[0, hh]
        n_st = n_ref[0, hh, 0:1, :]
        w_end = btot - b_col + ig
        m_new = jnp.maximum(btot + m_st, jnp.max(w_end, axis=0, keepdims=True))
        scale = jnp.exp(btot + m_st - m_new)
        kw = k * jnp.exp(w_end - m_new)
        dmat = b_col - b_row + ig_row
        dmax = jnp.max(jnp.where(incl, dmat, -1e30), axis=1, keepdims=True)
        m_row = jnp.maximum(b_col + m_st, dmax)
        inter = jnp.exp(b_col + m_st - m_row)
        s = jnp.where(incl, _dot_nt(q, k) * jnp.exp(jnp.where(incl, dmat - m_row, 0.0)), 0.0)
        num = _dot(s, v) + inter * _dot(q, c_st)
        den = jnp.sum(s, axis=1, keepdims=True) + inter * jnp.sum(q * n_st, axis=1, keepdims=True)
        h_ref[:, vs] = num / jnp.maximum(jnp.abs(den), jnp.exp(-m_row))
        c_ref[0, hh] = c_st * scale + _dot_tn(kw, v)
        n_ref[0, hh] = jnp.broadcast_to(n_st * scale + jnp.sum(kw, axis=0, keepdims=True), (SUBLANES, C_DK))
        m_ref[0, hh] = jnp.broadcast_to(m_new, (SUBLANES, LANES))


def mlstm_scan(p, gates, ib_vec, fb_vec, c0, n0, m0, *, batch, seq, rev, d, heads=2):
    nc = seq // CHUNK
    ng = C_HEADS // heads

    def row(b, c):
        return b * nc + ((nc - 1 - c) if rev else c)

    wq, wv = heads * C_DK, heads * C_DV
    qspec = pl.BlockSpec((CHUNK, wq), lambda b, h, c: (row(b, c), h))
    kspec = pl.BlockSpec((CHUNK, wq), lambda b, h, c: (row(b, c), ng + h))
    vspec = pl.BlockSpec((CHUNK, wv), lambda b, h, c: (row(b, c), (2 * C_QK) // wv + h))
    hspec = pl.BlockSpec((CHUNK, wv), lambda b, h, c: (row(b, c), h))
    vec = pl.BlockSpec((1, LANES), lambda b, h, c: (0, 0))
    cst = pl.BlockSpec((1, heads, C_DK, C_DV), lambda b, h, c: (b, h, 0, 0))
    nst = pl.BlockSpec((1, heads, SUBLANES, C_DK), lambda b, h, c: (b, h, 0, 0))
    mst = pl.BlockSpec((1, heads, SUBLANES, LANES), lambda b, h, c: (b, h, 0, 0))
    return pl.pallas_call(
        functools.partial(_mlstm_kernel, rev=rev, d=d, heads=heads),
        grid=(batch, ng, nc),
        in_specs=[qspec, kspec, vspec,
                  pl.BlockSpec((CHUNK, LANES), lambda b, h, c: (row(b, c), 0)),
                  vec, vec, cst, nst, mst],
        out_specs=[hspec, cst, nst, mst],
        out_shape=[jax.ShapeDtypeStruct((batch * seq, C_V), F32),
                   jax.ShapeDtypeStruct((batch, C_HEADS, C_DK, C_DV), F32),
                   jax.ShapeDtypeStruct((batch, C_HEADS, SUBLANES, C_DK), F32),
                   jax.ShapeDtypeStruct((batch, C_HEADS, SUBLANES, LANES), F32)],
        compiler_params=_cparams(("parallel", "parallel", "arbitrary")),
        name="mlstm_scan",
    )(p, p, p, gates, ib_vec, fb_vec, c0, n0, m0)


def _ml_out_kernel(hf_ref, hb_ref, o_ref_in, x_ref, mod_ref, nw_ref, w_ref, out_ref, mix_ref, *, gate_idx):
    for h in range(C_HEADS):
        sl = slice(h * C_DV, (h + 1) * C_DV)
        hh = hf_ref[:, sl] + hb_ref[:, sl]
        ms = jnp.mean(hh * hh, axis=-1, keepdims=True)
        y = hh * lax.rsqrt(ms + NORM_EPS) * nw_ref[:, sl]
        mix_ref[:, sl] = (y * _sigmoid(o_ref_in[:, sl])).astype(BF16)
    acc = jnp.dot(mix_ref[...], w_ref[...], preferred_element_type=F32)
    out_ref[...] = x_ref[...] + mod_ref[0, gate_idx:gate_idx + 1, :] * acc


def ml_out(h_f, h_b, p, x, mod, nw, w_out, *, gate_idx, rows_per_mod, tm=256):
    m, dm = x.shape
    tm = _row_tile(min(m, rows_per_mod), tm)
    tpm = rows_per_mod // tm
    oblk = (2 * C_QK + C_V) // C_V
    row = lambda width, blk=0: pl.BlockSpec((tm, width), lambda i: (i, blk))
    full = lambda shp: pl.BlockSpec(shp, lambda i: (0, 0))
    return pl.pallas_call(
        functools.partial(_ml_out_kernel, gate_idx=gate_idx),
        grid=(m // tm,),
        in_specs=[row(C_V), row(C_V), row(C_V, oblk), row(dm),
                  pl.BlockSpec((1, MOD_ROWS, dm), lambda i: (i // tpm, 0, 0)),
                  full((1, C_V)), full((C_V, dm))],
        out_specs=row(dm),
        out_shape=jax.ShapeDtypeStruct((m, dm), F32),
        scratch_shapes=[pltpu.VMEM((tm, C_V), BF16)],
        compiler_params=_cparams(("parallel",)),
        name="ml_out",
    )(h_f, h_b, p, x, mod, nw, w_out)


def _lane_vec(values, offset):
    flat = values.reshape(-1).astype(F32)
    return jnp.zeros((1, LANES), F32).at[0, offset:offset + flat.shape[0]].set(flat)


def _ffn(x, mod, nw, w1, w3, w2, rows_per_mod):
    g = ffn_up(x, mod, nw, w1, w3, shift_idx=3, scale_idx=4, rows_per_mod=rows_per_mod)
    return mm_res(g, w2, x, mod, gate_idx=5, rows_per_mod=rows_per_mod)


def _ab_layer(streams, prm, ctx_out):
    w_in = prm["w_in"]
    w_main = w_in[:, :CONV_CH + A_DIM].astype(BF16)
    small = w_in[:, CONV_CH + A_DIM + 4 * A_HEADS:]
    gates_w = w_in[:, CONV_CH + A_DIM:CONV_CH + A_DIM + 4 * A_HEADS]
    d_model = w_in.shape[0]
    w_small = jnp.concatenate(
        [small, jnp.zeros((d_model, 512 - small.shape[1]), F32),
         gates_w, jnp.zeros((d_model, LANES - gates_w.shape[1]), F32)], axis=1).astype(BF16)
    conv_w = jnp.concatenate([prm["conv_w"].reshape(9, CONV_CH), jnp.zeros((7, CONV_CH), F32)], axis=0)
    a_vec = _lane_vec(prm["a_log"], 0)
    dt_vec = _lane_vec(prm["dt_bias"], 0)
    zero_up = jnp.zeros((B_LORA, B_DIM), F32)
    blockdiag = lambda u: jnp.concatenate(
        [jnp.concatenate([u[0], zero_up], axis=1), jnp.concatenate([zero_up, u[1]], axis=1)], axis=0).astype(BF16)
    wup, aup = blockdiag(prm["w_up"]), blockdiag(prm["a_up"])
    gup = jnp.concatenate([prm["g_up"], jnp.zeros((256 - B_G_LORA, B_DIM), F32)], axis=0).astype(BF16)
    w0 = prm["w0"].reshape(1, 2 * B_DIM)
    a0 = prm["a0"].reshape(1, 2 * B_DIM)
    k_k = prm["k_k"].reshape(1, B_DIM)
    k_a = prm["k_a"].reshape(1, B_DIM)
    gnw = prm["gdn_norm_w"].reshape(1, A_DK)
    rk = prm["r_k"].reshape(1, B_DIM)
    lnw = prm["ln_w"].reshape(1, B_DIM)
    lnb = prm["ln_b"].reshape(1, B_DIM)
    w_out = prm["w_out"].astype(BF16)

    feats = []
    for st in streams:
        kw = dict(shift_idx=0, scale_idx=1, rows_per_mod=st["rpm"])
        p_main = modmm(st["x"], st["mod"], prm["norm_w"], w_main, **kw)
        p_small = modmm(st["x"], st["mod"], prm["norm_w"], w_small, tn=640, **kw)
        ckw = dict(batch=st["batch"], seq=st["seq"], rows=st["rows"])
        q = grid_conv(p_main, conv_w, col0=0, ncol=A_DIM, mode="q", **ckw)
        k = grid_conv(p_main, conv_w, col0=A_DIM, ncol=A_DIM, mode="k", **ckw)
        v = grid_conv(p_main, conv_w, col0=2 * A_DIM, ncol=A_DIM, mode="v", **ckw)
        rkv = grid_conv(p_main, conv_w, col0=3 * A_DIM, ncol=3 * B_DIM, mode="raw", **ckw)
        logw, kk, kka, kt, g_out = rw_features(p_small, rkv, wup, aup, gup, w0, a0, k_k, k_a)
        feats.append(dict(p_main=p_main, p_small=p_small, q=q, k=k, v=v, rkv=rkv,
                          logw=logw, kk=kk, kka=kka, kt=kt, g_out=g_out))

    nb = streams[-1]["batch"]
    outs = [dict() for _ in streams]
    for d in range(2):
        rev = d == 1
        s_a = jnp.zeros((nb, A_HEADS, A_DK, A_DK), F32)
        s_b = jnp.zeros((nb, B_DIM // LANES, LANES, LANES), F32)
        for si, (st, f) in enumerate(zip(streams, feats)):
            skw = dict(batch=st["batch"], seq=st["seq"], rev=rev, d=d)
            o, s_a = gdn_scan(f["q"], f["k"], f["v"], f["p_small"], a_vec, dt_vec, s_a, gate_blk=4, **skw)
            y, s_b = rwkv_scan(f["rkv"], f["logw"], f["kk"], f["kka"], f["kt"], s_b, **skw)
            outs[si]["o%d" % d] = o
            outs[si]["y%d" % d] = y

    new_x = []
    for si, (st, f) in enumerate(zip(streams, feats)):
        if si == 0 and not ctx_out:
            new_x.append(None)
            continue
        o = outs[si]
        new_x.append(ab_out(o["o0"], o["o1"], o["y0"], o["y1"], f["p_main"], f["rkv"], f["kt"], f["g_out"],
                            st["x"], st["mod"], gnw, rk, lnw, lnb, w_out, gate_idx=2, rows_per_mod=st["rpm"]))
    return new_x


def _ml_layer(streams, prm, ctx_out):
    w_in = prm["w_in"]
    main_cols = 2 * C_QK + 2 * C_V
    w_main = w_in[:, :main_cols].astype(BF16)
    d_model = w_in.shape[0]
    w_g = jnp.concatenate([w_in[:, main_cols:], jnp.zeros((d_model, LANES - 4 * C_HEADS), F32)], axis=1).astype(BF16)
    ib_vec = _lane_vec(prm["i_bias"], 0)
    fb_vec = _lane_vec(prm["f_bias"], 2 * C_HEADS)
    nw = prm["ml_norm_w"].reshape(1, C_V)
    w_out = prm["w_out"].astype(BF16)

    feats = []
    for st in streams:
        kw = dict(shift_idx=0, scale_idx=1, rows_per_mod=st["rpm"])
        p_main = modmm(st["x"], st["mod"], prm["norm_w"], w_main, **kw)
        p_g = modmm(st["x"], st["mod"], prm["norm_w"], w_g, tn=LANES, **kw)
        feats.append(dict(p_main=p_main, p_g=p_g))

    nb = streams[-1]["batch"]
    outs = [dict() for _ in streams]
    for d in range(2):
        rev = d == 1
        c_st = jnp.zeros((nb, C_HEADS, C_DK, C_DV), F32)
        n_st = jnp.zeros((nb, C_HEADS, SUBLANES, C_DK), F32)
        m_st = jnp.zeros((nb, C_HEADS, SUBLANES, LANES), F32)
        for si, (st, f) in enumerate(zip(streams, feats)):
            h, c_st, n_st, m_st = mlstm_scan(f["p_main"], f["p_g"], ib_vec, fb_vec, c_st, n_st, m_st,
                                             batch=st["batch"], seq=st["seq"], rev=rev, d=d)
            outs[si]["h%d" % d] = h

    new_x = []
    for si, (st, f) in enumerate(zip(streams, feats)):
        if si == 0 and not ctx_out:
            new_x.append(None)
            continue
        new_x.append(ml_out(outs[si]["h0"], outs[si]["h1"], f["p_main"], st["x"], st["mod"], nw, w_out,
                            gate_idx=2, rows_per_mod=st["rpm"]))
    return new_x


def kernel(x, c, ctx, c_ctx, ada_w, ada_b, norm_w, ab_w_in, ab_conv_w, gdn_a_log, gdn_dt_bias, gdn_norm_w, rw_w0, rw_w_up, rw_a0, rw_a_up, rw_g_up, rw_k_k, rw_k_a, rw_r_k, rw_ln_w, rw_ln_b, ab_w_out, ml_w_in, ml_i_bias, ml_f_bias, ml_norm_w, ml_w_out, ffn_w1, ffn_w3, ffn_w2, final_norm_w):
    bsz, seq, dm = x.shape
    ctx_len = ctx.shape[1]
    depth = ada_w.shape[0]
    xl = x.reshape(bsz * seq, dm)
    xc = ctx.reshape(bsz * ctx_len, dm)
    cond = jnp.zeros((MOD_ROWS, dm), F32).at[:bsz].set(c).at[bsz].set(c_ctx)

    for i in range(depth):
        ctx_out = i < depth - 1
        j = i // 2
        mod_all = adaln(cond, ada_w[i], ada_b[i].reshape(1, -1)).reshape(MOD_ROWS, 6, dm)
        mod_all = jnp.pad(mod_all, ((0, 0), (0, MOD_ROWS - 6), (0, 0)))
        streams = [
            dict(x=xc, mod=mod_all[bsz:bsz + 1], batch=bsz, seq=ctx_len, rows=1, rpm=bsz * ctx_len),
            dict(x=xl, mod=mod_all[:bsz], batch=bsz, seq=seq, rows=seq // GRID_W, rpm=seq),
        ]
        if i % 2 == 0:
            prm = dict(w_in=ab_w_in[j], conv_w=ab_conv_w[j], a_log=gdn_a_log[j], dt_bias=gdn_dt_bias[j],
                       gdn_norm_w=gdn_norm_w[j], w0=rw_w0[j], w_up=rw_w_up[j], a0=rw_a0[j], a_up=rw_a_up[j],
                       g_up=rw_g_up[j], k_k=rw_k_k[j], k_a=rw_k_a[j], r_k=rw_r_k[j], ln_w=rw_ln_w[j],
                       ln_b=rw_ln_b[j], w_out=ab_w_out[j], norm_w=norm_w[i, 0].reshape(1, dm))
            xc_new, xl = _ab_layer(streams, prm, ctx_out)
        else:
            prm = dict(w_in=ml_w_in[j], i_bias=ml_i_bias[j], f_bias=ml_f_bias[j], ml_norm_w=ml_norm_w[j],
                       w_out=ml_w_out[j], norm_w=norm_w[i, 0].reshape(1, dm))
            xc_new, xl = _ml_layer(streams, prm, ctx_out)
        w1, w3, w2 = ffn_w1[i].astype(BF16), ffn_w3[i].astype(BF16), ffn_w2[i].astype(BF16)
        nw2 = norm_w[i, 1].reshape(1, dm)
        xl = _ffn(xl, streams[1]["mod"], nw2, w1, w3, w2, streams[1]["rpm"])
        if ctx_out:
            xc = _ffn(xc_new, streams[0]["mod"], nw2, w1, w3, w2, streams[0]["rpm"])
    return final_norm(xl, final_norm_w.reshape(1, dm)).reshape(bsz, seq, dm)
```

```python
import functools
import math

import numpy as np
import jax
import jax.numpy as jnp
from jax import lax
from jax.experimental import pallas as pl
from jax.experimental.pallas import tpu as pltpu

F32 = jnp.float32
BF16 = jnp.bfloat16

NORM_EPS = 1e-6
GRID_W = 64
LANES = 128
SUBLANES = 8
VMEM_LIMIT = 56 * 1024 * 1024

A_HEADS, A_DK = 8, 128
A_DIM = A_HEADS * A_DK
B_HEADS, B_N = 16, 64
B_DIM = B_HEADS * B_N
B_LORA = 64
B_G_LORA = 160
B_GN_EPS = 64e-5
B_DECAY_SCALE = math.exp(-0.5)
C_HEADS, C_DK, C_DV = 8, 128, 256
C_QK = C_HEADS * C_DK
C_V = C_HEADS * C_DV
C_GATE_CAP = 15.0
CONV_CH = 3 * A_DIM + 3 * B_DIM
CHUNK = 64
MOD_ROWS = 8


def _cparams(sem):
    return pltpu.CompilerParams(dimension_semantics=sem, vmem_limit_bytes=VMEM_LIMIT)


def _dot(a, b):
    return jnp.dot(a.astype(BF16), b.astype(BF16), preferred_element_type=F32)


def _dot_nt(a, b):
    return lax.dot_general(a.astype(BF16), b.astype(BF16), (((1,), (1,)), ((), ())),
                           preferred_element_type=F32)


def _dot_tn(a, b):
    return lax.dot_general(a.astype(BF16), b.astype(BF16), (((0,), (0,)), ((), ())),
                           preferred_element_type=F32)


def _split2(a):
    hi = a.astype(BF16)
    lo = (a - hi.astype(F32)).astype(BF16)
    return hi, lo


def _split3(a):
    hi = a.astype(BF16)
    r = a - hi.astype(F32)
    mid = r.astype(BF16)
    lo = (r - mid.astype(F32)).astype(BF16)
    return hi, mid, lo


def _dot_exact_rhs(a, b_exact):
    hi, mid, lo = _split3(a)
    b = b_exact.astype(BF16)
    d = lambda t: jnp.dot(t, b, preferred_element_type=F32)
    return d(hi) + d(mid) + d(lo)


def _dot_exact_lhs(a_exact, b):
    hi, mid, lo = _split3(b)
    a = a_exact.astype(BF16)
    d = lambda t: jnp.dot(a, t, preferred_element_type=F32)
    return d(hi) + d(mid) + d(lo)


def _dot3(a, b):
    ah, al = _split2(a)
    bh, bl = _split2(b)
    d = lambda s, t: jnp.dot(s, t, preferred_element_type=F32)
    return d(ah, bh) + d(al, bh) + d(ah, bl)


def _sigmoid(t):
    return 1.0 / (1.0 + jnp.exp(-t))


def _silu(t):
    return t * _sigmoid(t)


def _softplus(t):
    return jnp.maximum(t, 0.0) + jnp.log(1.0 + jnp.exp(-jnp.abs(t)))


def _unit_lower_inverse(n_mat, eye_f, steps):
    x = eye_f + n_mat
    pw = n_mat
    for _ in range(steps):
        pw = _dot(pw, pw)
        x = x + _dot(x, pw)
    return x


def _unit_lower_inverse_many(n_mats, eye_f, steps):
    xs = [eye_f + n for n in n_mats]
    pws = list(n_mats)
    for _ in range(steps):
        pws = [_dot(p, p) for p in pws]
        xs = [x + _dot(x, p) for x, p in zip(xs, pws)]
    return xs


def _pair_masks(c, rev):
    c2 = 2 * c
    ii = lax.broadcasted_iota(jnp.int32, (c2, c2), 0)
    jj = lax.broadcasted_iota(jnp.int32, (c2, c2), 1)
    sh = int(math.log2(c))
    same = (ii >> sh) == (jj >> sh)
    ti, tj = ii & (c - 1), jj & (c - 1)
    incl2 = jnp.logical_and(same, (tj >= ti) if rev else (tj <= ti))
    incl2_t = jnp.logical_and(same, (tj <= ti) if rev else (tj >= ti))
    eye2 = ii == jj
    strict2 = jnp.logical_and(incl2, jnp.logical_not(eye2))
    return eye2, incl2, incl2_t, strict2


def _nilpotent_steps(c):
    return max(int(math.ceil(math.log2(c))) - 1, 0)


def _adaln_kernel(c_ref, w_ref, b_ref, o_ref):
    o_ref[...] = _dot3(_silu(c_ref[...]), w_ref[...]) + b_ref[...]


def adaln(cond, w, b):
    m, d = cond.shape
    e = w.shape[1]
    tn = _col_tile(e, 512)
    return pl.pallas_call(
        _adaln_kernel,
        grid=(e // tn,),
        in_specs=[pl.BlockSpec((m, d), lambda j: (0, 0)),
                  pl.BlockSpec((d, tn), lambda j: (0, j)),
                  pl.BlockSpec((1, tn), lambda j: (0, j))],
        out_specs=pl.BlockSpec((m, tn), lambda j: (0, j)),
        out_shape=jax.ShapeDtypeStruct((m, e), F32),
        compiler_params=_cparams(("parallel",)),
        name="adaln",
    )(cond, w, b)


def _modulated(x_ref, mod_ref, nw_ref, shift_idx, scale_idx):
    x = x_ref[...]
    ms = jnp.mean(x * x, axis=-1, keepdims=True)
    y = x * lax.rsqrt(ms + NORM_EPS) * nw_ref[...]
    return y * (1.0 + mod_ref[0, scale_idx:scale_idx + 1, :]) + mod_ref[0, shift_idx:shift_idx + 1, :]


def _modmm_kernel(x_ref, mod_ref, nw_ref, w_ref, o_ref, h_ref, *, shift_idx, scale_idx):
    @pl.when(pl.program_id(1) == 0)
    def _():
        h_ref[...] = _modulated(x_ref, mod_ref, nw_ref, shift_idx, scale_idx).astype(BF16)

    o_ref[...] = jnp.dot(h_ref[...], w_ref[...], preferred_element_type=F32)


def _row_tile(m, want):
    t = min(want, m)
    while m % t:
        t //= 2
    return t


def _col_tile(n, want):
    t = min(want, n)
    while n % t or t % LANES:
        t -= LANES
    return t


def modmm(x, mod, nw, w, *, shift_idx, scale_idx, rows_per_mod, tm=512, tn=512):
    m, d = x.shape
    n = w.shape[1]
    tm = _row_tile(min(m, rows_per_mod), tm)
    tn = _col_tile(n, tn)
    tpm = rows_per_mod // tm
    return pl.pallas_call(
        functools.partial(_modmm_kernel, shift_idx=shift_idx, scale_idx=scale_idx),
        grid=(m // tm, n // tn),
        in_specs=[pl.BlockSpec((tm, d), lambda i, j: (i, 0)),
                  pl.BlockSpec((1, MOD_ROWS, d), lambda i, j: (i // tpm, 0, 0)),
                  pl.BlockSpec((1, d), lambda i, j: (0, 0)),
                  pl.BlockSpec((d, tn), lambda i, j: (0, j))],
        out_specs=pl.BlockSpec((tm, tn), lambda i, j: (i, j)),
        out_shape=jax.ShapeDtypeStruct((m, n), F32),
        scratch_shapes=[pltpu.VMEM((tm, d), BF16)],
        compiler_params=_cparams(("parallel", "arbitrary")),
        name="modmm",
    )(x, mod, nw, w)


def _ffn_up_kernel(x_ref, mod_ref, nw_ref, w1_ref, w3_ref, o_ref, h_ref, *, shift_idx, scale_idx):
    @pl.when(pl.program_id(1) == 0)
    def _():
        h_ref[...] = _modulated(x_ref, mod_ref, nw_ref, shift_idx, scale_idx).astype(BF16)

    h = h_ref[...]
    a = jnp.dot(h, w1_ref[...], preferred_element_type=F32)
    b = jnp.dot(h, w3_ref[...], preferred_element_type=F32)
    o_ref[...] = (_silu(a) * b).astype(BF16)


def ffn_up(x, mod, nw, w1, w3, *, shift_idx, scale_idx, rows_per_mod, tm=512, tn=512):
    m, d = x.shape
    n = w1.shape[1]
    tm = _row_tile(min(m, rows_per_mod), tm)
    tn = _col_tile(n, tn)
    tpm = rows_per_mod // tm
    return pl.pallas_call(
        functools.partial(_ffn_up_kernel, shift_idx=shift_idx, scale_idx=scale_idx),
        grid=(m // tm, n // tn),
        in_specs=[pl.BlockSpec((tm, d), lambda i, j: (i, 0)),
                  pl.BlockSpec((1, MOD_ROWS, d), lambda i, j: (i // tpm, 0, 0)),
                  pl.BlockSpec((1, d), lambda i, j: (0, 0)),
                  pl.BlockSpec((d, tn), lambda i, j: (0, j)),
                  pl.BlockSpec((d, tn), lambda i, j: (0, j))],
        out_specs=pl.BlockSpec((tm, tn), lambda i, j: (i, j)),
        out_shape=jax.ShapeDtypeStruct((m, n), BF16),
        scratch_shapes=[pltpu.VMEM((tm, d), BF16)],
        compiler_params=_cparams(("parallel", "arbitrary")),
        name="ffn_up",
    )(x, mod, nw, w1, w3)


def _mm_res_kernel(a_ref, w_ref, res_ref, mod_ref, o_ref, *, gate_idx):
    acc = jnp.dot(a_ref[...], w_ref[...], preferred_element_type=F32)
    o_ref[...] = res_ref[...] + mod_ref[0, gate_idx:gate_idx + 1, :] * acc


def mm_res(a, w, res, mod, *, gate_idx, rows_per_mod, tm=512, tn=512):
    m, k = a.shape
    n = w.shape[1]
    tm = _row_tile(min(m, rows_per_mod), tm)
    tn = _col_tile(n, tn)
    tpm = rows_per_mod // tm
    return pl.pallas_call(
        functools.partial(_mm_res_kernel, gate_idx=gate_idx),
        grid=(m // tm, n // tn),
        in_specs=[pl.BlockSpec((tm, k), lambda i, j: (i, 0)),
                  pl.BlockSpec((k, tn), lambda i, j: (0, j)),
                  pl.BlockSpec((tm, tn), lambda i, j: (i, j)),
                  pl.BlockSpec((1, MOD_ROWS, tn), lambda i, j: (i // tpm, 0, j))],
        out_specs=pl.BlockSpec((tm, tn), lambda i, j: (i, j)),
        out_shape=jax.ShapeDtypeStruct((m, n), F32),
        compiler_params=_cparams(("parallel", "arbitrary")),
        name="mm_res",
    )(a, w, res, mod)


def _final_norm_kernel(x_ref, w_ref, o_ref):
    x = x_ref[...]
    ms = jnp.mean(x * x, axis=-1, keepdims=True)
    o_ref[...] = x * lax.rsqrt(ms + NORM_EPS) * w_ref[...]


def final_norm(x, w, tm=512):
    m, d = x.shape
    tm = _row_tile(m, tm)
    return pl.pallas_call(
        _final_norm_kernel,
        grid=(m // tm,),
        in_specs=[pl.BlockSpec((tm, d), lambda i: (i, 0)), pl.BlockSpec((1, d), lambda i: (0, 0))],
        out_specs=pl.BlockSpec((tm, d), lambda i: (i, 0)),
        out_shape=jax.ShapeDtypeStruct((m, d), F32),
        compiler_params=_cparams(("parallel",)),
        name="final_norm",
    )(x, w)


CONV_ROWS = 256


def _conv_kernel(p_ref, w_ref, o_ref, s_ref, *, seq, rows, cols, pad, mode):
    zeros = jnp.zeros((pad, LANES), F32)
    for t in range(3):
        s_ref[t, 0:pad, :] = zeros
        s_ref[t, pad + seq:pad + seq + pad, :] = zeros
    s_ref[1, pad:pad + seq, :] = p_ref[...]
    rc = min(CONV_ROWS, seq)
    for c in range(seq // rc):
        base = c * rc
        col = (lax.broadcasted_iota(jnp.int32, (rc, 1), 0) + base) & (cols - 1)
        left = s_ref[1, pad - 1 + base:pad - 1 + base + rc, :]
        right = s_ref[1, pad + 1 + base:pad + 1 + base + rc, :]
        s_ref[0, pad + base:pad + base + rc, :] = jnp.where(col == 0, 0.0, left)
        s_ref[2, pad + base:pad + base + rc, :] = jnp.where(col == cols - 1, 0.0, right)
    drs = (0, 1, 2) if rows > 1 else (1,)
    for c in range(seq // rc):
        base = c * rc
        acc = jnp.zeros((rc, LANES), F32)
        for dr in drs:
            for dc in range(3):
                start = pad + base + (dr - 1) * cols
                acc = acc + s_ref[dc, start:start + rc, :] * w_ref[dr * 3 + dc:dr * 3 + dc + 1, :]
        if mode != "raw":
            acc = _silu(acc)
        if mode in ("q", "k"):
            acc = acc * lax.rsqrt(jnp.sum(acc * acc, axis=-1, keepdims=True) + 1e-6)
        if mode == "q":
            acc = acc * (A_DK ** -0.5)
        o_ref[base:base + rc, :] = acc


def grid_conv(p, conv_w, *, batch, seq, rows, col0, ncol, mode):
    cols = seq // rows
    pad = cols if rows > 1 else SUBLANES
    t0 = col0 // LANES
    return pl.pallas_call(
        functools.partial(_conv_kernel, seq=seq, rows=rows, cols=cols, pad=pad, mode=mode),
        grid=(batch, ncol // LANES),
        in_specs=[pl.BlockSpec((seq, LANES), lambda b, j: (b, t0 + j)),
                  pl.BlockSpec((16, LANES), lambda b, j: (0, t0 + j))],
        out_specs=pl.BlockSpec((seq, LANES), lambda b, j: (b, j)),
        out_shape=jax.ShapeDtypeStruct((batch * seq, ncol), F32),
        scratch_shapes=[pltpu.VMEM((3, seq + 2 * pad, LANES), F32)],
        compiler_params=_cparams(("parallel", "parallel")),
        name="grid_conv_" + mode,
    )(p, conv_w)


def _chunk_masks(c, rev):
    ii = lax.broadcasted_iota(jnp.int32, (c, c), 0)
    jj = lax.broadcasted_iota(jnp.int32, (c, c), 1)
    eye = ii == jj
    incl = (jj >= ii) if rev else (jj <= ii)
    incl_t = (jj <= ii) if rev else (jj >= ii)
    return eye, incl, incl_t


def _col_to_row(col, eye):
    c = col.shape[0]
    return jnp.sum(jnp.where(eye, jnp.broadcast_to(col, (c, c)), 0.0), axis=0, keepdims=True)


def _cumsum_col_row(col, eye, incl, incl_t):
    c = col.shape[0]
    colb = jnp.broadcast_to(col, (c, c))
    row = jnp.sum(jnp.where(eye, colb, 0.0), axis=0, keepdims=True)
    cs_col = jnp.sum(jnp.where(incl, jnp.broadcast_to(row, (c, c)), 0.0), axis=1, keepdims=True)
    cs_row = jnp.sum(jnp.where(incl_t, colb, 0.0), axis=0, keepdims=True)
    return cs_col, cs_row


def _gdn_kernel(q_ref, k_ref, v_ref, g_ref, av_ref, dv_ref, s0_ref, o_ref, s_ref, *, rev, d, heads):
    c = q_ref.shape[0]
    dk = A_DK

    @pl.when(pl.program_id(2) == 0)
    def _():
        s_ref[...] = s0_ref[...]

    c2 = 2 * c
    eye2, incl2, incl2_t, strict2 = _pair_masks(c, rev)
    eye2_f = eye2.astype(F32)
    gl = g_ref[...]
    log_alpha = -jnp.exp(av_ref[...]) * _softplus(gl + dv_ref[...])
    beta_all = _sigmoid(gl)
    lane = lax.broadcasted_iota(jnp.int32, (1, LANES), 1)
    top = lax.broadcasted_iota(jnp.int32, (c2, 1), 0) < c
    hg = pl.program_id(1)
    steps = _nilpotent_steps(c)
    rng = range(heads // 2)
    gate_col = lambda mat, idx: jnp.sum(jnp.where(lane == idx, mat, 0.0), axis=1, keepdims=True)

    bcast = lambda col: jnp.broadcast_to(col, (c2, c2))
    stack = lambda ref, p: jnp.concatenate([ref[:, 2 * p * dk:(2 * p + 1) * dk],
                                            ref[:, (2 * p + 1) * dk:(2 * p + 2) * dk]], axis=0)
    h0 = [hg * heads + 2 * p for p in rng]
    gts = [(gate_col(log_alpha, d * A_HEADS + h0[p]), gate_col(log_alpha, d * A_HEADS + h0[p] + 1)) for p in rng]
    g = [jnp.concatenate(gts[p], axis=0) for p in rng]
    beta = [jnp.concatenate([gate_col(beta_all, 2 * A_HEADS + d * A_HEADS + h0[p]),
                             gate_col(beta_all, 2 * A_HEADS + d * A_HEADS + h0[p] + 1)], axis=0) for p in rng]
    g_row = [jnp.sum(jnp.where(eye2, bcast(g[p]), 0.0), axis=0, keepdims=True) for p in rng]
    gc_row = [jnp.sum(jnp.where(incl2_t, bcast(g[p]), 0.0), axis=0, keepdims=True) for p in rng]
    gc_col = [jnp.sum(jnp.where(incl2, jnp.broadcast_to(g_row[p], (c2, c2)), 0.0), axis=1, keepdims=True) for p in rng]
    gts = [(jnp.sum(gts[p][0], axis=0, keepdims=True), jnp.sum(gts[p][1], axis=0, keepdims=True)) for p in rng]
    gtot_col = [jnp.where(top, gts[p][0], gts[p][1]) for p in rng]
    decay = [jnp.where(incl2, jnp.exp(jnp.where(incl2, gc_col[p] - gc_row[p], 0.0)), 0.0) for p in rng]
    egc = [jnp.exp(gc_col[p]) for p in rng]
    k = [stack(k_ref, p) for p in rng]
    kb = [k[p] * beta[p] for p in rng]
    qe, big = [], []
    for p in rng:
        q = stack(q_ref, p)
        big.append(_dot_nt(jnp.concatenate([kb[p], q], axis=0), k[p]))
        qe.append(q * egc[p])
    m_neg = [jnp.where(strict2, -(big[p][0:c2] * decay[p]), 0.0) for p in rng]
    qk = [jnp.where(incl2, big[p][c2:2 * c2] * decay[p], 0.0) for p in rng]
    rhs = [jnp.concatenate([stack(v_ref, p) * beta[p], kb[p] * egc[p]], axis=1) for p in rng]
    ktil = [k[p] * jnp.exp(gtot_col[p] - gc_col[p]) for p in rng]
    xs = _unit_lower_inverse_many(m_neg, eye2_f, steps)
    uw = [_dot(xs[p], rhs[p]) for p in rng]
    s_old = [s_ref[0, h] for h in range(heads)]
    halves = (slice(0, c), slice(c, c2))
    ws_qs = [[_dot(jnp.concatenate([uw[p][r, dk:2 * dk], qe[p][r]], axis=0), s_old[2 * p + j])
              for j, r in enumerate(halves)] for p in rng]
    v_new = [uw[p][:, 0:dk] - jnp.concatenate([ws_qs[p][0][0:c], ws_qs[p][1][0:c]], axis=0) for p in rng]
    for p in rng:
        o = jnp.concatenate([ws_qs[p][0][c:c2], ws_qs[p][1][c:c2]], axis=0) + _dot(qk[p], v_new[p])
        o_ref[:, 2 * p * dk:(2 * p + 1) * dk] = o[0:c]
        o_ref[:, (2 * p + 1) * dk:(2 * p + 2) * dk] = o[c:c2]
    for p in rng:
        for j, r in enumerate(halves):
            h = 2 * p + j
            s_ref[0, h] = s_old[h] * jnp.exp(gts[p][j]) + _dot_tn(ktil[p][r], v_new[p][r])


def gdn_scan(q, k, v, gates, a_vec, dt_vec, s0, *, batch, seq, rev, d, gate_blk, heads=8):
    nc = seq // CHUNK
    ng = A_HEADS // heads
    w = heads * A_DK

    def row(b, c):
        return b * nc + ((nc - 1 - c) if rev else c)

    tok = pl.BlockSpec((CHUNK, w), lambda b, h, c: (row(b, c), h))
    vec = pl.BlockSpec((1, LANES), lambda b, h, c: (0, 0))
    st = pl.BlockSpec((1, heads, A_DK, A_DK), lambda b, h, c: (b, h, 0, 0))
    return pl.pallas_call(
        functools.partial(_gdn_kernel, rev=rev, d=d, heads=heads),
        grid=(batch, ng, nc),
        in_specs=[tok, tok, tok,
                  pl.BlockSpec((CHUNK, LANES), lambda b, h, c: (row(b, c), gate_blk)),
                  vec, vec, st],
        out_specs=[tok, st],
        out_shape=[jax.ShapeDtypeStruct((batch * seq, A_DIM), F32),
                   jax.ShapeDtypeStruct((batch, A_HEADS, A_DK, A_DK), F32)],
        compiler_params=_cparams(("parallel", "parallel", "arbitrary")),
        name="gdn_scan",
    )(q, k, v, gates, a_vec, dt_vec, s0)


def _group_ones(scale):
    ii = lax.broadcasted_iota(jnp.int32, (LANES, LANES), 0)
    jj = lax.broadcasted_iota(jnp.int32, (LANES, LANES), 1)
    sh = int(math.log2(B_N))
    return jnp.where((ii >> sh) == (jj >> sh), scale, 0.0).astype(F32)


def _rw_feat_kernel(ps_ref, kb_ref, wup_ref, aup_ref, gup_ref, w0_ref, a0_ref, kk_w_ref, ka_w_ref,
                    logw_ref, kk_ref, kka_ref, kt_ref, g_ref):
    ps = ps_ref[...]
    kb = kb_ref[...]
    lw = w0_ref[...] + _dot(jnp.tanh(ps[:, 0:2 * B_LORA]), wup_ref[...])
    logw_ref[...] = -B_DECAY_SCALE * _sigmoid(lw)
    a = _sigmoid(a0_ref[...] + _dot(ps[:, 2 * B_LORA:4 * B_LORA], aup_ref[...]))
    g_ref[...] = _dot(_sigmoid(ps[:, 4 * B_LORA:]), gup_ref[...])
    kkw = kb * kk_w_ref[...]
    ones = _group_ones(1.0)
    for t in range(B_DIM // LANES):
        sl = slice(t * LANES, (t + 1) * LANES)
        x = kkw[:, sl]
        ss = _dot_exact_rhs(x * x, ones)
        kk_ref[:, sl] = x * lax.rsqrt(ss + 1e-6)
    kk = kk_ref[...]
    for dd in range(2):
        a_d = a[:, dd * B_DIM:(dd + 1) * B_DIM]
        kka_ref[:, dd * B_DIM:(dd + 1) * B_DIM] = kk * a_d
        kt_ref[:, dd * B_DIM:(dd + 1) * B_DIM] = kb * (1.0 + (a_d - 1.0) * ka_w_ref[...])


def rw_features(ps, rkv, wup, aup, gup, w0, a0, k_k, k_a, tm=256):
    m = ps.shape[0]
    tm = _row_tile(m, tm)
    full = lambda shp: pl.BlockSpec(shp, lambda i: (0, 0))
    two = jax.ShapeDtypeStruct((m, 2 * B_DIM), F32)
    one = jax.ShapeDtypeStruct((m, B_DIM), F32)
    return pl.pallas_call(
        _rw_feat_kernel,
        grid=(m // tm,),
        in_specs=[pl.BlockSpec((tm, 512), lambda i: (i, 0)),
                  pl.BlockSpec((tm, B_DIM), lambda i: (i, 1)),
                  full((2 * B_LORA, 2 * B_DIM)), full((2 * B_LORA, 2 * B_DIM)), full((256, B_DIM)),
                  full((1, 2 * B_DIM)), full((1, 2 * B_DIM)), full((1, B_DIM)), full((1, B_DIM))],
        out_specs=[pl.BlockSpec((tm, 2 * B_DIM), lambda i: (i, 0)),
                   pl.BlockSpec((tm, B_DIM), lambda i: (i, 0)),
                   pl.BlockSpec((tm, 2 * B_DIM), lambda i: (i, 0)),
                   pl.BlockSpec((tm, 2 * B_DIM), lambda i: (i, 0)),
                   pl.BlockSpec((tm, B_DIM), lambda i: (i, 0))],
        out_shape=[two, one, two, two, one],
        compiler_params=_cparams(("parallel",)),
        name="rw_features",
    )(ps, rkv, wup, aup, gup, w0, a0, k_k, k_a)


def _stack_heads(t, lane_lo):
    return jnp.concatenate([jnp.where(lane_lo, t, 0.0), jnp.where(lane_lo, 0.0, t)], axis=0)


def _rwkv_kernel(r_ref, lw_ref, kk_ref, kka_ref, v_ref, kt_ref, s0_ref, y_ref, s_ref, *, rev, pairs):
    c = r_ref.shape[0]
    c2 = 2 * c

    @pl.when(pl.program_id(2) == 0)
    def _():
        s_ref[...] = s0_ref[...]

    _, incl, _ = _chunk_masks(c, rev)
    eye2, incl2, _, strict2 = _pair_masks(c, rev)
    eye2_f = eye2.astype(F32)
    lane_lo = lax.broadcasted_iota(jnp.int32, (1, LANES), 1) < B_N
    steps = _nilpotent_steps(c)
    rng = range(pairs)
    sls = [slice(p * LANES, (p + 1) * LANES) for p in rng]

    lw_all = lw_ref[...]
    g_all = _dot_exact_lhs(incl.astype(F32), lw_all)
    gtot_all = jnp.sum(lw_all, axis=0, keepdims=True)

    a_s, r_s, v_s, bt_s, kt_s, big = [], [], [], [], [], []
    for p in rng:
        sl = sls[p]
        g_in = g_all[:, sl]
        e_neg = jnp.exp(-g_in)
        e_end = jnp.exp(gtot_all[:, sl] - g_in)
        kk, kka, kt = kk_ref[:, sl], kka_ref[:, sl], kt_ref[:, sl]
        a_s.append(_stack_heads(kk * jnp.exp(g_in - lw_all[:, sl]), lane_lo))
        r_s.append(_stack_heads(r_ref[:, sl] * jnp.exp(g_in), lane_lo))
        v_s.append(_stack_heads(v_ref[:, sl], lane_lo))
        bt_s.append(_stack_heads(-kka * e_end, lane_lo))
        kt_s.append(_stack_heads(kt * e_end, lane_lo))
        b_s = _stack_heads(-kka * e_neg, lane_lo)
        k_s = _stack_heads(kt * e_neg, lane_lo)
        big.append(_dot_nt(jnp.concatenate([a_s[p], r_s[p]], axis=0), jnp.concatenate([b_s, k_s], axis=0)))
    l_ab = [jnp.where(strict2, big[p][0:c2, 0:c2], 0.0) for p in rng]
    xs = _unit_lower_inverse_many(l_ab, eye2_f, steps)
    lv = [_dot(jnp.where(strict2, big[p][0:c2, c2:2 * c2], 0.0), v_s[p]) for p in rng]
    wu = [_dot(xs[p], jnp.concatenate([a_s[p], lv[p]], axis=1)) for p in rng]
    rq = [_dot(jnp.where(incl2, big[p][c2:2 * c2, 0:c2], 0.0), wu[p]) for p in rng]
    y0 = [rq[p][:, LANES:] + _dot(jnp.where(incl2, big[p][c2:2 * c2, c2:2 * c2], 0.0), v_s[p]) for p in rng]
    s_add = [_dot_tn(v_s[p], kt_s[p]) for p in rng]
    s_old = [s_ref[0, p] for p in rng]
    u = [_dot_nt(wu[p][:, 0:LANES], s_old[p]) + wu[p][:, LANES:] for p in rng]
    for p in rng:
        rq_p = r_s[p] + rq[p][:, 0:LANES]
        y2 = _dot_nt(rq_p[0:c] + rq_p[c:c2], s_old[p])
        y_ref[:, sls[p]] = y2 + y0[p][0:c] + y0[p][c:c2]
    for p in rng:
        s_ref[0, p] = s_old[p] * jnp.exp(gtot_all[:, sls[p]]) + _dot_tn(u[p], bt_s[p]) + s_add[p]


def rwkv_scan(rkv, logw, kk, kka, kt, s0, *, batch, seq, rev, d, pairs=8):
    nc = seq // CHUNK
    npair = B_DIM // LANES
    ng = npair // pairs
    w = pairs * LANES

    def row(b, c):
        return b * nc + ((nc - 1 - c) if rev else c)

    def tok(off_blocks):
        return pl.BlockSpec((CHUNK, w), lambda b, h, c: (row(b, c), off_blocks + h))

    per_dir = d * ng
    st = pl.BlockSpec((1, pairs, LANES, LANES), lambda b, h, c: (b, h, 0, 0))
    return pl.pallas_call(
        functools.partial(_rwkv_kernel, rev=rev, pairs=pairs),
        grid=(batch, ng, nc),
        in_specs=[tok(0), tok(per_dir), tok(0), tok(per_dir), tok(2 * ng), tok(per_dir), st],
        out_specs=[tok(0), st],
        out_shape=[jax.ShapeDtypeStruct((batch * seq, B_DIM), F32),
                   jax.ShapeDtypeStruct((batch, npair, LANES, LANES), F32)],
        compiler_params=_cparams(("parallel", "parallel", "arbitrary")),
        name="rwkv_scan",
    )(rkv, logw, kk, kka, rkv, kt, s0)


def _ab_out_kernel(of_ref, ob_ref, yf_ref, yb_ref, z_ref, r_ref, v_ref, kt_ref, g_ref, x_ref, mod_ref,
                   gnw_ref, rk_ref, lnw_ref, lnb_ref, w_ref, o_ref, mix_ref, *, gate_idx):
    for h in range(A_HEADS):
        sl = slice(h * A_DK, (h + 1) * A_DK)
        o = of_ref[:, sl] + ob_ref[:, sl]
        ms = jnp.mean(o * o, axis=-1, keepdims=True)
        ya = o * lax.rsqrt(ms + NORM_EPS) * gnw_ref[...]
        mix_ref[:, sl] = (ya * _silu(z_ref[:, sl])).astype(BF16)
    avg = _group_ones(1.0 / B_N)
    ones = _group_ones(1.0)
    for t in range(B_DIM // LANES):
        sl = slice(t * LANES, (t + 1) * LANES)
        y = yf_ref[:, sl] + yb_ref[:, sl]
        mu = _dot_exact_rhs(y, avg)
        dlt = y - mu
        var = _dot_exact_rhs(dlt * dlt, avg)
        yn = dlt * lax.rsqrt(var + B_GN_EPS)
        kt_sum = kt_ref[:, sl] + kt_ref[:, B_DIM + t * LANES:B_DIM + (t + 1) * LANES]
        bonus = _dot_exact_rhs(r_ref[:, sl] * kt_sum * rk_ref[:, sl], ones) * v_ref[:, sl]
        yb = (yn * lnw_ref[:, sl] + lnb_ref[:, sl] + bonus) * g_ref[:, sl]
        mix_ref[:, A_DIM + t * LANES:A_DIM + (t + 1) * LANES] = yb.astype(BF16)
    acc = jnp.dot(mix_ref[...], w_ref[...], preferred_element_type=F32)
    o_ref[...] = x_ref[...] + mod_ref[0, gate_idx:gate_idx + 1, :] * acc


def ab_out(o_f, o_b, y_f, y_b, p_main, rkv, kt, g_out, x, mod, gnw, rk, lnw, lnb, w_out, *,
           gate_idx, rows_per_mod, tm=128):
    m, dm = x.shape
    tm = _row_tile(min(m, rows_per_mod), tm)
    tpm = rows_per_mod // tm
    zblk = CONV_CH // A_DIM
    row = lambda width, blk=0: pl.BlockSpec((tm, width), lambda i: (i, blk))
    full = lambda shp: pl.BlockSpec(shp, lambda i: (0, 0))
    return pl.pallas_call(
        functools.partial(_ab_out_kernel, gate_idx=gate_idx),
        grid=(m // tm,),
        in_specs=[row(A_DIM), row(A_DIM), row(B_DIM), row(B_DIM), row(A_DIM, zblk),
                  row(B_DIM, 0), row(B_DIM, 2), row(2 * B_DIM), row(B_DIM), row(dm),
                  pl.BlockSpec((1, MOD_ROWS, dm), lambda i: (i // tpm, 0, 0)),
                  full((1, A_DK)), full((1, B_DIM)), full((1, B_DIM)), full((1, B_DIM)),
                  full((A_DIM + B_DIM, dm))],
        out_specs=row(dm),
        out_shape=jax.ShapeDtypeStruct((m, dm), F32),
        scratch_shapes=[pltpu.VMEM((tm, A_DIM + B_DIM), BF16)],
        compiler_params=_cparams(("parallel",)),
        name="ab_out",
    )(o_f, o_b, y_f, y_b, p_main, rkv, rkv, kt, g_out, x, mod, gnw, rk, lnw, lnb, w_out)


def _mlstm_kernel(q_ref, k_ref, v_ref, g_ref, ib_ref, fb_ref, c0_ref, n0_ref, m0_ref,
                  h_ref, c_ref, n_ref, m_ref, *, rev, d, heads):
    c = q_ref.shape[0]

    @pl.when(pl.program_id(2) == 0)
    def _():
        c_ref[...] = c0_ref[...]
        n_ref[...] = n0_ref[...]
        m_ref[...] = m0_ref[...]

    eye, incl, incl_t = _chunk_masks(c, rev)
    gl = g_ref[...]
    cap = C_GATE_CAP
    ig_all = cap * jnp.tanh((gl + ib_ref[...]) / cap)
    fz = cap * jnp.tanh((gl + fb_ref[...]) / cap)
    logf_all = jnp.minimum(fz, 0.0) - jnp.log(1.0 + jnp.exp(-jnp.abs(fz)))
    lane = lax.broadcasted_iota(jnp.int32, (1, LANES), 1)
    hg = pl.program_id(1)
    rng = range(heads)
    qs = [slice(h * C_DK, (h + 1) * C_DK) for h in rng]
    vs = [slice(h * C_DV, (h + 1) * C_DV) for h in rng]
    q = [q_ref[:, qs[h]] * (C_DK ** -0.5) for h in rng]
    qk = [_dot_nt(q[h], k_ref[:, qs[h]]) for h in rng]
    qc = [_dot(q[h], c_ref[0, h]) for h in rng]
    heads_abs = [hg * heads + h for h in rng]
    gate_col = lambda mat, idx: jnp.sum(jnp.where(lane == idx, mat, 0.0), axis=1, keepdims=True)
    bcast = lambda col: jnp.broadcast_to(col, (c, c))
    ig = [gate_col(ig_all, d * C_HEADS + heads_abs[h]) for h in rng]
    logf = [gate_col(logf_all, 2 * C_HEADS + d * C_HEADS + heads_abs[h]) for h in rng]
    f_row = [jnp.sum(jnp.where(eye, bcast(logf[h]), 0.0), axis=0, keepdims=True) for h in rng]
    ig_row = [jnp.sum(jnp.where(eye, bcast(ig[h]), 0.0), axis=0, keepdims=True) for h in rng]
    b_row = [jnp.sum(jnp.where(incl_t, bcast(logf[h]), 0.0), axis=0, keepdims=True) for h in rng]
    b_col = [jnp.sum(jnp.where(incl, jnp.broadcast_to(f_row[h], (c, c)), 0.0), axis=1, keepdims=True) for h in rng]
    btot = [jnp.sum(logf[h], axis=0, keepdims=True) for h in rng]
    m_st = [m_ref[0, h, 0:1, 0:1] for h in rng]
    n_st = [n_ref[0, h, 0:1, :] for h in rng]
    w_end = [btot[h] - b_col[h] + ig[h] for h in rng]
    m_new = [jnp.maximum(btot[h] + m_st[h], jnp.max(w_end[h], axis=0, keepdims=True)) for h in rng]
    scale = [jnp.exp(btot[h] + m_st[h] - m_new[h]) for h in rng]
    kw = [k_ref[:, qs[h]] * jnp.exp(w_end[h] - m_new[h]) for h in rng]
    dmat = [b_col[h] - b_row[h] + ig_row[h] for h in rng]
    dmax = [jnp.max(jnp.where(incl, dmat[h], -1e30), axis=1, keepdims=True) for h in rng]
    m_row = [jnp.maximum(b_col[h] + m_st[h], dmax[h]) for h in rng]
    inter = [jnp.exp(b_col[h] + m_st[h] - m_row[h]) for h in rng]
    s = [jnp.where(incl, qk[h] * jnp.exp(jnp.where(incl, dmat[h] - m_row[h], 0.0)), 0.0) for h in rng]
    qn = [jnp.sum(q[h] * n_st[h], axis=1, keepdims=True) for h in rng]
    den = [jnp.sum(s[h], axis=1, keepdims=True) + inter[h] * qn[h] for h in rng]
    for h in rng:
        n_ref[0, h] = jnp.broadcast_to(n_st[h] * scale[h] + jnp.sum(kw[h], axis=0, keepdims=True), (SUBLANES, C_DK))
        m_ref[0, h] = jnp.broadcast_to(m_new[h], (SUBLANES, LANES))
    sv = [_dot(s[h], v_ref[:, vs[h]]) for h in rng]
    for h in rng:
        num = sv[h] + inter[h] * qc[h]
        h_ref[:, vs[h]] = num / jnp.maximum(jnp.abs(den[h]), jnp.exp(-m_row[h]))
    kv = [_dot_tn(kw[h], v_ref[:, vs[h]]) for h in rng]
    for h in rng:
        c_ref[0, h] = c_ref[0, h] * scale[h] + kv[h]


def mlstm_scan(p, gates, ib_vec, fb_vec, c0, n0, m0, *, batch, seq, rev, d, heads=8):
    nc = seq // CHUNK
    ng = C_HEADS // heads

    def row(b, c):
        return b * nc + ((nc - 1 - c) if rev else c)

    wq, wv = heads * C_DK, heads * C_DV
    qspec = pl.BlockSpec((CHUNK, wq), lambda b, h, c: (row(b, c), h))
    kspec = pl.BlockSpec((CHUNK, wq), lambda b, h, c: (row(b, c), ng + h))
    vspec = pl.BlockSpec((CHUNK, wv), lambda b, h, c: (row(b, c), (2 * C_QK) // wv + h))
    hspec = pl.BlockSpec((CHUNK, wv), lambda b, h, c: (row(b, c), h))
    vec = pl.BlockSpec((1, LANES), lambda b, h, c: (0, 0))
    cst = pl.BlockSpec((1, heads, C_DK, C_DV), lambda b, h, c: (b, h, 0, 0))
    nst = pl.BlockSpec((1, heads, SUBLANES, C_DK), lambda b, h, c: (b, h, 0, 0))
    mst = pl.BlockSpec((1, heads, SUBLANES, LANES), lambda b, h, c: (b, h, 0, 0))
    return pl.pallas_call(
        functools.partial(_mlstm_kernel, rev=rev, d=d, heads=heads),
        grid=(batch, ng, nc),
        in_specs=[qspec, kspec, vspec,
                  pl.BlockSpec((CHUNK, LANES), lambda b, h, c: (row(b, c), 0)),
                  vec, vec, cst, nst, mst],
        out_specs=[hspec, cst, nst, mst],
        out_shape=[jax.ShapeDtypeStruct((batch * seq, C_V), F32),
                   jax.ShapeDtypeStruct((batch, C_HEADS, C_DK, C_DV), F32),
                   jax.ShapeDtypeStruct((batch, C_HEADS, SUBLANES, C_DK), F32),
                   jax.ShapeDtypeStruct((batch, C_HEADS, SUBLANES, LANES), F32)],
        compiler_params=_cparams(("parallel", "parallel", "arbitrary")),
        name="mlstm_scan",
    )(p, p, p, gates, ib_vec, fb_vec, c0, n0, m0)


def _ml_out_kernel(hf_ref, hb_ref, o_ref_in, x_ref, mod_ref, nw_ref, w_ref, out_ref, mix_ref, *, gate_idx):
    for h in range(C_HEADS):
        sl = slice(h * C_DV, (h + 1) * C_DV)
        hh = hf_ref[:, sl] + hb_ref[:, sl]
        ms = jnp.mean(hh * hh, axis=-1, keepdims=True)
        y = hh * lax.rsqrt(ms + NORM_EPS) * nw_ref[:, sl]
        mix_ref[:, sl] = (y * _sigmoid(o_ref_in[:, sl])).astype(BF16)
    acc = jnp.dot(mix_ref[...], w_ref[...], preferred_element_type=F32)
    out_ref[...] = x_ref[...] + mod_ref[0, gate_idx:gate_idx + 1, :] * acc


def ml_out(h_f, h_b, p, x, mod, nw, w_out, *, gate_idx, rows_per_mod, tm=256):
    m, dm = x.shape
    tm = _row_tile(min(m, rows_per_mod), tm)
    tpm = rows_per_mod // tm
    oblk = (2 * C_QK + C_V) // C_V
    row = lambda width, blk=0: pl.BlockSpec((tm, width), lambda i: (i, blk))
    full = lambda shp: pl.BlockSpec(shp, lambda i: (0, 0))
    return pl.pallas_call(
        functools.partial(_ml_out_kernel, gate_idx=gate_idx),
        grid=(m // tm,),
        in_specs=[row(C_V), row(C_V), row(C_V, oblk), row(dm),
                  pl.BlockSpec((1, MOD_ROWS, dm), lambda i: (i // tpm, 0, 0)),
                  full((1, C_V)), full((C_V, dm))],
        out_specs=row(dm),
        out_shape=jax.ShapeDtypeStruct((m, dm), F32),
        scratch_shapes=[pltpu.VMEM((tm, C_V), BF16)],
        compiler_params=_cparams(("parallel",)),
        name="ml_out",
    )(h_f, h_b, p, x, mod, nw, w_out)


def _lane_vec(values, offset):
    flat = values.reshape(-1).astype(F32)
    return jnp.zeros((1, LANES), F32).at[0, offset:offset + flat.shape[0]].set(flat)


def _ffn(x, mod, nw, w1, w3, w2, rows_per_mod):
    g = ffn_up(x, mod, nw, w1, w3, shift_idx=3, scale_idx=4, rows_per_mod=rows_per_mod)
    return mm_res(g, w2, x, mod, gate_idx=5, rows_per_mod=rows_per_mod)


def _ab_layer(streams, prm, ctx_out):
    w_in = prm["w_in"]
    w_main = w_in[:, :CONV_CH + A_DIM].astype(BF16)
    small = w_in[:, CONV_CH + A_DIM + 4 * A_HEADS:]
    gates_w = w_in[:, CONV_CH + A_DIM:CONV_CH + A_DIM + 4 * A_HEADS]
    d_model = w_in.shape[0]
    w_small = jnp.concatenate(
        [small, jnp.zeros((d_model, 512 - small.shape[1]), F32),
         gates_w, jnp.zeros((d_model, LANES - gates_w.shape[1]), F32)], axis=1).astype(BF16)
    conv_w = jnp.concatenate([prm["conv_w"].reshape(9, CONV_CH), jnp.zeros((7, CONV_CH), F32)], axis=0)
    a_vec = _lane_vec(prm["a_log"], 0)
    dt_vec = _lane_vec(prm["dt_bias"], 0)
    zero_up = jnp.zeros((B_LORA, B_DIM), F32)
    blockdiag = lambda u: jnp.concatenate(
        [jnp.concatenate([u[0], zero_up], axis=1), jnp.concatenate([zero_up, u[1]], axis=1)], axis=0).astype(BF16)
    wup, aup = blockdiag(prm["w_up"]), blockdiag(prm["a_up"])
    gup = jnp.concatenate([prm["g_up"], jnp.zeros((256 - B_G_LORA, B_DIM), F32)], axis=0).astype(BF16)
    w0 = prm["w0"].reshape(1, 2 * B_DIM)
    a0 = prm["a0"].reshape(1, 2 * B_DIM)
    k_k = prm["k_k"].reshape(1, B_DIM)
    k_a = prm["k_a"].reshape(1, B_DIM)
    gnw = prm["gdn_norm_w"].reshape(1, A_DK)
    rk = prm["r_k"].reshape(1, B_DIM)
    lnw = prm["ln_w"].reshape(1, B_DIM)
    lnb = prm["ln_b"].reshape(1, B_DIM)
    w_out = prm["w_out"].astype(BF16)

    feats = []
    for st in streams:
        kw = dict(shift_idx=0, scale_idx=1, rows_per_mod=st["rpm"])
        p_main = modmm(st["x"], st["mod"], prm["norm_w"], w_main, **kw)
        p_small = modmm(st["x"], st["mod"], prm["norm_w"], w_small, tn=640, **kw)
        ckw = dict(batch=st["batch"], seq=st["seq"], rows=st["rows"])
        q = grid_conv(p_main, conv_w, col0=0, ncol=A_DIM, mode="q", **ckw)
        k = grid_conv(p_main, conv_w, col0=A_DIM, ncol=A_DIM, mode="k", **ckw)
        v = grid_conv(p_main, conv_w, col0=2 * A_DIM, ncol=A_DIM, mode="v", **ckw)
        rkv = grid_conv(p_main, conv_w, col0=3 * A_DIM, ncol=3 * B_DIM, mode="raw", **ckw)
        logw, kk, kka, kt, g_out = rw_features(p_small, rkv, wup, aup, gup, w0, a0, k_k, k_a)
        feats.append(dict(p_main=p_main, p_small=p_small, q=q, k=k, v=v, rkv=rkv,
                          logw=logw, kk=kk, kka=kka, kt=kt, g_out=g_out))

    nb = streams[-1]["batch"]
    outs = [dict() for _ in streams]
    for d in range(2):
        rev = d == 1
        s_a = jnp.zeros((nb, A_HEADS, A_DK, A_DK), F32)
        s_b = jnp.zeros((nb, B_DIM // LANES, LANES, LANES), F32)
        for si, (st, f) in enumerate(zip(streams, feats)):
            skw = dict(batch=st["batch"], seq=st["seq"], rev=rev, d=d)
            o, s_a = gdn_scan(f["q"], f["k"], f["v"], f["p_small"], a_vec, dt_vec, s_a, gate_blk=4, **skw)
            y, s_b = rwkv_scan(f["rkv"], f["logw"], f["kk"], f["kka"], f["kt"], s_b, **skw)
            outs[si]["o%d" % d] = o
            outs[si]["y%d" % d] = y

    new_x = []
    for si, (st, f) in enumerate(zip(streams, feats)):
        if si == 0 and not ctx_out:
            new_x.append(None)
            continue
        o = outs[si]
        new_x.append(ab_out(o["o0"], o["o1"], o["y0"], o["y1"], f["p_main"], f["rkv"], f["kt"], f["g_out"],
                            st["x"], st["mod"], gnw, rk, lnw, lnb, w_out, gate_idx=2, rows_per_mod=st["rpm"]))
    return new_x


def _ml_layer(streams, prm, ctx_out):
    w_in = prm["w_in"]
    main_cols = 2 * C_QK + 2 * C_V
    w_main = w_in[:, :main_cols].astype(BF16)
    d_model = w_in.shape[0]
    w_g = jnp.concatenate([w_in[:, main_cols:], jnp.zeros((d_model, LANES - 4 * C_HEADS), F32)], axis=1).astype(BF16)
    ib_vec = _lane_vec(prm["i_bias"], 0)
    fb_vec = _lane_vec(prm["f_bias"], 2 * C_HEADS)
    nw = prm["ml_norm_w"].reshape(1, C_V)
    w_out = prm["w_out"].astype(BF16)

    feats = []
    for st in streams:
        kw = dict(shift_idx=0, scale_idx=1, rows_per_mod=st["rpm"])
        p_main = modmm(st["x"], st["mod"], prm["norm_w"], w_main, **kw)
        p_g = modmm(st["x"], st["mod"], prm["norm_w"], w_g, tn=LANES, **kw)
        feats.append(dict(p_main=p_main, p_g=p_g))

    nb = streams[-1]["batch"]
    outs = [dict() for _ in streams]
    for d in range(2):
        rev = d == 1
        c_st = jnp.zeros((nb, C_HEADS, C_DK, C_DV), F32)
        n_st = jnp.zeros((nb, C_HEADS, SUBLANES, C_DK), F32)
        m_st = jnp.zeros((nb, C_HEADS, SUBLANES, LANES), F32)
        for si, (st, f) in enumerate(zip(streams, feats)):
            h, c_st, n_st, m_st = mlstm_scan(f["p_main"], f["p_g"], ib_vec, fb_vec, c_st, n_st, m_st,
                                             batch=st["batch"], seq=st["seq"], rev=rev, d=d)
            outs[si]["h%d" % d] = h

    new_x = []
    for si, (st, f) in enumerate(zip(streams, feats)):
        if si == 0 and not ctx_out:
            new_x.append(None)
            continue
        new_x.append(ml_out(outs[si]["h0"], outs[si]["h1"], f["p_main"], st["x"], st["mod"], nw, w_out,
                            gate_idx=2, rows_per_mod=st["rpm"]))
    return new_x


def kernel(x, c, ctx, c_ctx, ada_w, ada_b, norm_w, ab_w_in, ab_conv_w, gdn_a_log, gdn_dt_bias, gdn_norm_w, rw_w0, rw_w_up, rw_a0, rw_a_up, rw_g_up, rw_k_k, rw_k_a, rw_r_k, rw_ln_w, rw_ln_b, ab_w_out, ml_w_in, ml_i_bias, ml_f_bias, ml_norm_w, ml_w_out, ffn_w1, ffn_w3, ffn_w2, final_norm_w):
    bsz, seq, dm = x.shape
    ctx_len = ctx.shape[1]
    depth = ada_w.shape[0]
    xl = x.reshape(bsz * seq, dm)
    xc = ctx.reshape(bsz * ctx_len, dm)
    cond = jnp.zeros((MOD_ROWS, dm), F32).at[:bsz].set(c).at[bsz].set(c_ctx)

    for i in range(depth):
        ctx_out = i < depth - 1
        j = i // 2
        mod_all = adaln(cond, ada_w[i], ada_b[i].reshape(1, -1)).reshape(MOD_ROWS, 6, dm)
        mod_all = jnp.pad(mod_all, ((0, 0), (0, MOD_ROWS - 6), (0, 0)))
        streams = [
            dict(x=xc, mod=mod_all[bsz:bsz + 1], batch=bsz, seq=ctx_len, rows=1, rpm=bsz * ctx_len),
            dict(x=xl, mod=mod_all[:bsz], batch=bsz, seq=seq, rows=seq // GRID_W, rpm=seq),
        ]
        if i % 2 == 0:
            prm = dict(w_in=ab_w_in[j], conv_w=ab_conv_w[j], a_log=gdn_a_log[j], dt_bias=gdn_dt_bias[j],
                       gdn_norm_w=gdn_norm_w[j], w0=rw_w0[j], w_up=rw_w_up[j], a0=rw_a0[j], a_up=rw_a_up[j],
                       g_up=rw_g_up[j], k_k=rw_k_k[j], k_a=rw_k_a[j], r_k=rw_r_k[j], ln_w=rw_ln_w[j],
                       ln_b=rw_ln_b[j], w_out=ab_w_out[j], norm_w=norm_w[i, 0].reshape(1, dm))
            xc_new, xl = _ab_layer(streams, prm, ctx_out)
        else:
            prm = dict(w_in=ml_w_in[j], i_bias=ml_i_bias[j], f_bias=ml_f_bias[j], ml_norm_w=ml_norm_w[j],
                       w_out=ml_w_out[j], norm_w=norm_w[i, 0].reshape(1, dm))
            xc_new, xl = _ml_layer(streams, prm, ctx_out)
        w1, w3, w2 = ffn_w1[i].astype(BF16), ffn_w3[i].astype(BF16), ffn_w2[i].astype(BF16)
        nw2 = norm_w[i, 1].reshape(1, dm)
        xl = _ffn(xl, streams[1]["mod"], nw2, w1, w3, w2, streams[1]["rpm"])
        if ctx_out:
            xc = _ffn(xc_new, streams[0]["mod"], nw2, w1, w3, w2, streams[0]["rpm"])
    return final_norm(xl, final_norm_w.reshape(1, dm)).reshape(bsz, seq, dm)
```

```python
import functools
import math

import jax
import jax.numpy as jnp
from jax import lax
from jax.experimental import pallas as pl
from jax.experimental.pallas import tpu as pltpu

F32 = jnp.float32
BF16 = jnp.bfloat16

NORM_EPS = 1e-6
GRID_W = 64
LANES = 128
SUBLANES = 8
VMEM_LIMIT = 56 * 1024 * 1024

A_HEADS, A_DK = 8, 128
A_DIM = A_HEADS * A_DK
B_HEADS, B_N = 16, 64
B_DIM = B_HEADS * B_N
B_LORA = 64
B_G_LORA = 160
B_GN_EPS = 64e-5
B_DECAY_SCALE = math.exp(-0.5)
C_HEADS, C_DK, C_DV = 8, 128, 256
C_QK = C_HEADS * C_DK
C_V = C_HEADS * C_DV
C_GATE_CAP = 15.0
CONV_CH = 3 * A_DIM + 3 * B_DIM
CHUNK = 64
MOD_ROWS = 8


def _cparams(sem):
    return pltpu.CompilerParams(dimension_semantics=sem, vmem_limit_bytes=VMEM_LIMIT)


def _dot(a, b):
    return jnp.dot(a.astype(BF16), b.astype(BF16), preferred_element_type=F32)


def _dot_nt(a, b):
    return lax.dot_general(a.astype(BF16), b.astype(BF16), (((1,), (1,)), ((), ())),
                           preferred_element_type=F32)


def _dot_tn(a, b):
    return lax.dot_general(a.astype(BF16), b.astype(BF16), (((0,), (0,)), ((), ())),
                           preferred_element_type=F32)


def _split2(a):
    hi = a.astype(BF16)
    lo = (a - hi.astype(F32)).astype(BF16)
    return hi, lo


def _split3(a):
    hi = a.astype(BF16)
    r = a - hi.astype(F32)
    mid = r.astype(BF16)
    lo = (r - mid.astype(F32)).astype(BF16)
    return hi, mid, lo


def _dot_exact_rhs(a, b_exact):
    hi, lo = _split2(a)
    b = b_exact.astype(BF16)
    d = lambda t: jnp.dot(t, b, preferred_element_type=F32)
    return d(hi) + d(lo)


def _dot_exact_lhs(a_exact, b):
    hi, mid, lo = _split3(b)
    a = a_exact.astype(BF16)
    d = lambda t: jnp.dot(a, t, preferred_element_type=F32)
    return d(hi) + d(mid) + d(lo)


def _dot3(a, b):
    ah, al = _split2(a)
    bh, bl = _split2(b)
    d = lambda s, t: jnp.dot(s, t, preferred_element_type=F32)
    return d(ah, bh) + d(al, bh) + d(ah, bl)


def _sigmoid(t):
    return 1.0 / (1.0 + jnp.exp(-t))


def _silu(t):
    return t * _sigmoid(t)


def _softplus(t):
    return jnp.maximum(t, 0.0) + jnp.log(1.0 + jnp.exp(-jnp.abs(t)))


def _unit_lower_inverse_many(n_mats, eye_f, steps):
    xs = [eye_f + n for n in n_mats]
    if steps == 0:
        return xs
    r = n_mats[0].shape[0]
    pws = [_dot(p, p) for p in n_mats]
    for _ in range(steps - 1):
        both = [_dot(jnp.concatenate([x, p], axis=0), p) for x, p in zip(xs, pws)]
        xs = [x + b[0:r] for x, b in zip(xs, both)]
        pws = [b[r:2 * r] for b in both]
    return [x + _dot(x, p) for x, p in zip(xs, pws)]


def _pair_masks(c, rev):
    c2 = 2 * c
    ii = lax.broadcasted_iota(jnp.int32, (c2, c2), 0)
    jj = lax.broadcasted_iota(jnp.int32, (c2, c2), 1)
    sh = int(math.log2(c))
    same = (ii >> sh) == (jj >> sh)
    ti, tj = ii & (c - 1), jj & (c - 1)
    incl2 = jnp.logical_and(same, (tj >= ti) if rev else (tj <= ti))
    incl2_t = jnp.logical_and(same, (tj <= ti) if rev else (tj >= ti))
    eye2 = ii == jj
    strict2 = jnp.logical_and(incl2, jnp.logical_not(eye2))
    return eye2, incl2, incl2_t, strict2


def _nilpotent_steps(c):
    return max(int(math.ceil(math.log2(c))) - 1, 0)


def _adaln_kernel(c_ref, w_ref, b_ref, o_ref):
    o_ref[...] = _dot3(_silu(c_ref[...]), w_ref[...]) + b_ref[...]


def adaln(cond, w, b):
    m, d = cond.shape
    e = w.shape[1]
    tn = _col_tile(e, 512)
    return pl.pallas_call(
        _adaln_kernel,
        grid=(e // tn,),
        in_specs=[pl.BlockSpec((m, d), lambda j: (0, 0)),
                  pl.BlockSpec((d, tn), lambda j: (0, j)),
                  pl.BlockSpec((1, tn), lambda j: (0, j))],
        out_specs=pl.BlockSpec((m, tn), lambda j: (0, j)),
        out_shape=jax.ShapeDtypeStruct((m, e), F32),
        compiler_params=_cparams(("parallel",)),
        name="adaln",
    )(cond, w, b)


def _modulated(x_ref, mod_ref, nw_ref, shift_idx, scale_idx):
    x = x_ref[...]
    ms = jnp.mean(x * x, axis=-1, keepdims=True)
    y = x * lax.rsqrt(ms + NORM_EPS) * nw_ref[...]
    return y * (1.0 + mod_ref[0, scale_idx:scale_idx + 1, :]) + mod_ref[0, shift_idx:shift_idx + 1, :]


def _modmm_kernel(x_ref, mod_ref, nw_ref, w_ref, o_ref, h_ref, *, shift_idx, scale_idx):
    @pl.when(pl.program_id(1) == 0)
    def _():
        h_ref[...] = _modulated(x_ref, mod_ref, nw_ref, shift_idx, scale_idx).astype(BF16)

    o_ref[...] = jnp.dot(h_ref[...], w_ref[...], preferred_element_type=F32)


def _row_tile(m, want):
    t = min(want, m)
    while m % t:
        t //= 2
    return t


def _col_tile(n, want):
    t = min(want, n)
    while n % t or t % LANES:
        t -= LANES
    return t


def modmm(x, mod, nw, w, *, shift_idx, scale_idx, rows_per_mod, tm=1024, tn=1024):
    m, d = x.shape
    n = w.shape[1]
    tm = _row_tile(min(m, rows_per_mod), tm)
    tn = _col_tile(n, tn)
    tpm = rows_per_mod // tm
    return pl.pallas_call(
        functools.partial(_modmm_kernel, shift_idx=shift_idx, scale_idx=scale_idx),
        grid=(m // tm, n // tn),
        in_specs=[pl.BlockSpec((tm, d), lambda i, j: (i, 0)),
                  pl.BlockSpec((1, MOD_ROWS, d), lambda i, j: (i // tpm, 0, 0)),
                  pl.BlockSpec((1, d), lambda i, j: (0, 0)),
                  pl.BlockSpec((d, tn), lambda i, j: (0, j))],
        out_specs=pl.BlockSpec((tm, tn), lambda i, j: (i, j)),
        out_shape=jax.ShapeDtypeStruct((m, n), F32),
        scratch_shapes=[pltpu.VMEM((tm, d), BF16)],
        compiler_params=_cparams(("parallel", "arbitrary")),
        name="modmm",
    )(x, mod, nw, w)


def _ffn_up_kernel(x_ref, mod_ref, nw_ref, w1_ref, w3_ref, o_ref, h_ref, *, shift_idx, scale_idx):
    @pl.when(pl.program_id(1) == 0)
    def _():
        h_ref[...] = _modulated(x_ref, mod_ref, nw_ref, shift_idx, scale_idx).astype(BF16)

    h = h_ref[...]
    a = jnp.dot(h, w1_ref[...], preferred_element_type=F32)
    b = jnp.dot(h, w3_ref[...], preferred_element_type=F32)
    o_ref[...] = (_silu(a) * b).astype(BF16)


def ffn_up(x, mod, nw, w1, w3, *, shift_idx, scale_idx, rows_per_mod, tm=1024, tn=512):
    m, d = x.shape
    n = w1.shape[1]
    tm = _row_tile(min(m, rows_per_mod), tm)
    tn = _col_tile(n, tn)
    tpm = rows_per_mod // tm
    return pl.pallas_call(
        functools.partial(_ffn_up_kernel, shift_idx=shift_idx, scale_idx=scale_idx),
        grid=(m // tm, n // tn),
        in_specs=[pl.BlockSpec((tm, d), lambda i, j: (i, 0)),
                  pl.BlockSpec((1, MOD_ROWS, d), lambda i, j: (i // tpm, 0, 0)),
                  pl.BlockSpec((1, d), lambda i, j: (0, 0)),
                  pl.BlockSpec((d, tn), lambda i, j: (0, j)),
                  pl.BlockSpec((d, tn), lambda i, j: (0, j))],
        out_specs=pl.BlockSpec((tm, tn), lambda i, j: (i, j)),
        out_shape=jax.ShapeDtypeStruct((m, n), BF16),
        scratch_shapes=[pltpu.VMEM((tm, d), BF16)],
        compiler_params=_cparams(("parallel", "arbitrary")),
        name="ffn_up",
    )(x, mod, nw, w1, w3)


def _mm_res_kernel(a_ref, w_ref, res_ref, mod_ref, o_ref, *, gate_idx):
    acc = jnp.dot(a_ref[...], w_ref[...], preferred_element_type=F32)
    o_ref[...] = res_ref[...] + mod_ref[0, gate_idx:gate_idx + 1, :] * acc


def mm_res(a, w, res, mod, *, gate_idx, rows_per_mod, tm=1024, tn=512):
    m, k = a.shape
    n = w.shape[1]
    tm = _row_tile(min(m, rows_per_mod), tm)
    tn = _col_tile(n, tn)
    tpm = rows_per_mod // tm
    return pl.pallas_call(
        functools.partial(_mm_res_kernel, gate_idx=gate_idx),
        grid=(m // tm, n // tn),
        in_specs=[pl.BlockSpec((tm, k), lambda i, j: (i, 0)),
                  pl.BlockSpec((k, tn), lambda i, j: (0, j)),
                  pl.BlockSpec((tm, tn), lambda i, j: (i, j)),
                  pl.BlockSpec((1, MOD_ROWS, tn), lambda i, j: (i // tpm, 0, j))],
        out_specs=pl.BlockSpec((tm, tn), lambda i, j: (i, j)),
        out_shape=jax.ShapeDtypeStruct((m, n), F32),
        compiler_params=_cparams(("parallel", "arbitrary")),
        name="mm_res",
    )(a, w, res, mod)


def _final_norm_kernel(x_ref, w_ref, o_ref):
    x = x_ref[...]
    ms = jnp.mean(x * x, axis=-1, keepdims=True)
    o_ref[...] = x * lax.rsqrt(ms + NORM_EPS) * w_ref[...]


def final_norm(x, w, tm=512):
    m, d = x.shape
    tm = _row_tile(m, tm)
    return pl.pallas_call(
        _final_norm_kernel,
        grid=(m // tm,),
        in_specs=[pl.BlockSpec((tm, d), lambda i: (i, 0)), pl.BlockSpec((1, d), lambda i: (0, 0))],
        out_specs=pl.BlockSpec((tm, d), lambda i: (i, 0)),
        out_shape=jax.ShapeDtypeStruct((m, d), F32),
        compiler_params=_cparams(("parallel",)),
        name="final_norm",
    )(x, w)


CONV_ROWS = 256


def _conv_kernel(p_ref, w_ref, o_ref, s_ref, *, seq, rows, cols, pad, mode):
    zeros = jnp.zeros((pad, LANES), F32)
    for t in range(3):
        s_ref[t, 0:pad, :] = zeros
        s_ref[t, pad + seq:pad + seq + pad, :] = zeros
    s_ref[1, pad:pad + seq, :] = p_ref[...]
    rc = min(CONV_ROWS, seq)
    for c in range(seq // rc):
        base = c * rc
        col = (lax.broadcasted_iota(jnp.int32, (rc, 1), 0) + base) & (cols - 1)
        left = s_ref[1, pad - 1 + base:pad - 1 + base + rc, :]
        right = s_ref[1, pad + 1 + base:pad + 1 + base + rc, :]
        s_ref[0, pad + base:pad + base + rc, :] = jnp.where(col == 0, 0.0, left)
        s_ref[2, pad + base:pad + base + rc, :] = jnp.where(col == cols - 1, 0.0, right)
    drs = (0, 1, 2) if rows > 1 else (1,)
    for c in range(seq // rc):
        base = c * rc
        acc = jnp.zeros((rc, LANES), F32)
        for dr in drs:
            for dc in range(3):
                start = pad + base + (dr - 1) * cols
                acc = acc + s_ref[dc, start:start + rc, :] * w_ref[dr * 3 + dc:dr * 3 + dc + 1, :]
        if mode != "raw":
            acc = _silu(acc)
        if mode in ("q", "k"):
            acc = acc * lax.rsqrt(jnp.sum(acc * acc, axis=-1, keepdims=True) + 1e-6)
        if mode == "q":
            acc = acc * (A_DK ** -0.5)
        o_ref[base:base + rc, :] = acc


def grid_conv(p, conv_w, *, batch, seq, rows, col0, ncol, mode):
    cols = seq // rows
    pad = cols if rows > 1 else SUBLANES
    t0 = col0 // LANES
    return pl.pallas_call(
        functools.partial(_conv_kernel, seq=seq, rows=rows, cols=cols, pad=pad, mode=mode),
        grid=(batch, ncol // LANES),
        in_specs=[pl.BlockSpec((seq, LANES), lambda b, j: (b, t0 + j)),
                  pl.BlockSpec((16, LANES), lambda b, j: (0, t0 + j))],
        out_specs=pl.BlockSpec((seq, LANES), lambda b, j: (b, j)),
        out_shape=jax.ShapeDtypeStruct((batch * seq, ncol), F32),
        scratch_shapes=[pltpu.VMEM((3, seq + 2 * pad, LANES), F32)],
        compiler_params=_cparams(("parallel", "parallel")),
        name="grid_conv_" + mode,
    )(p, conv_w)


def _scan_block(seq, want):
    blk = min(want, seq // CHUNK)
    while (seq // CHUNK) % blk:
        blk -= 1
    return blk * CHUNK


def _chunk_masks(c, rev):
    ii = lax.broadcasted_iota(jnp.int32, (c, c), 0)
    jj = lax.broadcasted_iota(jnp.int32, (c, c), 1)
    eye = ii == jj
    incl = (jj >= ii) if rev else (jj <= ii)
    incl_t = (jj <= ii) if rev else (jj >= ii)
    return eye, incl, incl_t


def _gdn_kernel(q_ref, k_ref, v_ref, g_ref, av_ref, dv_ref, s0_ref, o_ref, s_ref, *, rev, d, heads):
    c = CHUNK
    c2 = 2 * c
    dk = A_DK
    nsub = q_ref.shape[0] // c

    @pl.when(pl.program_id(2) == 0)
    def _():
        s_ref[...] = s0_ref[...]

    eye2, incl2, incl2_t, strict2 = _pair_masks(c, rev)
    eye2_f = eye2.astype(F32)
    gl = g_ref[...]
    log_alpha = -jnp.exp(av_ref[...]) * _softplus(gl + dv_ref[...])
    beta_all = _sigmoid(gl)
    lane = lax.broadcasted_iota(jnp.int32, (1, LANES), 1)
    top = lax.broadcasted_iota(jnp.int32, (c2, 1), 0) < c
    hg = pl.program_id(1)
    steps = _nilpotent_steps(c)
    npair = heads // 2
    rows = [slice(ci * c, (ci + 1) * c) for ci in range(nsub)]
    items = [(ci, p) for ci in range(nsub) for p in range(npair)]
    idx = {it: n for n, it in enumerate(items)}
    ids = range(len(items))
    gate_col = lambda mat, ci, i: jnp.sum(jnp.where(lane == i, mat[rows[ci]], 0.0), axis=1, keepdims=True)

    bcast = lambda col: jnp.broadcast_to(col, (c2, c2))
    stack = lambda ref, ci, p: jnp.concatenate([ref[rows[ci], 2 * p * dk:(2 * p + 1) * dk],
                                                ref[rows[ci], (2 * p + 1) * dk:(2 * p + 2) * dk]], axis=0)
    la0 = [d * A_HEADS + hg * heads + 2 * p for ci, p in items]
    gts = [(gate_col(log_alpha, items[n][0], la0[n]), gate_col(log_alpha, items[n][0], la0[n] + 1)) for n in ids]
    g = [jnp.concatenate(gts[n], axis=0) for n in ids]
    beta = [jnp.concatenate([gate_col(beta_all, items[n][0], 2 * A_HEADS + la0[n]),
                             gate_col(beta_all, items[n][0], 2 * A_HEADS + la0[n] + 1)], axis=0) for n in ids]
    g_row = [jnp.sum(jnp.where(eye2, bcast(g[n]), 0.0), axis=0, keepdims=True) for n in ids]
    gc_row = [jnp.sum(jnp.where(incl2_t, bcast(g[n]), 0.0), axis=0, keepdims=True) for n in ids]
    gc_col = [jnp.sum(jnp.where(incl2, jnp.broadcast_to(g_row[n], (c2, c2)), 0.0), axis=1, keepdims=True) for n in ids]
    gts = [(jnp.sum(gts[n][0], axis=0, keepdims=True), jnp.sum(gts[n][1], axis=0, keepdims=True)) for n in ids]
    gtot_col = [jnp.where(top, gts[n][0], gts[n][1]) for n in ids]
    decay = [jnp.where(incl2, jnp.exp(jnp.where(incl2, gc_col[n] - gc_row[n], 0.0)), 0.0) for n in ids]
    egc = [jnp.exp(gc_col[n]) for n in ids]
    k = [stack(k_ref, ci, p) for ci, p in items]
    kb = [k[n] * beta[n] for n in ids]
    qe, big = [], []
    for n, (ci, p) in enumerate(items):
        q = stack(q_ref, ci, p)
        big.append(_dot_nt(jnp.concatenate([kb[n], q], axis=0), k[n]))
        qe.append(q * egc[n])
    m_neg = [jnp.where(strict2, -(big[n][0:c2] * decay[n]), 0.0) for n in ids]
    qk = [jnp.where(incl2, big[n][c2:2 * c2] * decay[n], 0.0) for n in ids]
    rhs = [jnp.concatenate([stack(v_ref, ci, p) * beta[n], kb[n] * egc[n]], axis=1) for n, (ci, p) in enumerate(items)]
    ktil = [k[n] * jnp.exp(gtot_col[n] - gc_col[n]) for n in ids]
    xs = _unit_lower_inverse_many(m_neg, eye2_f, steps)
    uw = [_dot(xs[n], rhs[n]) for n in ids]
    s_cur = [s_ref[0, h] for h in range(heads)]
    halves = (slice(0, c), slice(c, c2))
    for ci in (reversed(range(nsub)) if rev else range(nsub)):
        ns = [idx[(ci, p)] for p in range(npair)]
        ws_qs = [[_dot(jnp.concatenate([uw[ns[p]][r, dk:2 * dk], qe[ns[p]][r]], axis=0), s_cur[2 * p + j])
                  for j, r in enumerate(halves)] for p in range(npair)]
        v_new = [uw[ns[p]][:, 0:dk] - jnp.concatenate([ws_qs[p][0][0:c], ws_qs[p][1][0:c]], axis=0)
                 for p in range(npair)]
        for p in range(npair):
            o = jnp.concatenate([ws_qs[p][0][c:c2], ws_qs[p][1][c:c2]], axis=0) + _dot(qk[ns[p]], v_new[p])
            o_ref[rows[ci], 2 * p * dk:(2 * p + 1) * dk] = o[0:c]
            o_ref[rows[ci], (2 * p + 1) * dk:(2 * p + 2) * dk] = o[c:c2]
        s_cur = [s_cur[2 * p + j] * jnp.exp(gts[ns[p]][j]) + _dot_tn(ktil[ns[p]][r], v_new[p][r])
                 for p in range(npair) for j, r in enumerate(halves)]
    for h in range(heads):
        s_ref[0, h] = s_cur[h]


def gdn_scan(q, k, v, gates, a_vec, dt_vec, s0, *, batch, seq, rev, d, gate_blk, heads=8, sub=2):
    blk = _scan_block(seq, sub)
    nc = seq // blk
    ng = A_HEADS // heads
    w = heads * A_DK

    def row(b, c):
        return b * nc + ((nc - 1 - c) if rev else c)

    tok = pl.BlockSpec((blk, w), lambda b, h, c: (row(b, c), h))
    vec = pl.BlockSpec((1, LANES), lambda b, h, c: (0, 0))
    st = pl.BlockSpec((1, heads, A_DK, A_DK), lambda b, h, c: (b, h, 0, 0))
    return pl.pallas_call(
        functools.partial(_gdn_kernel, rev=rev, d=d, heads=heads),
        grid=(batch, ng, nc),
        in_specs=[tok, tok, tok,
                  pl.BlockSpec((blk, LANES), lambda b, h, c: (row(b, c), gate_blk)),
                  vec, vec, st],
        out_specs=[tok, st],
        out_shape=[jax.ShapeDtypeStruct((batch * seq, A_DIM), F32),
                   jax.ShapeDtypeStruct((batch, A_HEADS, A_DK, A_DK), F32)],
        compiler_params=_cparams(("parallel", "parallel", "arbitrary")),
        name="gdn_scan",
    )(q, k, v, gates, a_vec, dt_vec, s0)


def _group_ones(scale):
    ii = lax.broadcasted_iota(jnp.int32, (LANES, LANES), 0)
    jj = lax.broadcasted_iota(jnp.int32, (LANES, LANES), 1)
    sh = int(math.log2(B_N))
    return jnp.where((ii >> sh) == (jj >> sh), scale, 0.0).astype(F32)


def _rw_feat_kernel(ps_ref, kb_ref, wup_ref, aup_ref, gup_ref, w0_ref, a0_ref, kk_w_ref, ka_w_ref,
                    logw_ref, kk_ref, kka_ref, kt_ref, g_ref):
    ps = ps_ref[...]
    kb = kb_ref[...]
    lw = w0_ref[...] + _dot(jnp.tanh(ps[:, 0:2 * B_LORA]), wup_ref[...])
    logw_ref[...] = -B_DECAY_SCALE * _sigmoid(lw)
    a = _sigmoid(a0_ref[...] + _dot(ps[:, 2 * B_LORA:4 * B_LORA], aup_ref[...]))
    g_ref[...] = _dot(_sigmoid(ps[:, 4 * B_LORA:]), gup_ref[...])
    kkw = kb * kk_w_ref[...]
    ones = _group_ones(1.0)
    for t in range(B_DIM // LANES):
        sl = slice(t * LANES, (t + 1) * LANES)
        x = kkw[:, sl]
        ss = _dot_exact_rhs(x * x, ones)
        kk_ref[:, sl] = x * lax.rsqrt(ss + 1e-6)
    kk = kk_ref[...]
    for dd in range(2):
        a_d = a[:, dd * B_DIM:(dd + 1) * B_DIM]
        kka_ref[:, dd * B_DIM:(dd + 1) * B_DIM] = kk * a_d
        kt_ref[:, dd * B_DIM:(dd + 1) * B_DIM] = kb * (1.0 + (a_d - 1.0) * ka_w_ref[...])


def rw_features(ps, rkv, wup, aup, gup, w0, a0, k_k, k_a, tm=256):
    m = ps.shape[0]
    tm = _row_tile(m, tm)
    full = lambda shp: pl.BlockSpec(shp, lambda i: (0, 0))
    two = jax.ShapeDtypeStruct((m, 2 * B_DIM), F32)
    one = jax.ShapeDtypeStruct((m, B_DIM), F32)
    return pl.pallas_call(
        _rw_feat_kernel,
        grid=(m // tm,),
        in_specs=[pl.BlockSpec((tm, 512), lambda i: (i, 0)),
                  pl.BlockSpec((tm, B_DIM), lambda i: (i, 1)),
                  full((2 * B_LORA, 2 * B_DIM)), full((2 * B_LORA, 2 * B_DIM)), full((256, B_DIM)),
                  full((1, 2 * B_DIM)), full((1, 2 * B_DIM)), full((1, B_DIM)), full((1, B_DIM))],
        out_specs=[pl.BlockSpec((tm, 2 * B_DIM), lambda i: (i, 0)),
                   pl.BlockSpec((tm, B_DIM), lambda i: (i, 0)),
                   pl.BlockSpec((tm, 2 * B_DIM), lambda i: (i, 0)),
                   pl.BlockSpec((tm, 2 * B_DIM), lambda i: (i, 0)),
                   pl.BlockSpec((tm, B_DIM), lambda i: (i, 0))],
        out_shape=[two, one, two, two, one],
        compiler_params=_cparams(("parallel",)),
        name="rw_features",
    )(ps, rkv, wup, aup, gup, w0, a0, k_k, k_a)


def _stack_heads(t, lane_lo):
    return jnp.concatenate([jnp.where(lane_lo, t, 0.0), jnp.where(lane_lo, 0.0, t)], axis=0)


def _side_masks(c, rev):
    ti = lax.broadcasted_iota(jnp.int32, (c, 2 * c), 0)
    tj = lax.broadcasted_iota(jnp.int32, (c, 2 * c), 1) & (c - 1)
    incl = (tj >= ti) if rev else (tj <= ti)
    eye = tj == ti
    return eye, incl, jnp.logical_and(incl, jnp.logical_not(eye))


def _side_inverse_many(n_mats, eye_f, lane_lo, steps):
    c = n_mats[0].shape[0]
    xs = [eye_f + n for n in n_mats]
    if steps == 0:
        return xs
    pws = [_dot(p, _stack_heads(p, lane_lo)) for p in n_mats]
    for _ in range(steps - 1):
        both = [_dot(jnp.concatenate([x, p], axis=0), _stack_heads(p, lane_lo)) for x, p in zip(xs, pws)]
        xs = [x + bth[0:c] for x, bth in zip(xs, both)]
        pws = [bth[c:2 * c] for bth in both]
    return [x + _dot(x, _stack_heads(p, lane_lo)) for x, p in zip(xs, pws)]


def _rwkv_kernel(r_ref, lw_ref, kk_ref, kka_ref, v_ref, kt_ref, s0_ref, y_ref, s_ref, *, rev, pairs):
    c = CHUNK
    nsub = r_ref.shape[0] // c

    @pl.when(pl.program_id(2) == 0)
    def _():
        s_ref[...] = s0_ref[...]

    _, incl, _ = _chunk_masks(c, rev)
    incl_f = incl.astype(F32)
    eye_s, incl_s, strict_s = _side_masks(c, rev)
    eye_sf = eye_s.astype(F32)
    lane_lo = lax.broadcasted_iota(jnp.int32, (1, LANES), 1) < B_N
    ii = lax.broadcasted_iota(jnp.int32, (LANES, LANES), 0)
    jj = lax.broadcasted_iota(jnp.int32, (LANES, LANES), 1)
    sh = int(math.log2(B_N))
    same_head = (ii >> sh) == (jj >> sh)
    steps = _nilpotent_steps(c)
    stk = lambda t: _stack_heads(t, lane_lo)
    rng = range(pairs)
    sls = [slice(p * LANES, (p + 1) * LANES) for p in rng]
    rows = [slice(ci * c, (ci + 1) * c) for ci in range(nsub)]
    items = [(ci, p) for ci in range(nsub) for p in rng]
    idx = {it: n for n, it in enumerate(items)}
    ids = range(len(items))

    lw_all = [lw_ref[rows[ci], :] for ci in range(nsub)]
    g_all = [_dot_exact_lhs(incl_f, lw_all[ci]) for ci in range(nsub)]
    gtot_all = [jnp.sum(lw_all[ci], axis=0, keepdims=True) for ci in range(nsub)]

    a_n, r_n, v_n, big, upd_rhs = [], [], [], [], []
    for ci, p in items:
        sl, rw = sls[p], rows[ci]
        g_in = g_all[ci][:, sl]
        e_neg = jnp.exp(-g_in)
        e_end = jnp.exp(gtot_all[ci][:, sl] - g_in)
        kk, kka, kt = kk_ref[rw, sl], kka_ref[rw, sl], kt_ref[rw, sl]
        a_n.append(kk * jnp.exp(g_in - lw_all[ci][:, sl]))
        r_n.append(r_ref[rw, sl] * jnp.exp(g_in))
        v_n.append(v_ref[rw, sl])
        upd_rhs.append(jnp.concatenate([-kka * e_end, kt * e_end], axis=0))
        bk_s = jnp.concatenate([stk(-kka * e_neg), stk(kt * e_neg)], axis=0)
        big.append(_dot_nt(jnp.concatenate([a_n[-1], r_n[-1]], axis=0), bk_s))
    l_ab = [jnp.where(strict_s, big[n][0:c, 0:LANES], 0.0) for n in ids]
    xs = _side_inverse_many(l_ab, eye_sf, lane_lo, steps)
    v_s = [stk(v_n[n]) for n in ids]
    lv = [_dot(jnp.where(strict_s, big[n][0:c, LANES:2 * LANES], 0.0), v_s[n]) for n in ids]
    wu = [_dot(xs[n], jnp.concatenate([stk(a_n[n]), stk(lv[n])], axis=1)) for n in ids]
    rq = [_dot(jnp.where(incl_s, big[n][c:2 * c, 0:LANES], 0.0),
               jnp.concatenate([stk(wu[n][:, 0:LANES]), stk(wu[n][:, LANES:2 * LANES])], axis=1)) for n in ids]
    y0 = [rq[n][:, LANES:2 * LANES] + _dot(jnp.where(incl_s, big[n][c:2 * c, LANES:2 * LANES], 0.0), v_s[n])
          for n in ids]
    rq_n = [r_n[n] + rq[n][:, 0:LANES] for n in ids]
    s_cur = [s_ref[0, p] for p in rng]
    for ci in (reversed(range(nsub)) if rev else range(nsub)):
        ns = [idx[(ci, p)] for p in rng]
        u = [_dot_nt(wu[ns[p]][:, 0:LANES], s_cur[p]) + wu[ns[p]][:, LANES:2 * LANES] for p in rng]
        for p in rng:
            y_ref[rows[ci], sls[p]] = _dot_nt(rq_n[ns[p]], s_cur[p]) + y0[ns[p]]
        add = [_dot_tn(jnp.concatenate([u[p], v_n[ns[p]]], axis=0), upd_rhs[ns[p]]) for p in rng]
        s_cur = [s_cur[p] * jnp.exp(gtot_all[ci][:, sls[p]]) + jnp.where(same_head, add[p], 0.0) for p in rng]
    for p in rng:
        s_ref[0, p] = s_cur[p]


def rwkv_scan(rkv, logw, kk, kka, kt, s0, *, batch, seq, rev, d, pairs=8, sub=2):
    blk = _scan_block(seq, sub)
    nc = seq // blk
    npair = B_DIM // LANES
    ng = npair // pairs
    w = pairs * LANES

    def row(b, c):
        return b * nc + ((nc - 1 - c) if rev else c)

    def tok(off_blocks):
        return pl.BlockSpec((blk, w), lambda b, h, c: (row(b, c), off_blocks + h))

    per_dir = d * ng
    st = pl.BlockSpec((1, pairs, LANES, LANES), lambda b, h, c: (b, h, 0, 0))
    return pl.pallas_call(
        functools.partial(_rwkv_kernel, rev=rev, pairs=pairs),
        grid=(batch, ng, nc),
        in_specs=[tok(0), tok(per_dir), tok(0), tok(per_dir), tok(2 * ng), tok(per_dir), st],
        out_specs=[tok(0), st],
        out_shape=[jax.ShapeDtypeStruct((batch * seq, B_DIM), F32),
                   jax.ShapeDtypeStruct((batch, npair, LANES, LANES), F32)],
        compiler_params=_cparams(("parallel", "parallel", "arbitrary")),
        name="rwkv_scan",
    )(rkv, logw, kk, kka, rkv, kt, s0)


def _ab_out_kernel(of_ref, ob_ref, yf_ref, yb_ref, z_ref, r_ref, v_ref, kt_ref, g_ref, x_ref, mod_ref,
                   gnw_ref, rk_ref, lnw_ref, lnb_ref, w_ref, o_ref, mix_ref, *, gate_idx):
    for h in range(A_HEADS):
        sl = slice(h * A_DK, (h + 1) * A_DK)
        o = of_ref[:, sl] + ob_ref[:, sl]
        ms = jnp.mean(o * o, axis=-1, keepdims=True)
        ya = o * lax.rsqrt(ms + NORM_EPS) * gnw_ref[...]
        mix_ref[:, sl] = (ya * _silu(z_ref[:, sl])).astype(BF16)
    avg = _group_ones(1.0 / B_N)
    ones = _group_ones(1.0)
    for t in range(B_DIM // LANES):
        sl = slice(t * LANES, (t + 1) * LANES)
        y = yf_ref[:, sl] + yb_ref[:, sl]
        mu = _dot_exact_rhs(y, avg)
        dlt = y - mu
        var = _dot_exact_rhs(dlt * dlt, avg)
        yn = dlt * lax.rsqrt(var + B_GN_EPS)
        kt_sum = kt_ref[:, sl] + kt_ref[:, B_DIM + t * LANES:B_DIM + (t + 1) * LANES]
        bonus = _dot_exact_rhs(r_ref[:, sl] * kt_sum * rk_ref[:, sl], ones) * v_ref[:, sl]
        yb = (yn * lnw_ref[:, sl] + lnb_ref[:, sl] + bonus) * g_ref[:, sl]
        mix_ref[:, A_DIM + t * LANES:A_DIM + (t + 1) * LANES] = yb.astype(BF16)
    acc = jnp.dot(mix_ref[...], w_ref[...], preferred_element_type=F32)
    o_ref[...] = x_ref[...] + mod_ref[0, gate_idx:gate_idx + 1, :] * acc


def ab_out(o_f, o_b, y_f, y_b, p_main, rkv, kt, g_out, x, mod, gnw, rk, lnw, lnb, w_out, *,
           gate_idx, rows_per_mod, tm=128):
    m, dm = x.shape
    tm = _row_tile(min(m, rows_per_mod), tm)
    tpm = rows_per_mod // tm
    zblk = CONV_CH // A_DIM
    row = lambda width, blk=0: pl.BlockSpec((tm, width), lambda i: (i, blk))
    full = lambda shp: pl.BlockSpec(shp, lambda i: (0, 0))
    return pl.pallas_call(
        functools.partial(_ab_out_kernel, gate_idx=gate_idx),
        grid=(m // tm,),
        in_specs=[row(A_DIM), row(A_DIM), row(B_DIM), row(B_DIM), row(A_DIM, zblk),
                  row(B_DIM, 0), row(B_DIM, 2), row(2 * B_DIM), row(B_DIM), row(dm),
                  pl.BlockSpec((1, MOD_ROWS, dm), lambda i: (i // tpm, 0, 0)),
                  full((1, A_DK)), full((1, B_DIM)), full((1, B_DIM)), full((1, B_DIM)),
                  full((A_DIM + B_DIM, dm))],
        out_specs=row(dm),
        out_shape=jax.ShapeDtypeStruct((m, dm), F32),
        scratch_shapes=[pltpu.VMEM((tm, A_DIM + B_DIM), BF16)],
        compiler_params=_cparams(("parallel",)),
        name="ab_out",
    )(o_f, o_b, y_f, y_b, p_main, rkv, rkv, kt, g_out, x, mod, gnw, rk, lnw, lnb, w_out)


def _mlstm_kernel(q_ref, k_ref, v_ref, g_ref, ib_ref, fb_ref, c0_ref, n0_ref, m0_ref,
                  h_ref, c_ref, n_ref, m_ref, *, rev, d, heads):
    c = CHUNK
    nsub = q_ref.shape[0] // c

    @pl.when(pl.program_id(2) == 0)
    def _():
        c_ref[...] = c0_ref[...]
        n_ref[...] = n0_ref[...]
        m_ref[...] = m0_ref[...]

    eye, incl, incl_t = _chunk_masks(c, rev)
    gl = g_ref[...]
    cap = C_GATE_CAP
    ig_all = cap * jnp.tanh((gl + ib_ref[...]) / cap)
    fz = cap * jnp.tanh((gl + fb_ref[...]) / cap)
    logf_all = jnp.minimum(fz, 0.0) - jnp.log(1.0 + jnp.exp(-jnp.abs(fz)))
    lane = lax.broadcasted_iota(jnp.int32, (1, LANES), 1)
    hg = pl.program_id(1)
    rng = range(heads)
    qs = [slice(h * C_DK, (h + 1) * C_DK) for h in rng]
    vs = [slice(h * C_DV, (h + 1) * C_DV) for h in rng]
    rows = [slice(ci * c, (ci + 1) * c) for ci in range(nsub)]
    order = list(reversed(range(nsub))) if rev else list(range(nsub))
    items = [(ci, h) for ci in range(nsub) for h in rng]
    idx = {it: n for n, it in enumerate(items)}
    ids = range(len(items))
    q = [q_ref[rows[ci], qs[h]] * (C_DK ** -0.5) for ci, h in items]
    qk = [_dot_nt(q[n], k_ref[rows[ci], qs[h]]) for n, (ci, h) in enumerate(items)]
    gate_col = lambda mat, ci, i: jnp.sum(jnp.where(lane == i, mat[rows[ci]], 0.0), axis=1, keepdims=True)
    bcast = lambda col: jnp.broadcast_to(col, (c, c))
    ig = [gate_col(ig_all, ci, d * C_HEADS + hg * heads + h) for ci, h in items]
    logf = [gate_col(logf_all, ci, 2 * C_HEADS + d * C_HEADS + hg * heads + h) for ci, h in items]
    f_row = [jnp.sum(jnp.where(eye, bcast(logf[n]), 0.0), axis=0, keepdims=True) for n in ids]
    ig_row = [jnp.sum(jnp.where(eye, bcast(ig[n]), 0.0), axis=0, keepdims=True) for n in ids]
    b_row = [jnp.sum(jnp.where(incl_t, bcast(logf[n]), 0.0), axis=0, keepdims=True) for n in ids]
    b_col = [jnp.sum(jnp.where(incl, jnp.broadcast_to(f_row[n], (c, c)), 0.0), axis=1, keepdims=True) for n in ids]
    btot = [jnp.sum(logf[n], axis=0, keepdims=True) for n in ids]
    w_end = [btot[n] - b_col[n] + ig[n] for n in ids]
    w_max = [jnp.max(w_end[n], axis=0, keepdims=True) for n in ids]
    dmat = [b_col[n] - b_row[n] + ig_row[n] for n in ids]
    dmax = [jnp.max(jnp.where(incl, dmat[n], -1e30), axis=1, keepdims=True) for n in ids]
    m_st, m_new = [None] * len(items), [None] * len(items)
    m_cur = [m_ref[0, h, 0:1, 0:1] for h in rng]
    for ci in order:
        for h in rng:
            n = idx[(ci, h)]
            m_st[n] = m_cur[h]
            m_new[n] = jnp.maximum(btot[n] + m_cur[h], w_max[n])
            m_cur[h] = m_new[n]
    scale = [jnp.exp(btot[n] + m_st[n] - m_new[n]) for n in ids]
    kw = [k_ref[rows[ci], qs[h]] * jnp.exp(w_end[n] - m_new[n]) for n, (ci, h) in enumerate(items)]
    kw_sum = [jnp.sum(kw[n], axis=0, keepdims=True) for n in ids]
    m_row = [jnp.maximum(b_col[n] + m_st[n], dmax[n]) for n in ids]
    inter = [jnp.exp(b_col[n] + m_st[n] - m_row[n]) for n in ids]
    s = [jnp.where(incl, qk[n] * jnp.exp(jnp.where(incl, dmat[n] - m_row[n], 0.0)), 0.0) for n in ids]
    s_sum = [jnp.sum(s[n], axis=1, keepdims=True) for n in ids]
    floor = [jnp.exp(-m_row[n]) for n in ids]
    sv = [_dot(s[n], v_ref[rows[ci], vs[h]]) for n, (ci, h) in enumerate(items)]
    kv = [_dot_tn(kw[n], v_ref[rows[ci], vs[h]]) for n, (ci, h) in enumerate(items)]
    c_cur = [c_ref[0, h] for h in rng]
    n_cur = [n_ref[0, h, 0:1, :] for h in rng]
    for ci in order:
        ns = [idx[(ci, h)] for h in rng]
        qc = [_dot(q[ns[h]], c_cur[h]) for h in rng]
        for h in rng:
            n = ns[h]
            den = s_sum[n] + inter[n] * jnp.sum(q[n] * n_cur[h], axis=1, keepdims=True)
            num = sv[n] + inter[n] * qc[h]
            h_ref[rows[ci], vs[h]] = num / jnp.maximum(jnp.abs(den), floor[n])
        c_cur = [c_cur[h] * scale[ns[h]] + kv[ns[h]] for h in rng]
        n_cur = [n_cur[h] * scale[ns[h]] + kw_sum[ns[h]] for h in rng]
    for h in rng:
        c_ref[0, h] = c_cur[h]
        n_ref[0, h] = jnp.broadcast_to(n_cur[h], (SUBLANES, C_DK))
        m_ref[0, h] = jnp.broadcast_to(m_cur[h], (SUBLANES, LANES))


def mlstm_scan(p, gates, ib_vec, fb_vec, c0, n0, m0, *, batch, seq, rev, d, heads=8, sub=2):
    blk = _scan_block(seq, sub)
    nc = seq // blk
    ng = C_HEADS // heads

    def row(b, c):
        return b * nc + ((nc - 1 - c) if rev else c)

    wq, wv = heads * C_DK, heads * C_DV
    qspec = pl.BlockSpec((blk, wq), lambda b, h, c: (row(b, c), h))
    kspec = pl.BlockSpec((blk, wq), lambda b, h, c: (row(b, c), ng + h))
    vspec = pl.BlockSpec((blk, wv), lambda b, h, c: (row(b, c), (2 * C_QK) // wv + h))
    hspec = pl.BlockSpec((blk, wv), lambda b, h, c: (row(b, c), h))
    vec = pl.BlockSpec((1, LANES), lambda b, h, c: (0, 0))
    cst = pl.BlockSpec((1, heads, C_DK, C_DV), lambda b, h, c: (b, h, 0, 0))
    nst = pl.BlockSpec((1, heads, SUBLANES, C_DK), lambda b, h, c: (b, h, 0, 0))
    mst = pl.BlockSpec((1, heads, SUBLANES, LANES), lambda b, h, c: (b, h, 0, 0))
    return pl.pallas_call(
        functools.partial(_mlstm_kernel, rev=rev, d=d, heads=heads),
        grid=(batch, ng, nc),
        in_specs=[qspec, kspec, vspec,
                  pl.BlockSpec((blk, LANES), lambda b, h, c: (row(b, c), 0)),
                  vec, vec, cst, nst, mst],
        out_specs=[hspec, cst, nst, mst],
        out_shape=[jax.ShapeDtypeStruct((batch * seq, C_V), F32),
                   jax.ShapeDtypeStruct((batch, C_HEADS, C_DK, C_DV), F32),
                   jax.ShapeDtypeStruct((batch, C_HEADS, SUBLANES, C_DK), F32),
                   jax.ShapeDtypeStruct((batch, C_HEADS, SUBLANES, LANES), F32)],
        compiler_params=_cparams(("parallel", "parallel", "arbitrary")),
        name="mlstm_scan",
    )(p, p, p, gates, ib_vec, fb_vec, c0, n0, m0)


def _ml_out_kernel(hf_ref, hb_ref, o_ref_in, x_ref, mod_ref, nw_ref, w_ref, out_ref, mix_ref, *, gate_idx):
    for h in range(C_HEADS):
        sl = slice(h * C_DV, (h + 1) * C_DV)
        hh = hf_ref[:, sl] + hb_ref[:, sl]
        ms = jnp.mean(hh * hh, axis=-1, keepdims=True)
        y = hh * lax.rsqrt(ms + NORM_EPS) * nw_ref[:, sl]
        mix_ref[:, sl] = (y * _sigmoid(o_ref_in[:, sl])).astype(BF16)
    acc = jnp.dot(mix_ref[...], w_ref[...], preferred_element_type=F32)
    out_ref[...] = x_ref[...] + mod_ref[0, gate_idx:gate_idx + 1, :] * acc


def ml_out(h_f, h_b, p, x, mod, nw, w_out, *, gate_idx, rows_per_mod, tm=256):
    m, dm = x.shape
    tm = _row_tile(min(m, rows_per_mod), tm)
    tpm = rows_per_mod // tm
    oblk = (2 * C_QK + C_V) // C_V
    row = lambda width, blk=0: pl.BlockSpec((tm, width), lambda i: (i, blk))
    full = lambda shp: pl.BlockSpec(shp, lambda i: (0, 0))
    return pl.pallas_call(
        functools.partial(_ml_out_kernel, gate_idx=gate_idx),
        grid=(m // tm,),
        in_specs=[row(C_V), row(C_V), row(C_V, oblk), row(dm),
                  pl.BlockSpec((1, MOD_ROWS, dm), lambda i: (i // tpm, 0, 0)),
                  full((1, C_V)), full((C_V, dm))],
        out_specs=row(dm),
        out_shape=jax.ShapeDtypeStruct((m, dm), F32),
        scratch_shapes=[pltpu.VMEM((tm, C_V), BF16)],
        compiler_params=_cparams(("parallel",)),
        name="ml_out",
    )(h_f, h_b, p, x, mod, nw, w_out)


def _lane_vec(values, offset):
    flat = values.reshape(-1).astype(F32)
    return jnp.zeros((1, LANES), F32).at[0, offset:offset + flat.shape[0]].set(flat)


def _ffn(x, mod, nw, w1, w3, w2, rows_per_mod):
    g = ffn_up(x, mod, nw, w1, w3, shift_idx=3, scale_idx=4, rows_per_mod=rows_per_mod)
    return mm_res(g, w2, x, mod, gate_idx=5, rows_per_mod=rows_per_mod)


def _ab_layer(streams, prm, ctx_out):
    w_in = prm["w_in"]
    w_main = w_in[:, :CONV_CH + A_DIM].astype(BF16)
    small = w_in[:, CONV_CH + A_DIM + 4 * A_HEADS:]
    gates_w = w_in[:, CONV_CH + A_DIM:CONV_CH + A_DIM + 4 * A_HEADS]
    d_model = w_in.shape[0]
    w_small = jnp.concatenate(
        [small, jnp.zeros((d_model, 512 - small.shape[1]), F32),
         gates_w, jnp.zeros((d_model, LANES - gates_w.shape[1]), F32)], axis=1).astype(BF16)
    conv_w = jnp.concatenate([prm["conv_w"].reshape(9, CONV_CH), jnp.zeros((7, CONV_CH), F32)], axis=0)
    a_vec = _lane_vec(prm["a_log"], 0)
    dt_vec = _lane_vec(prm["dt_bias"], 0)
    zero_up = jnp.zeros((B_LORA, B_DIM), F32)
    blockdiag = lambda u: jnp.concatenate(
        [jnp.concatenate([u[0], zero_up], axis=1), jnp.concatenate([zero_up, u[1]], axis=1)], axis=0).astype(BF16)
    wup, aup = blockdiag(prm["w_up"]), blockdiag(prm["a_up"])
    gup = jnp.concatenate([prm["g_up"], jnp.zeros((256 - B_G_LORA, B_DIM), F32)], axis=0).astype(BF16)
    w0 = prm["w0"].reshape(1, 2 * B_DIM)
    a0 = prm["a0"].reshape(1, 2 * B_DIM)
    k_k = prm["k_k"].reshape(1, B_DIM)
    k_a = prm["k_a"].reshape(1, B_DIM)
    gnw = prm["gdn_norm_w"].reshape(1, A_DK)
    rk = prm["r_k"].reshape(1, B_DIM)
    lnw = prm["ln_w"].reshape(1, B_DIM)
    lnb = prm["ln_b"].reshape(1, B_DIM)
    w_out = prm["w_out"].astype(BF16)

    feats = []
    for st in streams:
        kw = dict(shift_idx=0, scale_idx=1, rows_per_mod=st["rpm"])
        p_main = modmm(st["x"], st["mod"], prm["norm_w"], w_main, **kw)
        p_small = modmm(st["x"], st["mod"], prm["norm_w"], w_small, tn=640, **kw)
        ckw = dict(batch=st["batch"], seq=st["seq"], rows=st["rows"])
        q = grid_conv(p_main, conv_w, col0=0, ncol=A_DIM, mode="q", **ckw)
        k = grid_conv(p_main, conv_w, col0=A_DIM, ncol=A_DIM, mode="k", **ckw)
        v = grid_conv(p_main, conv_w, col0=2 * A_DIM, ncol=A_DIM, mode="v", **ckw)
        rkv = grid_conv(p_main, conv_w, col0=3 * A_DIM, ncol=3 * B_DIM, mode="raw", **ckw)
        logw, kk, kka, kt, g_out = rw_features(p_small, rkv, wup, aup, gup, w0, a0, k_k, k_a)
        feats.append(dict(p_main=p_main, p_small=p_small, q=q, k=k, v=v, rkv=rkv,
                          logw=logw, kk=kk, kka=kka, kt=kt, g_out=g_out))

    nb = streams[-1]["batch"]
    outs = [dict() for _ in streams]
    for d in range(2):
        rev = d == 1
        s_a = jnp.zeros((nb, A_HEADS, A_DK, A_DK), F32)
        s_b = jnp.zeros((nb, B_DIM // LANES, LANES, LANES), F32)
        for si, (st, f) in enumerate(zip(streams, feats)):
            skw = dict(batch=st["batch"], seq=st["seq"], rev=rev, d=d)
            o, s_a = gdn_scan(f["q"], f["k"], f["v"], f["p_small"], a_vec, dt_vec, s_a, gate_blk=4, **skw)
            y, s_b = rwkv_scan(f["rkv"], f["logw"], f["kk"], f["kka"], f["kt"], s_b, **skw)
            outs[si]["o%d" % d] = o
            outs[si]["y%d" % d] = y

    new_x = []
    for si, (st, f) in enumerate(zip(streams, feats)):
        if si == 0 and not ctx_out:
            new_x.append(None)
            continue
        o = outs[si]
        new_x.append(ab_out(o["o0"], o["o1"], o["y0"], o["y1"], f["p_main"], f["rkv"], f["kt"], f["g_out"],
                            st["x"], st["mod"], gnw, rk, lnw, lnb, w_out, gate_idx=2, rows_per_mod=st["rpm"]))
    return new_x


def _ml_layer(streams, prm, ctx_out):
    w_in = prm["w_in"]
    main_cols = 2 * C_QK + 2 * C_V
    w_main = w_in[:, :main_cols].astype(BF16)
    d_model = w_in.shape[0]
    w_g = jnp.concatenate([w_in[:, main_cols:], jnp.zeros((d_model, LANES - 4 * C_HEADS), F32)], axis=1).astype(BF16)
    ib_vec = _lane_vec(prm["i_bias"], 0)
    fb_vec = _lane_vec(prm["f_bias"], 2 * C_HEADS)
    nw = prm["ml_norm_w"].reshape(1, C_V)
    w_out = prm["w_out"].astype(BF16)

    feats = []
    for st in streams:
        kw = dict(shift_idx=0, scale_idx=1, rows_per_mod=st["rpm"])
        p_main = modmm(st["x"], st["mod"], prm["norm_w"], w_main, **kw)
        p_g = modmm(st["x"], st["mod"], prm["norm_w"], w_g, tn=LANES, **kw)
        feats.append(dict(p_main=p_main, p_g=p_g))

    nb = streams[-1]["batch"]
    outs = [dict() for _ in streams]
    for d in range(2):
        rev = d == 1
        c_st = jnp.zeros((nb, C_HEADS, C_DK, C_DV), F32)
        n_st = jnp.zeros((nb, C_HEADS, SUBLANES, C_DK), F32)
        m_st = jnp.zeros((nb, C_HEADS, SUBLANES, LANES), F32)
        for si, (st, f) in enumerate(zip(streams, feats)):
            h, c_st, n_st, m_st = mlstm_scan(f["p_main"], f["p_g"], ib_vec, fb_vec, c_st, n_st, m_st,
                                             batch=st["batch"], seq=st["seq"], rev=rev, d=d)
            outs[si]["h%d" % d] = h

    new_x = []
    for si, (st, f) in enumerate(zip(streams, feats)):
        if si == 0 and not ctx_out:
            new_x.append(None)
            continue
        new_x.append(ml_out(outs[si]["h0"], outs[si]["h1"], f["p_main"], st["x"], st["mod"], nw, w_out,
                            gate_idx=2, rows_per_mod=st["rpm"]))
    return new_x


def kernel(x, c, ctx, c_ctx, ada_w, ada_b, norm_w, ab_w_in, ab_conv_w, gdn_a_log, gdn_dt_bias, gdn_norm_w, rw_w0, rw_w_up, rw_a0, rw_a_up, rw_g_up, rw_k_k, rw_k_a, rw_r_k, rw_ln_w, rw_ln_b, ab_w_out, ml_w_in, ml_i_bias, ml_f_bias, ml_norm_w, ml_w_out, ffn_w1, ffn_w3, ffn_w2, final_norm_w):
    bsz, seq, dm = x.shape
    ctx_len = ctx.shape[1]
    depth = ada_w.shape[0]
    xl = x.reshape(bsz * seq, dm)
    xc = ctx.reshape(bsz * ctx_len, dm)
    cond = jnp.zeros((MOD_ROWS, dm), F32).at[:bsz].set(c).at[bsz].set(c_ctx)

    for i in range(depth):
        ctx_out = i < depth - 1
        j = i // 2
        mod_all = adaln(cond, ada_w[i], ada_b[i].reshape(1, -1)).reshape(MOD_ROWS, 6, dm)
        mod_all = jnp.pad(mod_all, ((0, 0), (0, MOD_ROWS - 6), (0, 0)))
        streams = [
            dict(x=xc, mod=mod_all[bsz:bsz + 1], batch=bsz, seq=ctx_len, rows=1, rpm=bsz * ctx_len),
            dict(x=xl, mod=mod_all[:bsz], batch=bsz, seq=seq, rows=seq // GRID_W, rpm=seq),
        ]
        if i % 2 == 0:
            prm = dict(w_in=ab_w_in[j], conv_w=ab_conv_w[j], a_log=gdn_a_log[j], dt_bias=gdn_dt_bias[j],
                       gdn_norm_w=gdn_norm_w[j], w0=rw_w0[j], w_up=rw_w_up[j], a0=rw_a0[j], a_up=rw_a_up[j],
                       g_up=rw_g_up[j], k_k=rw_k_k[j], k_a=rw_k_a[j], r_k=rw_r_k[j], ln_w=rw_ln_w[j],
                       ln_b=rw_ln_b[j], w_out=ab_w_out[j], norm_w=norm_w[i, 0].reshape(1, dm))
            xc_new, xl = _ab_layer(streams, prm, ctx_out)
        else:
            prm = dict(w_in=ml_w_in[j], i_bias=ml_i_bias[j], f_bias=ml_f_bias[j], ml_norm_w=ml_norm_w[j],
                       w_out=ml_w_out[j], norm_w=norm_w[i, 0].reshape(1, dm))
            xc_new, xl = _ml_layer(streams, prm, ctx_out)
        w1, w3, w2 = ffn_w1[i].astype(BF16), ffn_w3[i].astype(BF16), ffn_w2[i].astype(BF16)
        nw2 = norm_w[i, 1].reshape(1, dm)
        xl = _ffn(xl, streams[1]["mod"], nw2, w1, w3, w2, streams[1]["rpm"])
        if ctx_out:
            xc = _ffn(xc_new, streams[0]["mod"], nw2, w1, w3, w2, streams[0]["rpm"])
    return final_norm(xl, final_norm_w.reshape(1, dm)).reshape(bsz, seq, dm)
```

```python
import functools
import math

import jax
import jax.numpy as jnp
from jax import lax
from jax.experimental import pallas as pl
from jax.experimental.pallas import tpu as pltpu

F32 = jnp.float32
BF16 = jnp.bfloat16

NORM_EPS = 1e-6
GRID_W = 64
LANES = 128
SUBLANES = 8
VMEM_LIMIT = 56 * 1024 * 1024

A_HEADS, A_DK = 8, 128
A_DIM = A_HEADS * A_DK
B_HEADS, B_N = 16, 64
B_DIM = B_HEADS * B_N
B_LORA = 64
B_G_LORA = 160
B_GN_EPS = 64e-5
B_DECAY_SCALE = math.exp(-0.5)
C_HEADS, C_DK, C_DV = 8, 128, 256
C_QK = C_HEADS * C_DK
C_V = C_HEADS * C_DV
C_GATE_CAP = 15.0
CONV_CH = 3 * A_DIM + 3 * B_DIM
CHUNK = 64
MOD_ROWS = 8


def _cparams(sem):
    return pltpu.CompilerParams(dimension_semantics=sem, vmem_limit_bytes=VMEM_LIMIT)


def _dot(a, b):
    return jnp.dot(a.astype(BF16), b.astype(BF16), preferred_element_type=F32)


def _dot_nt(a, b):
    return lax.dot_general(a.astype(BF16), b.astype(BF16), (((1,), (1,)), ((), ())),
                           preferred_element_type=F32)


def _dot_tn(a, b):
    return lax.dot_general(a.astype(BF16), b.astype(BF16), (((0,), (0,)), ((), ())),
                           preferred_element_type=F32)


def _split2(a):
    hi = a.astype(BF16)
    lo = (a - hi.astype(F32)).astype(BF16)
    return hi, lo


def _split3(a):
    hi = a.astype(BF16)
    r = a - hi.astype(F32)
    mid = r.astype(BF16)
    lo = (r - mid.astype(F32)).astype(BF16)
    return hi, mid, lo


def _dot_exact_rhs(a, b_exact):
    hi, lo = _split2(a)
    b = b_exact.astype(BF16)
    d = lambda t: jnp.dot(t, b, preferred_element_type=F32)
    return d(hi) + d(lo)


def _dot_exact_lhs(a_exact, b):
    hi, mid, lo = _split3(b)
    a = a_exact.astype(BF16)
    d = lambda t: jnp.dot(a, t, preferred_element_type=F32)
    return d(hi) + d(mid) + d(lo)


def _dot3(a, b):
    ah, al = _split2(a)
    bh, bl = _split2(b)
    d = lambda s, t: jnp.dot(s, t, preferred_element_type=F32)
    return d(ah, bh) + d(al, bh) + d(ah, bl)


def _sigmoid(t):
    return 1.0 / (1.0 + jnp.exp(-t))


def _silu(t):
    return t * _sigmoid(t)


def _softplus(t):
    return jnp.maximum(t, 0.0) + jnp.log(1.0 + jnp.exp(-jnp.abs(t)))


def _unit_lower_inverse_many(n_mats, eye_f, steps):
    xs = [eye_f + n for n in n_mats]
    if steps == 0:
        return xs
    r = n_mats[0].shape[0]
    pws = [_dot(p, p) for p in n_mats]
    for _ in range(steps - 1):
        both = [_dot(jnp.concatenate([x, p], axis=0), p) for x, p in zip(xs, pws)]
        xs = [x + b[0:r] for x, b in zip(xs, both)]
        pws = [b[r:2 * r] for b in both]
    return [x + _dot(x, p) for x, p in zip(xs, pws)]


def _pair_masks(c, rev):
    c2 = 2 * c
    ii = lax.broadcasted_iota(jnp.int32, (c2, c2), 0)
    jj = lax.broadcasted_iota(jnp.int32, (c2, c2), 1)
    sh = int(math.log2(c))
    same = (ii >> sh) == (jj >> sh)
    ti, tj = ii & (c - 1), jj & (c - 1)
    incl2 = jnp.logical_and(same, (tj >= ti) if rev else (tj <= ti))
    incl2_t = jnp.logical_and(same, (tj <= ti) if rev else (tj >= ti))
    eye2 = ii == jj
    strict2 = jnp.logical_and(incl2, jnp.logical_not(eye2))
    return eye2, incl2, incl2_t, strict2


def _nilpotent_steps(c):
    return max(int(math.ceil(math.log2(c))) - 1, 0)


def _adaln_kernel(c_ref, w_ref, b_ref, o_ref):
    o_ref[...] = _dot3(_silu(c_ref[...]), w_ref[...]) + b_ref[...]


def adaln(cond, w, b):
    m, d = cond.shape
    e = w.shape[1]
    tn = _col_tile(e, 512)
    return pl.pallas_call(
        _adaln_kernel,
        grid=(e // tn,),
        in_specs=[pl.BlockSpec((m, d), lambda j: (0, 0)),
                  pl.BlockSpec((d, tn), lambda j: (0, j)),
                  pl.BlockSpec((1, tn), lambda j: (0, j))],
        out_specs=pl.BlockSpec((m, tn), lambda j: (0, j)),
        out_shape=jax.ShapeDtypeStruct((m, e), F32),
        compiler_params=_cparams(("parallel",)),
        name="adaln",
    )(cond, w, b)


def _modulated(x_ref, mod_ref, nw_ref, shift_idx, scale_idx):
    x = x_ref[...]
    ms = jnp.mean(x * x, axis=-1, keepdims=True)
    gain = nw_ref[...] * (1.0 + mod_ref[0, scale_idx:scale_idx + 1, :])
    return x * lax.rsqrt(ms + NORM_EPS) * gain + mod_ref[0, shift_idx:shift_idx + 1, :]


def _modmm_kernel(x_ref, mod_ref, nw_ref, w_ref, ws_ref, o_ref, os_ref, h_ref, *, shift_idx, scale_idx, nmain):
    j = pl.program_id(1)

    @pl.when(j == 0)
    def _():
        h_ref[...] = _modulated(x_ref, mod_ref, nw_ref, shift_idx, scale_idx).astype(BF16)

    @pl.when(j < nmain)
    def _():
        o_ref[...] = jnp.dot(h_ref[...], w_ref[...], preferred_element_type=F32)

    @pl.when(j == nmain)
    def _():
        os_ref[...] = jnp.dot(h_ref[...], ws_ref[...], preferred_element_type=F32)


def _row_tile(m, want):
    t = min(want, m)
    while m % t:
        t //= 2
    return t


def _col_tile(n, want):
    t = min(want, n)
    while n % t or t % LANES:
        t -= LANES
    return t


def modmm(x, mod, nw, w, w_small, *, n_main, shift_idx, scale_idx, rows_per_mod, tm=1024, tn=1024):
    m, d = x.shape
    ns = w_small.shape[1]
    tm = _row_tile(min(m, rows_per_mod), tm)
    tn = _col_tile(n_main, tn)
    tpm = rows_per_mod // tm
    nmain = n_main // tn
    main_col = lambda j: jnp.minimum(j, nmain - 1)
    return pl.pallas_call(
        functools.partial(_modmm_kernel, shift_idx=shift_idx, scale_idx=scale_idx, nmain=nmain),
        grid=(m // tm, nmain + 1),
        in_specs=[pl.BlockSpec((tm, d), lambda i, j: (i, 0)),
                  pl.BlockSpec((1, MOD_ROWS, d), lambda i, j: (i // tpm, 0, 0)),
                  pl.BlockSpec((1, d), lambda i, j: (0, 0)),
                  pl.BlockSpec((d, tn), lambda i, j: (0, main_col(j))),
                  pl.BlockSpec((d, ns), lambda i, j: (0, 0))],
        out_specs=[pl.BlockSpec((tm, tn), lambda i, j: (i, main_col(j))),
                   pl.BlockSpec((tm, ns), lambda i, j: (i, 0))],
        out_shape=[jax.ShapeDtypeStruct((m, n_main), F32), jax.ShapeDtypeStruct((m, ns), F32)],
        scratch_shapes=[pltpu.VMEM((tm, d), BF16)],
        compiler_params=_cparams(("parallel", "arbitrary")),
        name="modmm",
    )(x, mod, nw, w, w_small)


def _ffn_up_kernel(x_ref, mod_ref, nw_ref, w1_ref, w3_ref, o_ref, h_ref, *, shift_idx, scale_idx):
    @pl.when(pl.program_id(1) == 0)
    def _():
        h_ref[...] = _modulated(x_ref, mod_ref, nw_ref, shift_idx, scale_idx).astype(BF16)

    h = h_ref[...]
    a = jnp.dot(h, w1_ref[...], preferred_element_type=F32)
    b = jnp.dot(h, w3_ref[...], preferred_element_type=F32)
    o_ref[...] = (_silu(a) * b).astype(BF16)


def ffn_up(x, mod, nw, w1, w3, *, shift_idx, scale_idx, rows_per_mod, tm=1024, tn=512):
    m, d = x.shape
    n = w1.shape[1]
    tm = _row_tile(min(m, rows_per_mod), tm)
    tn = _col_tile(n, tn)
    tpm = rows_per_mod // tm
    return pl.pallas_call(
        functools.partial(_ffn_up_kernel, shift_idx=shift_idx, scale_idx=scale_idx),
        grid=(m // tm, n // tn),
        in_specs=[pl.BlockSpec((tm, d), lambda i, j: (i, 0)),
                  pl.BlockSpec((1, MOD_ROWS, d), lambda i, j: (i // tpm, 0, 0)),
                  pl.BlockSpec((1, d), lambda i, j: (0, 0)),
                  pl.BlockSpec((d, tn), lambda i, j: (0, j)),
                  pl.BlockSpec((d, tn), lambda i, j: (0, j))],
        out_specs=pl.BlockSpec((tm, tn), lambda i, j: (i, j)),
        out_shape=jax.ShapeDtypeStruct((m, n), BF16),
        scratch_shapes=[pltpu.VMEM((tm, d), BF16)],
        compiler_params=_cparams(("parallel", "arbitrary")),
        name="ffn_up",
    )(x, mod, nw, w1, w3)


def _mm_res_kernel(a_ref, w_ref, res_ref, mod_ref, o_ref, *, gate_idx):
    acc = jnp.dot(a_ref[...], w_ref[...], preferred_element_type=F32)
    o_ref[...] = res_ref[...] + mod_ref[0, gate_idx:gate_idx + 1, :] * acc


def mm_res(a, w, res, mod, *, gate_idx, rows_per_mod, tm=1024, tn=512):
    m, k = a.shape
    n = w.shape[1]
    tm = _row_tile(min(m, rows_per_mod), tm)
    tn = _col_tile(n, tn)
    tpm = rows_per_mod // tm
    return pl.pallas_call(
        functools.partial(_mm_res_kernel, gate_idx=gate_idx),
        grid=(m // tm, n // tn),
        in_specs=[pl.BlockSpec((tm, k), lambda i, j: (i, 0)),
                  pl.BlockSpec((k, tn), lambda i, j: (0, j)),
                  pl.BlockSpec((tm, tn), lambda i, j: (i, j)),
                  pl.BlockSpec((1, MOD_ROWS, tn), lambda i, j: (i // tpm, 0, j))],
        out_specs=pl.BlockSpec((tm, tn), lambda i, j: (i, j)),
        out_shape=jax.ShapeDtypeStruct((m, n), F32),
        compiler_params=_cparams(("parallel", "arbitrary")),
        name="mm_res",
    )(a, w, res, mod)


def _final_norm_kernel(x_ref, w_ref, o_ref):
    x = x_ref[...]
    ms = jnp.mean(x * x, axis=-1, keepdims=True)
    o_ref[...] = x * lax.rsqrt(ms + NORM_EPS) * w_ref[...]


def final_norm(x, w, tm=512):
    m, d = x.shape
    tm = _row_tile(m, tm)
    return pl.pallas_call(
        _final_norm_kernel,
        grid=(m // tm,),
        in_specs=[pl.BlockSpec((tm, d), lambda i: (i, 0)), pl.BlockSpec((1, d), lambda i: (0, 0))],
        out_specs=pl.BlockSpec((tm, d), lambda i: (i, 0)),
        out_shape=jax.ShapeDtypeStruct((m, d), F32),
        compiler_params=_cparams(("parallel",)),
        name="final_norm",
    )(x, w)


CONV_ROWS = 256


def _conv_kernel(p_ref, w_ref, o_ref, s_ref, *, seq, rows, cols, pad, mode):
    zeros = jnp.zeros((pad, LANES), F32)
    for t in range(3):
        s_ref[t, 0:pad, :] = zeros
        s_ref[t, pad + seq:pad + seq + pad, :] = zeros
    s_ref[1, pad:pad + seq, :] = p_ref[...]
    rc = min(CONV_ROWS, seq)
    for c in range(seq // rc):
        base = c * rc
        col = (lax.broadcasted_iota(jnp.int32, (rc, 1), 0) + base) & (cols - 1)
        left = s_ref[1, pad - 1 + base:pad - 1 + base + rc, :]
        right = s_ref[1, pad + 1 + base:pad + 1 + base + rc, :]
        s_ref[0, pad + base:pad + base + rc, :] = jnp.where(col == 0, 0.0, left)
        s_ref[2, pad + base:pad + base + rc, :] = jnp.where(col == cols - 1, 0.0, right)
    drs = (0, 1, 2) if rows > 1 else (1,)
    for c in range(seq // rc):
        base = c * rc
        acc = jnp.zeros((rc, LANES), F32)
        for dr in drs:
            for dc in range(3):
                start = pad + base + (dr - 1) * cols
                acc = acc + s_ref[dc, start:start + rc, :] * w_ref[dr * 3 + dc:dr * 3 + dc + 1, :]
        if mode != "raw":
            acc = _silu(acc)
        if mode in ("q", "k"):
            acc = acc * lax.rsqrt(jnp.sum(acc * acc, axis=-1, keepdims=True) + 1e-6)
        if mode == "q":
            acc = acc * (A_DK ** -0.5)
        o_ref[base:base + rc, :] = acc


def grid_conv(p, conv_w, *, batch, seq, rows, col0, ncol, mode):
    cols = seq // rows
    pad = cols if rows > 1 else SUBLANES
    t0 = col0 // LANES
    return pl.pallas_call(
        functools.partial(_conv_kernel, seq=seq, rows=rows, cols=cols, pad=pad, mode=mode),
        grid=(batch, ncol // LANES),
        in_specs=[pl.BlockSpec((seq, LANES), lambda b, j: (b, t0 + j)),
                  pl.BlockSpec((16, LANES), lambda b, j: (0, t0 + j))],
        out_specs=pl.BlockSpec((seq, LANES), lambda b, j: (b, j)),
        out_shape=jax.ShapeDtypeStruct((batch * seq, ncol), F32),
        scratch_shapes=[pltpu.VMEM((3, seq + 2 * pad, LANES), F32)],
        compiler_params=_cparams(("parallel", "parallel")),
        name="grid_conv_" + mode,
    )(p, conv_w)


def _scan_block(seq, want):
    blk = min(want, seq // CHUNK)
    while (seq // CHUNK) % blk:
        blk -= 1
    return blk * CHUNK


def _chunk_masks(c, rev):
    ii = lax.broadcasted_iota(jnp.int32, (c, c), 0)
    jj = lax.broadcasted_iota(jnp.int32, (c, c), 1)
    eye = ii == jj
    incl = (jj >= ii) if rev else (jj <= ii)
    incl_t = (jj <= ii) if rev else (jj >= ii)
    return eye, incl, incl_t


def _gdn_kernel(q_ref, k_ref, v_ref, g_ref, av_ref, dv_ref, s0_ref, o_ref, s_ref, *, rev, d, heads):
    c = CHUNK
    c2 = 2 * c
    dk = A_DK
    nsub = q_ref.shape[0] // c

    @pl.when(pl.program_id(2) == 0)
    def _():
        s_ref[...] = s0_ref[...]

    eye2, incl2, incl2_t, strict2 = _pair_masks(c, rev)
    eye2_f = eye2.astype(F32)
    gl = g_ref[...]
    log_alpha = -jnp.exp(av_ref[...]) * _softplus(gl + dv_ref[...])
    beta_all = _sigmoid(gl)
    top = lax.broadcasted_iota(jnp.int32, (c2, 1), 0) < c
    steps = _nilpotent_steps(c)
    npair = heads // 2
    rows = [slice(ci * c, (ci + 1) * c) for ci in range(nsub)]
    items = [(ci, p) for ci in range(nsub) for p in range(npair)]
    idx = {it: n for n, it in enumerate(items)}
    ids = range(len(items))
    gate_col = lambda mat, ci, i: mat[rows[ci], i:i + 1]

    bcast = lambda col: jnp.broadcast_to(col, (c2, c2))
    stack = lambda ref, ci, p: jnp.concatenate([ref[rows[ci], 2 * p * dk:(2 * p + 1) * dk],
                                                ref[rows[ci], (2 * p + 1) * dk:(2 * p + 2) * dk]], axis=0)
    la0 = [d * A_HEADS + 2 * p for ci, p in items]
    gts = [(gate_col(log_alpha, items[n][0], la0[n]), gate_col(log_alpha, items[n][0], la0[n] + 1)) for n in ids]
    g = [jnp.concatenate(gts[n], axis=0) for n in ids]
    beta = [jnp.concatenate([gate_col(beta_all, items[n][0], 2 * A_HEADS + la0[n]),
                             gate_col(beta_all, items[n][0], 2 * A_HEADS + la0[n] + 1)], axis=0) for n in ids]
    g_row = [jnp.sum(jnp.where(eye2, bcast(g[n]), 0.0), axis=0, keepdims=True) for n in ids]
    gc_row = [jnp.sum(jnp.where(incl2_t, bcast(g[n]), 0.0), axis=0, keepdims=True) for n in ids]
    gc_col = [jnp.sum(jnp.where(incl2, jnp.broadcast_to(g_row[n], (c2, c2)), 0.0), axis=1, keepdims=True) for n in ids]
    gts = [(jnp.sum(gts[n][0], axis=0, keepdims=True), jnp.sum(gts[n][1], axis=0, keepdims=True)) for n in ids]
    gtot_col = [jnp.where(top, gts[n][0], gts[n][1]) for n in ids]
    decay = [jnp.where(incl2, jnp.exp(jnp.where(incl2, gc_col[n] - gc_row[n], 0.0)), 0.0) for n in ids]
    egc = [jnp.exp(gc_col[n]) for n in ids]
    k = [stack(k_ref, ci, p) for ci, p in items]
    kb = [k[n] * beta[n] for n in ids]
    qe, big = [], []
    for n, (ci, p) in enumerate(items):
        q = stack(q_ref, ci, p)
        big.append(_dot_nt(jnp.concatenate([kb[n], q], axis=0), k[n]))
        qe.append(q * egc[n])
    m_neg = [jnp.where(strict2, -(big[n][0:c2] * decay[n]), 0.0) for n in ids]
    qk = [jnp.where(incl2, big[n][c2:2 * c2] * decay[n], 0.0) for n in ids]
    rhs = [jnp.concatenate([stack(v_ref, ci, p) * beta[n], kb[n] * egc[n]], axis=1) for n, (ci, p) in enumerate(items)]
    ktil = [k[n] * jnp.exp(gtot_col[n] - gc_col[n]) for n in ids]
    xs = _unit_lower_inverse_many(m_neg, eye2_f, steps)
    uw = [_dot(xs[n], rhs[n]) for n in ids]
    s_cur = [s_ref[0, h] for h in range(heads)]
    halves = (slice(0, c), slice(c, c2))
    for ci in (reversed(range(nsub)) if rev else range(nsub)):
        ns = [idx[(ci, p)] for p in range(npair)]
        ws_qs = [[_dot(jnp.concatenate([uw[ns[p]][r, dk:2 * dk], qe[ns[p]][r]], axis=0), s_cur[2 * p + j])
                  for j, r in enumerate(halves)] for p in range(npair)]
        v_new = [uw[ns[p]][:, 0:dk] - jnp.concatenate([ws_qs[p][0][0:c], ws_qs[p][1][0:c]], axis=0)
                 for p in range(npair)]
        for p in range(npair):
            o = jnp.concatenate([ws_qs[p][0][c:c2], ws_qs[p][1][c:c2]], axis=0) + _dot(qk[ns[p]], v_new[p])
            o_ref[rows[ci], 2 * p * dk:(2 * p + 1) * dk] = o[0:c]
            o_ref[rows[ci], (2 * p + 1) * dk:(2 * p + 2) * dk] = o[c:c2]
        s_cur = [s_cur[2 * p + j] * jnp.exp(gts[ns[p]][j]) + _dot_tn(ktil[ns[p]][r], v_new[p][r])
                 for p in range(npair) for j, r in enumerate(halves)]
    for h in range(heads):
        s_ref[0, h] = s_cur[h]


def gdn_scan(q, k, v, gates, a_vec, dt_vec, s0, *, batch, seq, rev, d, gate_blk, sub=2):
    blk = _scan_block(seq, sub)
    nc = seq // blk
    heads, ng = A_HEADS, 1
    w = heads * A_DK

    def row(b, c):
        return b * nc + ((nc - 1 - c) if rev else c)

    tok = pl.BlockSpec((blk, w), lambda b, h, c: (row(b, c), h))
    vec = pl.BlockSpec((1, LANES), lambda b, h, c: (0, 0))
    st = pl.BlockSpec((1, heads, A_DK, A_DK), lambda b, h, c: (b, h, 0, 0))
    return pl.pallas_call(
        functools.partial(_gdn_kernel, rev=rev, d=d, heads=heads),
        grid=(batch, ng, nc),
        in_specs=[tok, tok, tok,
                  pl.BlockSpec((blk, LANES), lambda b, h, c: (row(b, c), gate_blk)),
                  vec, vec, st],
        out_specs=[tok, st],
        out_shape=[jax.ShapeDtypeStruct((batch * seq, A_DIM), F32),
                   jax.ShapeDtypeStruct((batch, A_HEADS, A_DK, A_DK), F32)],
        compiler_params=_cparams(("parallel", "parallel", "arbitrary")),
        name="gdn_scan",
    )(q, k, v, gates, a_vec, dt_vec, s0)


def _group_ones(scale):
    ii = lax.broadcasted_iota(jnp.int32, (LANES, LANES), 0)
    jj = lax.broadcasted_iota(jnp.int32, (LANES, LANES), 1)
    sh = int(math.log2(B_N))
    return jnp.where((ii >> sh) == (jj >> sh), scale, 0.0).astype(F32)


def _rw_feat_kernel(ps_ref, kb_ref, wup_ref, aup_ref, gup_ref, w0_ref, a0_ref, kk_w_ref, ka_w_ref,
                    logw_ref, kk_ref, kka_ref, kt_ref, g_ref):
    ps = ps_ref[...]
    kb = kb_ref[...]
    lw = w0_ref[...] + _dot(jnp.tanh(ps[:, 0:2 * B_LORA]), wup_ref[...])
    logw_ref[...] = -B_DECAY_SCALE * _sigmoid(lw)
    a = _sigmoid(a0_ref[...] + _dot(ps[:, 2 * B_LORA:4 * B_LORA], aup_ref[...]))
    g_ref[...] = _dot(_sigmoid(ps[:, 4 * B_LORA:]), gup_ref[...])
    kkw = kb * kk_w_ref[...]
    ones = _group_ones(1.0)
    for t in range(B_DIM // LANES):
        sl = slice(t * LANES, (t + 1) * LANES)
        x = kkw[:, sl]
        ss = _dot_exact_rhs(x * x, ones)
        kk_ref[:, sl] = x * lax.rsqrt(ss + 1e-6)
    kk = kk_ref[...]
    for dd in range(2):
        a_d = a[:, dd * B_DIM:(dd + 1) * B_DIM]
        kka_ref[:, dd * B_DIM:(dd + 1) * B_DIM] = kk * a_d
        kt_ref[:, dd * B_DIM:(dd + 1) * B_DIM] = kb * (1.0 + (a_d - 1.0) * ka_w_ref[...])


def rw_features(ps, rkv, wup, aup, gup, w0, a0, k_k, k_a, tm=256):
    m = ps.shape[0]
    tm = _row_tile(m, tm)
    full = lambda shp: pl.BlockSpec(shp, lambda i: (0, 0))
    two = jax.ShapeDtypeStruct((m, 2 * B_DIM), F32)
    one = jax.ShapeDtypeStruct((m, B_DIM), F32)
    return pl.pallas_call(
        _rw_feat_kernel,
        grid=(m // tm,),
        in_specs=[pl.BlockSpec((tm, 512), lambda i: (i, 0)),
                  pl.BlockSpec((tm, B_DIM), lambda i: (i, 1)),
                  full((2 * B_LORA, 2 * B_DIM)), full((2 * B_LORA, 2 * B_DIM)), full((256, B_DIM)),
                  full((1, 2 * B_DIM)), full((1, 2 * B_DIM)), full((1, B_DIM)), full((1, B_DIM))],
        out_specs=[pl.BlockSpec((tm, 2 * B_DIM), lambda i: (i, 0)),
                   pl.BlockSpec((tm, B_DIM), lambda i: (i, 0)),
                   pl.BlockSpec((tm, 2 * B_DIM), lambda i: (i, 0)),
                   pl.BlockSpec((tm, 2 * B_DIM), lambda i: (i, 0)),
                   pl.BlockSpec((tm, B_DIM), lambda i: (i, 0))],
        out_shape=[two, one, two, two, one],
        compiler_params=_cparams(("parallel",)),
        name="rw_features",
    )(ps, rkv, wup, aup, gup, w0, a0, k_k, k_a)


def _stack_heads(t, lane_lo):
    return jnp.concatenate([jnp.where(lane_lo, t, 0.0), jnp.where(lane_lo, 0.0, t)], axis=0)


def _side_masks(c, rev):
    ti = lax.broadcasted_iota(jnp.int32, (c, 2 * c), 0)
    tj = lax.broadcasted_iota(jnp.int32, (c, 2 * c), 1) & (c - 1)
    incl = (tj >= ti) if rev else (tj <= ti)
    eye = tj == ti
    return eye, incl, jnp.logical_and(incl, jnp.logical_not(eye))


def _side_inverse_many(n_mats, eye_f, lane_lo, steps):
    c = n_mats[0].shape[0]
    xs = [eye_f + n for n in n_mats]
    if steps == 0:
        return xs
    pws = [_dot(p, _stack_heads(p, lane_lo)) for p in n_mats]
    for _ in range(steps - 1):
        both = [_dot(jnp.concatenate([x, p], axis=0), _stack_heads(p, lane_lo)) for x, p in zip(xs, pws)]
        xs = [x + bth[0:c] for x, bth in zip(xs, both)]
        pws = [bth[c:2 * c] for bth in both]
    return [x + _dot(x, _stack_heads(p, lane_lo)) for x, p in zip(xs, pws)]


def _rwkv_kernel(r_ref, lw_ref, kk_ref, kka_ref, v_ref, kt_ref, s0_ref, y_ref, s_ref, *, rev, pairs):
    c = CHUNK
    nsub = r_ref.shape[0] // c

    @pl.when(pl.program_id(2) == 0)
    def _():
        s_ref[...] = s0_ref[...]

    _, incl, _ = _chunk_masks(c, rev)
    incl_f = incl.astype(F32)
    eye_s, incl_s, strict_s = _side_masks(c, rev)
    eye_sf = eye_s.astype(F32)
    lane_lo = lax.broadcasted_iota(jnp.int32, (1, LANES), 1) < B_N
    ii = lax.broadcasted_iota(jnp.int32, (LANES, LANES), 0)
    jj = lax.broadcasted_iota(jnp.int32, (LANES, LANES), 1)
    sh = int(math.log2(B_N))
    same_head = (ii >> sh) == (jj >> sh)
    steps = _nilpotent_steps(c)
    stk = lambda t: _stack_heads(t, lane_lo)
    rng = range(pairs)
    sls = [slice(p * LANES, (p + 1) * LANES) for p in rng]
    rows = [slice(ci * c, (ci + 1) * c) for ci in range(nsub)]
    items = [(ci, p) for ci in range(nsub) for p in rng]
    idx = {it: n for n, it in enumerate(items)}
    ids = range(len(items))

    lw_all = [lw_ref[rows[ci], :] for ci in range(nsub)]
    g_all = [_dot_exact_lhs(incl_f, lw_all[ci]) for ci in range(nsub)]
    gtot_all = [jnp.sum(lw_all[ci], axis=0, keepdims=True) for ci in range(nsub)]

    a_n, r_n, v_n, big, upd_rhs = [], [], [], [], []
    for ci, p in items:
        sl, rw = sls[p], rows[ci]
        g_in = g_all[ci][:, sl]
        e_neg = jnp.exp(-g_in)
        e_end = jnp.exp(gtot_all[ci][:, sl] - g_in)
        kk, kka, kt = kk_ref[rw, sl], kka_ref[rw, sl], kt_ref[rw, sl]
        a_n.append(kk * jnp.exp(g_in - lw_all[ci][:, sl]))
        r_n.append(r_ref[rw, sl] * jnp.exp(g_in))
        v_n.append(v_ref[rw, sl])
        upd_rhs.append(jnp.concatenate([-kka * e_end, kt * e_end], axis=0))
        bk_s = jnp.concatenate([stk(-kka * e_neg), stk(kt * e_neg)], axis=0)
        big.append(_dot_nt(jnp.concatenate([a_n[-1], r_n[-1]], axis=0), bk_s))
    l_ab = [jnp.where(strict_s, big[n][0:c, 0:LANES], 0.0) for n in ids]
    xs = _side_inverse_many(l_ab, eye_sf, lane_lo, steps)
    v_s = [stk(v_n[n]) for n in ids]
    lv = [_dot(jnp.where(strict_s, big[n][0:c, LANES:2 * LANES], 0.0), v_s[n]) for n in ids]
    wu = [_dot(xs[n], jnp.concatenate([stk(a_n[n]), stk(lv[n])], axis=1)) for n in ids]
    rq = [_dot(jnp.where(incl_s, big[n][c:2 * c, 0:LANES], 0.0),
               jnp.concatenate([stk(wu[n][:, 0:LANES]), stk(wu[n][:, LANES:2 * LANES])], axis=1)) for n in ids]
    y0 = [rq[n][:, LANES:2 * LANES] + _dot(jnp.where(incl_s, big[n][c:2 * c, LANES:2 * LANES], 0.0), v_s[n])
          for n in ids]
    rq_n = [r_n[n] + rq[n][:, 0:LANES] for n in ids]
    s_cur = [s_ref[0, p] for p in rng]
    for ci in (reversed(range(nsub)) if rev else range(nsub)):
        ns = [idx[(ci, p)] for p in rng]
        u = [_dot_nt(wu[ns[p]][:, 0:LANES], s_cur[p]) + wu[ns[p]][:, LANES:2 * LANES] for p in rng]
        for p in rng:
            y_ref[rows[ci], sls[p]] = _dot_nt(rq_n[ns[p]], s_cur[p]) + y0[ns[p]]
        add = [_dot_tn(jnp.concatenate([u[p], v_n[ns[p]]], axis=0), upd_rhs[ns[p]]) for p in rng]
        s_cur = [s_cur[p] * jnp.exp(gtot_all[ci][:, sls[p]]) + jnp.where(same_head, add[p], 0.0) for p in rng]
    for p in rng:
        s_ref[0, p] = s_cur[p]


def rwkv_scan(rkv, logw, kk, kka, kt, s0, *, batch, seq, rev, d, pairs=8, sub=2):
    blk = _scan_block(seq, sub)
    nc = seq // blk
    npair = B_DIM // LANES
    ng = npair // pairs
    w = pairs * LANES

    def row(b, c):
        return b * nc + ((nc - 1 - c) if rev else c)

    def tok(off_blocks):
        return pl.BlockSpec((blk, w), lambda b, h, c: (row(b, c), off_blocks + h))

    per_dir = d * ng
    st = pl.BlockSpec((1, pairs, LANES, LANES), lambda b, h, c: (b, h, 0, 0))
    return pl.pallas_call(
        functools.partial(_rwkv_kernel, rev=rev, pairs=pairs),
        grid=(batch, ng, nc),
        in_specs=[tok(0), tok(per_dir), tok(0), tok(per_dir), tok(2 * ng), tok(per_dir), st],
        out_specs=[tok(0), st],
        out_shape=[jax.ShapeDtypeStruct((batch * seq, B_DIM), F32),
                   jax.ShapeDtypeStruct((batch, npair, LANES, LANES), F32)],
        compiler_params=_cparams(("parallel", "parallel", "arbitrary")),
        name="rwkv_scan",
    )(rkv, logw, kk, kka, rkv, kt, s0)


def _ab_out_kernel(of_ref, ob_ref, yf_ref, yb_ref, z_ref, r_ref, v_ref, kt_ref, g_ref, x_ref, mod_ref,
                   gnw_ref, rk_ref, lnw_ref, lnb_ref, w_ref, o_ref, mix_ref, *, gate_idx):
    for h in range(A_HEADS):
        sl = slice(h * A_DK, (h + 1) * A_DK)
        o = of_ref[:, sl] + ob_ref[:, sl]
        ms = jnp.mean(o * o, axis=-1, keepdims=True)
        ya = o * lax.rsqrt(ms + NORM_EPS) * gnw_ref[...]
        mix_ref[:, sl] = (ya * _silu(z_ref[:, sl])).astype(BF16)
    avg = _group_ones(1.0 / B_N)
    ones = _group_ones(1.0)
    for t in range(B_DIM // LANES):
        sl = slice(t * LANES, (t + 1) * LANES)
        y = yf_ref[:, sl] + yb_ref[:, sl]
        mu = _dot_exact_rhs(y, avg)
        dlt = y - mu
        var = _dot_exact_rhs(dlt * dlt, avg)
        yn = dlt * lax.rsqrt(var + B_GN_EPS)
        kt_sum = kt_ref[:, sl] + kt_ref[:, B_DIM + t * LANES:B_DIM + (t + 1) * LANES]
        bonus = _dot_exact_rhs(r_ref[:, sl] * kt_sum * rk_ref[:, sl], ones) * v_ref[:, sl]
        yb = (yn * lnw_ref[:, sl] + lnb_ref[:, sl] + bonus) * g_ref[:, sl]
        mix_ref[:, A_DIM + t * LANES:A_DIM + (t + 1) * LANES] = yb.astype(BF16)
    acc = jnp.dot(mix_ref[...], w_ref[...], preferred_element_type=F32)
    o_ref[...] = x_ref[...] + mod_ref[0, gate_idx:gate_idx + 1, :] * acc


def ab_out(o_f, o_b, y_f, y_b, p_main, rkv, kt, g_out, x, mod, gnw, rk, lnw, lnb, w_out, *,
           gate_idx, rows_per_mod, tm=256):
    m, dm = x.shape
    tm = _row_tile(min(m, rows_per_mod), tm)
    tpm = rows_per_mod // tm
    zblk = CONV_CH // A_DIM
    row = lambda width, blk=0: pl.BlockSpec((tm, width), lambda i: (i, blk))
    full = lambda shp: pl.BlockSpec(shp, lambda i: (0, 0))
    return pl.pallas_call(
        functools.partial(_ab_out_kernel, gate_idx=gate_idx),
        grid=(m // tm,),
        in_specs=[row(A_DIM), row(A_DIM), row(B_DIM), row(B_DIM), row(A_DIM, zblk),
                  row(B_DIM, 0), row(B_DIM, 2), row(2 * B_DIM), row(B_DIM), row(dm),
                  pl.BlockSpec((1, MOD_ROWS, dm), lambda i: (i // tpm, 0, 0)),
                  full((1, A_DK)), full((1, B_DIM)), full((1, B_DIM)), full((1, B_DIM)),
                  pl.BlockSpec((A_DIM + B_DIM, dm), lambda i: (0, 0), pipeline_mode=pl.Buffered(1))],
        out_specs=row(dm),
        out_shape=jax.ShapeDtypeStruct((m, dm), F32),
        scratch_shapes=[pltpu.VMEM((tm, A_DIM + B_DIM), BF16)],
        compiler_params=_cparams(("parallel",)),
        name="ab_out",
    )(o_f, o_b, y_f, y_b, p_main, rkv, rkv, kt, g_out, x, mod, gnw, rk, lnw, lnb, w_out)


def _mlstm_kernel(q_ref, k_ref, v_ref, g_ref, ib_ref, fb_ref, c0_ref, n0_ref, m0_ref,
                  h_ref, c_ref, n_ref, m_ref, *, rev, d, heads):
    c = CHUNK
    nsub = q_ref.shape[0] // c

    @pl.when(pl.program_id(2) == 0)
    def _():
        c_ref[...] = c0_ref[...]
        n_ref[...] = n0_ref[...]
        m_ref[...] = m0_ref[...]

    eye, incl, incl_t = _chunk_masks(c, rev)
    gl = g_ref[...]
    cap = C_GATE_CAP
    ig_all = cap * jnp.tanh((gl + ib_ref[...]) / cap)
    fz = cap * jnp.tanh((gl + fb_ref[...]) / cap)
    logf_all = jnp.minimum(fz, 0.0) - jnp.log(1.0 + jnp.exp(-jnp.abs(fz)))
    rng = range(heads)
    qs = [slice(h * C_DK, (h + 1) * C_DK) for h in rng]
    vs = [slice(h * C_DV, (h + 1) * C_DV) for h in rng]
    rows = [slice(ci * c, (ci + 1) * c) for ci in range(nsub)]
    order = list(reversed(range(nsub))) if rev else list(range(nsub))
    items = [(ci, h) for ci in range(nsub) for h in rng]
    idx = {it: n for n, it in enumerate(items)}
    ids = range(len(items))
    q = [q_ref[rows[ci], qs[h]] * (C_DK ** -0.5) for ci, h in items]
    qk = [_dot_nt(q[n], k_ref[rows[ci], qs[h]]) for n, (ci, h) in enumerate(items)]
    lane = lax.broadcasted_iota(jnp.int32, (1, LANES), 1)
    gate_col = lambda mat, ci, i: jnp.sum(jnp.where(lane == i, mat[rows[ci]], 0.0), axis=1, keepdims=True)
    bcast = lambda col: jnp.broadcast_to(col, (c, c))
    ig =[gate_col(ig_all, ci, d * C_HEADS + h) for ci, h in items]
    logf = [gate_col(logf_all, ci, 2 * C_HEADS + d * C_HEADS + h) for ci, h in items]
    f_row = [jnp.sum(jnp.where(eye, bcast(logf[n]), 0.0), axis=0, keepdims=True) for n in ids]
    ig_row = [jnp.sum(jnp.where(eye, bcast(ig[n]), 0.0), axis=0, keepdims=True) for n in ids]
    b_row = [jnp.sum(jnp.where(incl_t, bcast(logf[n]), 0.0), axis=0, keepdims=True) for n in ids]
    b_col = [jnp.sum(jnp.where(incl, jnp.broadcast_to(f_row[n], (c, c)), 0.0), axis=1, keepdims=True) for n in ids]
    btot = [jnp.sum(logf[n], axis=0, keepdims=True) for n in ids]
    w_end = [btot[n] - b_col[n] + ig[n] for n in ids]
    w_max = [jnp.max(w_end[n], axis=0, keepdims=True) for n in ids]
    dmat = [b_col[n] - b_row[n] + ig_row[n] for n in ids]
    dmax = [jnp.max(jnp.where(incl, dmat[n], -1e30), axis=1, keepdims=True) for n in ids]
    m_st, m_new = [None] * len(items), [None] * len(items)
    m_cur = [m_ref[0, h, 0:1, 0:1] for h in rng]
    for ci in order:
        for h in rng:
            n = idx[(ci, h)]
            m_st[n] = m_cur[h]
            m_new[n] = jnp.maximum(btot[n] + m_cur[h], w_max[n])
            m_cur[h] = m_new[n]
    scale = [jnp.exp(btot[n] + m_st[n] - m_new[n]) for n in ids]
    kw = [k_ref[rows[ci], qs[h]] * jnp.exp(w_end[n] - m_new[n]) for n, (ci, h) in enumerate(items)]
    kw_sum = [jnp.sum(kw[n], axis=0, keepdims=True) for n in ids]
    m_row = [jnp.maximum(b_col[n] + m_st[n], dmax[n]) for n in ids]
    inter = [jnp.exp(b_col[n] + m_st[n] - m_row[n]) for n in ids]
    s = [jnp.where(incl, qk[n] * jnp.exp(jnp.where(incl, dmat[n] - m_row[n], 0.0)), 0.0) for n in ids]
    s_sum = [jnp.sum(s[n], axis=1, keepdims=True) for n in ids]
    floor = [jnp.exp(-m_row[n]) for n in ids]
    sv = [_dot(s[n], v_ref[rows[ci], vs[h]]) for n, (ci, h) in enumerate(items)]
    kv = [_dot_tn(kw[n], v_ref[rows[ci], vs[h]]) for n, (ci, h) in enumerate(items)]
    c_cur = [c_ref[0, h] for h in rng]
    n_cur = [n_ref[0, h, 0:1, :] for h in rng]
    for ci in order:
        ns = [idx[(ci, h)] for h in rng]
        qc = [_dot(q[ns[h]], c_cur[h]) for h in rng]
        for h in rng:
            n = ns[h]
            den = s_sum[n] + inter[n] * jnp.sum(q[n] * n_cur[h], axis=1, keepdims=True)
            num = sv[n] + inter[n] * qc[h]
            h_ref[rows[ci], vs[h]] = num / jnp.maximum(jnp.abs(den), floor[n])
        c_cur = [c_cur[h] * scale[ns[h]] + kv[ns[h]] for h in rng]
        n_cur = [n_cur[h] * scale[ns[h]] + kw_sum[ns[h]] for h in rng]
    for h in rng:
        c_ref[0, h] = c_cur[h]
        n_ref[0, h] = jnp.broadcast_to(n_cur[h], (SUBLANES, C_DK))
        m_ref[0, h] = jnp.broadcast_to(m_cur[h], (SUBLANES, LANES))


def mlstm_scan(p, gates, ib_vec, fb_vec, c0, n0, m0, *, batch, seq, rev, d, sub=2):
    blk = _scan_block(seq, sub)
    nc = seq // blk
    heads, ng = C_HEADS, 1

    def row(b, c):
        return b * nc + ((nc - 1 - c) if rev else c)

    wq, wv = heads * C_DK, heads * C_DV
    qspec = pl.BlockSpec((blk, wq), lambda b, h, c: (row(b, c), h))
    kspec = pl.BlockSpec((blk, wq), lambda b, h, c: (row(b, c), ng + h))
    vspec = pl.BlockSpec((blk, wv), lambda b, h, c: (row(b, c), (2 * C_QK) // wv + h))
    hspec = pl.BlockSpec((blk, wv), lambda b, h, c: (row(b, c), h))
    vec = pl.BlockSpec((1, LANES), lambda b, h, c: (0, 0))
    cst = pl.BlockSpec((1, heads, C_DK, C_DV), lambda b, h, c: (b, h, 0, 0))
    nst = pl.BlockSpec((1, heads, SUBLANES, C_DK), lambda b, h, c: (b, h, 0, 0))
    mst = pl.BlockSpec((1, heads, SUBLANES, LANES), lambda b, h, c: (b, h, 0, 0))
    return pl.pallas_call(
        functools.partial(_mlstm_kernel, rev=rev, d=d, heads=heads),
        grid=(batch, ng, nc),
        in_specs=[qspec, kspec, vspec,
                  pl.BlockSpec((blk, LANES), lambda b, h, c: (row(b, c), 0)),
                  vec, vec, cst, nst, mst],
        out_specs=[hspec, cst, nst, mst],
        out_shape=[jax.ShapeDtypeStruct((batch * seq, C_V), F32),
                   jax.ShapeDtypeStruct((batch, C_HEADS, C_DK, C_DV), F32),
                   jax.ShapeDtypeStruct((batch, C_HEADS, SUBLANES, C_DK), F32),
                   jax.ShapeDtypeStruct((batch, C_HEADS, SUBLANES, LANES), F32)],
        compiler_params=_cparams(("parallel", "parallel", "arbitrary")),
        name="mlstm_scan",
    )(p, p, p, gates, ib_vec, fb_vec, c0, n0, m0)


def _ml_out_kernel(hf_ref, hb_ref, o_ref_in, x_ref, mod_ref, nw_ref, w_ref, out_ref, mix_ref, *, gate_idx):
    for h in range(C_HEADS):
        sl = slice(h * C_DV, (h + 1) * C_DV)
        hh = hf_ref[:, sl] + hb_ref[:, sl]
        ms = jnp.mean(hh * hh, axis=-1, keepdims=True)
        y = hh * lax.rsqrt(ms + NORM_EPS) * nw_ref[:, sl]
        mix_ref[:, sl] = (y * _sigmoid(o_ref_in[:, sl])).astype(BF16)
    acc = jnp.dot(mix_ref[...], w_ref[...], preferred_element_type=F32)
    out_ref[...] = x_ref[...] + mod_ref[0, gate_idx:gate_idx + 1, :] * acc


def ml_out(h_f, h_b, p, x, mod, nw, w_out, *, gate_idx, rows_per_mod, tm=256):
    m, dm = x.shape
    tm = _row_tile(min(m, rows_per_mod), tm)
    tpm = rows_per_mod // tm
    oblk = (2 * C_QK + C_V) // C_V
    row = lambda width, blk=0: pl.BlockSpec((tm, width), lambda i: (i, blk))
    full = lambda shp: pl.BlockSpec(shp, lambda i: (0, 0))
    return pl.pallas_call(
        functools.partial(_ml_out_kernel, gate_idx=gate_idx),
        grid=(m // tm,),
        in_specs=[row(C_V), row(C_V), row(C_V, oblk), row(dm),
                  pl.BlockSpec((1, MOD_ROWS, dm), lambda i: (i // tpm, 0, 0)),
                  full((1, C_V)), pl.BlockSpec((C_V, dm), lambda i: (0, 0), pipeline_mode=pl.Buffered(1))],
        out_specs=row(dm),
        out_shape=jax.ShapeDtypeStruct((m, dm), F32),
        scratch_shapes=[pltpu.VMEM((tm, C_V), BF16)],
        compiler_params=_cparams(("parallel",)),
        name="ml_out",
    )(h_f, h_b, p, x, mod, nw, w_out)


def _lane_vec(values, offset):
    flat = values.reshape(-1).astype(F32)
    return jnp.zeros((1, LANES), F32).at[0, offset:offset + flat.shape[0]].set(flat)


def _ffn(x, mod, nw, w1, w3, w2, rows_per_mod):
    g = ffn_up(x, mod, nw, w1, w3, shift_idx=3, scale_idx=4, rows_per_mod=rows_per_mod)
    return mm_res(g, w2, x, mod, gate_idx=5, rows_per_mod=rows_per_mod)


def _ab_layer(streams, prm, ctx_out):
    w_in = prm["w_in"]
    w_all = w_in.astype(BF16)
    small = w_in[:, CONV_CH + A_DIM + 4 * A_HEADS:]
    gates_w = w_in[:, CONV_CH + A_DIM:CONV_CH + A_DIM + 4 * A_HEADS]
    d_model = w_in.shape[0]
    w_small = jnp.concatenate(
        [small, jnp.zeros((d_model, 512 - small.shape[1]), F32),
         gates_w, jnp.zeros((d_model, LANES - gates_w.shape[1]), F32)], axis=1).astype(BF16)
    conv_w = jnp.concatenate([prm["conv_w"].reshape(9, CONV_CH), jnp.zeros((7, CONV_CH), F32)], axis=0)
    a_vec = _lane_vec(prm["a_log"], 0)
    dt_vec = _lane_vec(prm["dt_bias"], 0)
    zero_up = jnp.zeros((B_LORA, B_DIM), F32)
    blockdiag = lambda u: jnp.concatenate(
        [jnp.concatenate([u[0], zero_up], axis=1), jnp.concatenate([zero_up, u[1]], axis=1)], axis=0).astype(BF16)
    wup, aup = blockdiag(prm["w_up"]), blockdiag(prm["a_up"])
    gup = jnp.concatenate([prm["g_up"], jnp.zeros((256 - B_G_LORA, B_DIM), F32)], axis=0).astype(BF16)
    w0 = prm["w0"].reshape(1, 2 * B_DIM)
    a0 = prm["a0"].reshape(1, 2 * B_DIM)
    k_k = prm["k_k"].reshape(1, B_DIM)
    k_a = prm["k_a"].reshape(1, B_DIM)
    gnw = prm["gdn_norm_w"].reshape(1, A_DK)
    rk = prm["r_k"].reshape(1, B_DIM)
    lnw = prm["ln_w"].reshape(1, B_DIM)
    lnb = prm["ln_b"].reshape(1, B_DIM)
    w_out = prm["w_out"].astype(BF16)

    feats = []
    for st in streams:
        kw = dict(shift_idx=0, scale_idx=1, rows_per_mod=st["rpm"])
        p_main, p_small = modmm(st["x"], st["mod"], prm["norm_w"], w_all, w_small, n_main=CONV_CH + A_DIM, **kw)
        ckw = dict(batch=st["batch"], seq=st["seq"], rows=st["rows"])
        q = grid_conv(p_main, conv_w, col0=0, ncol=A_DIM, mode="q", **ckw)
        k = grid_conv(p_main, conv_w, col0=A_DIM, ncol=A_DIM, mode="k", **ckw)
        v = grid_conv(p_main, conv_w, col0=2 * A_DIM, ncol=A_DIM, mode="v", **ckw)
        rkv = grid_conv(p_main, conv_w, col0=3 * A_DIM, ncol=3 * B_DIM, mode="raw", **ckw)
        logw, kk, kka, kt, g_out = rw_features(p_small, rkv, wup, aup, gup, w0, a0, k_k, k_a)
        feats.append(dict(p_main=p_main, p_small=p_small, q=q, k=k, v=v, rkv=rkv,
                          logw=logw, kk=kk, kka=kka, kt=kt, g_out=g_out))

    nb = streams[-1]["batch"]
    outs = [dict() for _ in streams]
    for d in range(2):
        rev = d == 1
        s_a = jnp.zeros((nb, A_HEADS, A_DK, A_DK), F32)
        s_b = jnp.zeros((nb, B_DIM // LANES, LANES, LANES), F32)
        for si, (st, f) in enumerate(zip(streams, feats)):
            skw = dict(batch=st["batch"], seq=st["seq"], rev=rev, d=d)
            o, s_a = gdn_scan(f["q"], f["k"], f["v"], f["p_small"], a_vec, dt_vec, s_a, gate_blk=4, **skw)
            y, s_b = rwkv_scan(f["rkv"], f["logw"], f["kk"], f["kka"], f["kt"], s_b, **skw)
            outs[si]["o%d" % d] = o
            outs[si]["y%d" % d] = y

    new_x = []
    for si, (st, f) in enumerate(zip(streams, feats)):
        if si == 0 and not ctx_out:
            new_x.append(None)
            continue
        o = outs[si]
        new_x.append(ab_out(o["o0"], o["o1"], o["y0"], o["y1"], f["p_main"], f["rkv"], f["kt"], f["g_out"],
                            st["x"], st["mod"], gnw, rk, lnw, lnb, w_out, gate_idx=2, rows_per_mod=st["rpm"]))
    return new_x


def _ml_layer(streams, prm, ctx_out):
    w_in = prm["w_in"]
    main_cols = 2 * C_QK + 2 * C_V
    w_all = w_in.astype(BF16)
    d_model = w_in.shape[0]
    w_g = jnp.concatenate([w_in[:, main_cols:], jnp.zeros((d_model, LANES - 4 * C_HEADS), F32)], axis=1).astype(BF16)
    ib_vec = _lane_vec(prm["i_bias"], 0)
    fb_vec = _lane_vec(prm["f_bias"], 2 * C_HEADS)
    nw = prm["ml_norm_w"].reshape(1, C_V)
    w_out = prm["w_out"].astype(BF16)

    feats = []
    for st in streams:
        kw = dict(shift_idx=0, scale_idx=1, rows_per_mod=st["rpm"])
        p_main, p_g = modmm(st["x"], st["mod"], prm["norm_w"], w_all, w_g, n_main=main_cols, **kw)
        feats.append(dict(p_main=p_main, p_g=p_g))

    nb = streams[-1]["batch"]
    outs = [dict() for _ in streams]
    for d in range(2):
        rev = d == 1
        c_st = jnp.zeros((nb, C_HEADS, C_DK, C_DV), F32)
        n_st = jnp.zeros((nb, C_HEADS, SUBLANES, C_DK), F32)
        m_st = jnp.zeros((nb, C_HEADS, SUBLANES, LANES), F32)
        for si, (st, f) in enumerate(zip(streams, feats)):
            h, c_st, n_st, m_st = mlstm_scan(f["p_main"], f["p_g"], ib_vec, fb_vec, c_st, n_st, m_st,
                                             batch=st["batch"], seq=st["seq"], rev=rev, d=d)
            outs[si]["h%d" % d] = h

    new_x = []
    for si, (st, f) in enumerate(zip(streams, feats)):
        if si == 0 and not ctx_out:
            new_x.append(None)
            continue
        new_x.append(ml_out(outs[si]["h0"], outs[si]["h1"], f["p_main"], st["x"], st["mod"], nw, w_out,
                            gate_idx=2, rows_per_mod=st["rpm"]))
    return new_x


def kernel(x, c, ctx, c_ctx, ada_w, ada_b, norm_w, ab_w_in, ab_conv_w, gdn_a_log, gdn_dt_bias, gdn_norm_w, rw_w0, rw_w_up, rw_a0, rw_a_up, rw_g_up, rw_k_k, rw_k_a, rw_r_k, rw_ln_w, rw_ln_b, ab_w_out, ml_w_in, ml_i_bias, ml_f_bias, ml_norm_w, ml_w_out, ffn_w1, ffn_w3, ffn_w2, final_norm_w):
    bsz, seq, dm = x.shape
    ctx_len = ctx.shape[1]
    depth = ada_w.shape[0]
    xl = x.reshape(bsz * seq, dm)
    xc = ctx.reshape(bsz * ctx_len, dm)
    cond = jnp.zeros((MOD_ROWS, dm), F32).at[:bsz].set(c).at[bsz].set(c_ctx)

    for i in range(depth):
        ctx_out = i < depth - 1
        j = i // 2
        mod_all = adaln(cond, ada_w[i], ada_b[i].reshape(1, -1)).reshape(MOD_ROWS, 6, dm)
        mod_all = jnp.pad(mod_all, ((0, 0), (0, MOD_ROWS - 6), (0, 0)))
        streams = [
            dict(x=xc, mod=mod_all[bsz:bsz + 1], batch=bsz, seq=ctx_len, rows=1, rpm=bsz * ctx_len),
            dict(x=xl, mod=mod_all[:bsz], batch=bsz, seq=seq, rows=seq // GRID_W, rpm=seq),
        ]
        if i % 2 == 0:
            prm = dict(w_in=ab_w_in[j], conv_w=ab_conv_w[j], a_log=gdn_a_log[j], dt_bias=gdn_dt_bias[j],
                       gdn_norm_w=gdn_norm_w[j], w0=rw_w0[j], w_up=rw_w_up[j], a0=rw_a0[j], a_up=rw_a_up[j],
                       g_up=rw_g_up[j], k_k=rw_k_k[j], k_a=rw_k_a[j], r_k=rw_r_k[j], ln_w=rw_ln_w[j],
                       ln_b=rw_ln_b[j], w_out=ab_w_out[j], norm_w=norm_w[i, 0].reshape(1, dm))
            xc_new, xl = _ab_layer(streams, prm, ctx_out)
        else:
            prm = dict(w_in=ml_w_in[j], i_bias=ml_i_bias[j], f_bias=ml_f_bias[j], ml_norm_w=ml_norm_w[j],
                       w_out=ml_w_out[j], norm_w=norm_w[i, 0].reshape(1, dm))
            xc_new, xl = _ml_layer(streams, prm, ctx_out)
        w1, w3, w2 = ffn_w1[i].astype(BF16), ffn_w3[i].astype(BF16), ffn_w2[i].astype(BF16)
        nw2 = norm_w[i, 1].reshape(1, dm)
        xl = _ffn(xl, streams[1]["mod"], nw2, w1, w3, w2, streams[1]["rpm"])
        if ctx_out:
            xc = _ffn(xc_new, streams[0]["mod"], nw2, w1, w3, w2, streams[0]["rpm"])
    return final_norm(xl, final_norm_w.reshape(1, dm)).reshape(bsz, seq, dm)
```

```python
import functools
import math

import jax
import jax.numpy as jnp
from jax import lax
from jax.experimental import pallas as pl
from jax.experimental.pallas import tpu as pltpu

F32 = jnp.float32
BF16 = jnp.bfloat16

NORM_EPS = 1e-6
GRID_W = 64
LANES = 128
SUBLANES = 8
VMEM_LIMIT = 56 * 1024 * 1024

A_HEADS, A_DK = 8, 128
A_DIM = A_HEADS * A_DK
B_HEADS, B_N = 16, 64
B_DIM = B_HEADS * B_N
B_LORA = 64
B_G_LORA = 160
B_GN_EPS = 64e-5
B_DECAY_SCALE = math.exp(-0.5)
C_HEADS, C_DK, C_DV = 8, 128, 256
C_QK = C_HEADS * C_DK
C_V = C_HEADS * C_DV
C_GATE_CAP = 15.0
CONV_CH = 3 * A_DIM + 3 * B_DIM
CHUNK = 64
MOD_ROWS = 8


def _cparams(sem):
    return pltpu.CompilerParams(dimension_semantics=sem, vmem_limit_bytes=VMEM_LIMIT)


def _dot(a, b):
    return jnp.dot(a.astype(BF16), b.astype(BF16), preferred_element_type=F32)


def _dot_nt(a, b):
    return lax.dot_general(a.astype(BF16), b.astype(BF16), (((1,), (1,)), ((), ())),
                           preferred_element_type=F32)


def _dot_tn(a, b):
    return lax.dot_general(a.astype(BF16), b.astype(BF16), (((0,), (0,)), ((), ())),
                           preferred_element_type=F32)


def _split2(a):
    hi = a.astype(BF16)
    lo = (a - hi.astype(F32)).astype(BF16)
    return hi, lo


def _split3(a):
    hi = a.astype(BF16)
    r = a - hi.astype(F32)
    mid = r.astype(BF16)
    lo = (r - mid.astype(F32)).astype(BF16)
    return hi, mid, lo


def _dot_exact_rhs(a, b_exact):
    hi, lo = _split2(a)
    b = b_exact.astype(BF16)
    d = lambda t: jnp.dot(t, b, preferred_element_type=F32)
    return d(hi) + d(lo)


def _dot_exact_lhs(a_exact, b):
    hi, mid, lo = _split3(b)
    a = a_exact.astype(BF16)
    d = lambda t: jnp.dot(a, t, preferred_element_type=F32)
    return d(hi) + d(mid) + d(lo)


def _dot3(a, b):
    ah, al = _split2(a)
    bh, bl = _split2(b)
    d = lambda s, t: jnp.dot(s, t, preferred_element_type=F32)
    return d(ah, bh) + d(al, bh) + d(ah, bl)


def _sigmoid(t):
    return 1.0 / (1.0 + jnp.exp(-t))


def _silu(t):
    return t * _sigmoid(t)


def _softplus(t):
    return jnp.maximum(t, 0.0) + jnp.log(1.0 + jnp.exp(-jnp.abs(t)))


def _unit_lower_inverse_many(n_mats, eye_f, steps):
    xs = [eye_f + n for n in n_mats]
    if steps == 0:
        return xs
    r = n_mats[0].shape[0]
    pws = [_dot(p, p) for p in n_mats]
    for _ in range(steps - 1):
        both = [_dot(jnp.concatenate([x, p], axis=0), p) for x, p in zip(xs, pws)]
        xs = [x + b[0:r] for x, b in zip(xs, both)]
        pws = [b[r:2 * r] for b in both]
    return [x + _dot(x, p) for x, p in zip(xs, pws)]


def _pair_masks(c, rev):
    c2 = 2 * c
    ii = lax.broadcasted_iota(jnp.int32, (c2, c2), 0)
    jj = lax.broadcasted_iota(jnp.int32, (c2, c2), 1)
    sh = int(math.log2(c))
    same = (ii >> sh) == (jj >> sh)
    ti, tj = ii & (c - 1), jj & (c - 1)
    incl2 = jnp.logical_and(same, (tj >= ti) if rev else (tj <= ti))
    incl2_t = jnp.logical_and(same, (tj <= ti) if rev else (tj >= ti))
    eye2 = ii == jj
    strict2 = jnp.logical_and(incl2, jnp.logical_not(eye2))
    return eye2, incl2, incl2_t, strict2


def _nilpotent_steps(c):
    return max(int(math.ceil(math.log2(c))) - 1, 0)


def _adaln_kernel(c_ref, w_ref, b_ref, o_ref):
    o_ref[...] = _dot3(_silu(c_ref[...]), w_ref[...]) + b_ref[...]


def adaln(cond, w, b, layer):
    m, d = cond.shape
    e = w.shape[2]
    tn = _col_tile(e, 512)
    return pl.pallas_call(
        _adaln_kernel,
        grid=(e // tn,),
        in_specs=[pl.BlockSpec((m, d), lambda j: (0, 0)),
                  pl.BlockSpec((None, d, tn), lambda j: (layer, 0, j)),
                  pl.BlockSpec((1, tn), lambda j: (0, j))],
        out_specs=pl.BlockSpec((m, tn), lambda j: (0, j)),
        out_shape=jax.ShapeDtypeStruct((m, e), F32),
        compiler_params=_cparams(("parallel",)),
        name="adaln",
    )(cond, w, b)


def _modulated(x_ref, mod_ref, nw_ref, shift_idx, scale_idx):
    x = x_ref[...]
    ms = jnp.mean(x * x, axis=-1, keepdims=True)
    gain = nw_ref[...] * (1.0 + mod_ref[0, scale_idx:scale_idx + 1, :])
    return x * lax.rsqrt(ms + NORM_EPS) * gain + mod_ref[0, shift_idx:shift_idx + 1, :]


def _modmm_kernel(x_ref, mod_ref, nw_ref, w_ref, ws_ref, o_ref, os_ref, h_ref, *, shift_idx, scale_idx, nmain):
    j = pl.program_id(1)

    @pl.when(j == 0)
    def _():
        h_ref[...] = _modulated(x_ref, mod_ref, nw_ref, shift_idx, scale_idx).astype(BF16)

    @pl.when(j < nmain)
    def _():
        o_ref[...] = jnp.dot(h_ref[...], w_ref[...], preferred_element_type=F32)

    @pl.when(j == nmain)
    def _():
        os_ref[...] = jnp.dot(h_ref[...], ws_ref[...], preferred_element_type=F32)


def _row_tile(m, want):
    t = min(want, m)
    while m % t:
        t //= 2
    return t


def _col_tile(n, want):
    t = min(want, n)
    while n % t or t % LANES:
        t -= LANES
    return t


def modmm(x, mod, nw, w, w_small, *, n_main, shift_idx, scale_idx, rows_per_mod, tm=1024, tn=1024):
    m, d = x.shape
    ns = w_small.shape[1]
    tm = _row_tile(min(m, rows_per_mod), tm)
    tn = _col_tile(n_main, tn)
    tpm = rows_per_mod // tm
    nmain = n_main // tn
    main_col = lambda j: jnp.minimum(j, nmain - 1)
    return pl.pallas_call(
        functools.partial(_modmm_kernel, shift_idx=shift_idx, scale_idx=scale_idx, nmain=nmain),
        grid=(m // tm, nmain + 1),
        in_specs=[pl.BlockSpec((tm, d), lambda i, j: (i, 0)),
                  pl.BlockSpec((1, MOD_ROWS, d), lambda i, j: (i // tpm, 0, 0)),
                  pl.BlockSpec((1, d), lambda i, j: (0, 0)),
                  pl.BlockSpec((d, tn), lambda i, j: (0, main_col(j))),
                  pl.BlockSpec((d, ns), lambda i, j: (0, 0))],
        out_specs=[pl.BlockSpec((tm, tn), lambda i, j: (i, main_col(j))),
                   pl.BlockSpec((tm, ns), lambda i, j: (i, 0))],
        out_shape=[jax.ShapeDtypeStruct((m, n_main), F32), jax.ShapeDtypeStruct((m, ns), F32)],
        scratch_shapes=[pltpu.VMEM((tm, d), BF16)],
        compiler_params=_cparams(("parallel", "arbitrary")),
        name="modmm",
    )(x, mod, nw, w, w_small)


def _ffn_up_kernel(x_ref, mod_ref, nw_ref, w1_ref, w3_ref, o_ref, h_ref, *, shift_idx, scale_idx):
    @pl.when(pl.program_id(1) == 0)
    def _():
        h_ref[...] = _modulated(x_ref, mod_ref, nw_ref, shift_idx, scale_idx).astype(BF16)

    h = h_ref[...]
    a = jnp.dot(h, w1_ref[...].astype(BF16), preferred_element_type=F32)
    b = jnp.dot(h, w3_ref[...].astype(BF16), preferred_element_type=F32)
    o_ref[...] = (_silu(a) * b).astype(BF16)


def ffn_up(x, mod, nw, w1, w3, layer, *, shift_idx, scale_idx, rows_per_mod, tm=1024, tn=512):
    m, d = x.shape
    n = w1.shape[2]
    tm = _row_tile(min(m, rows_per_mod), tm)
    tn = _col_tile(n, tn)
    tpm = rows_per_mod // tm
    return pl.pallas_call(
        functools.partial(_ffn_up_kernel, shift_idx=shift_idx, scale_idx=scale_idx),
        grid=(m // tm, n // tn),
        in_specs=[pl.BlockSpec((tm, d), lambda i, j: (i, 0)),
                  pl.BlockSpec((1, MOD_ROWS, d), lambda i, j: (i // tpm, 0, 0)),
                  pl.BlockSpec((1, d), lambda i, j: (0, 0)),
                  pl.BlockSpec((None, d, tn), lambda i, j: (layer, 0, j)),
                  pl.BlockSpec((None, d, tn), lambda i, j: (layer, 0, j))],
        out_specs=pl.BlockSpec((tm, tn), lambda i, j: (i, j)),
        out_shape=jax.ShapeDtypeStruct((m, n), BF16),
        scratch_shapes=[pltpu.VMEM((tm, d), BF16)],
        compiler_params=_cparams(("parallel", "arbitrary")),
        name="ffn_up",
    )(x, mod, nw, w1, w3)


def _mm_res_kernel(a_ref, w_ref, res_ref, mod_ref, o_ref, *, gate_idx):
    acc = jnp.dot(a_ref[...], w_ref[...], preferred_element_type=F32)
    o_ref[...] = res_ref[...] + mod_ref[0, gate_idx:gate_idx + 1, :] * acc


def mm_res(a, w, res, mod, layer, *, gate_idx, rows_per_mod, tm=1024, tn=512):
    m, k = a.shape
    n = w.shape[2]
    tm = _row_tile(min(m, rows_per_mod), tm)
    tn = _col_tile(n, tn)
    tpm = rows_per_mod // tm
    return pl.pallas_call(
        functools.partial(_mm_res_kernel, gate_idx=gate_idx),
        grid=(m // tm, n // tn),
        in_specs=[pl.BlockSpec((tm, k), lambda i, j: (i, 0)),
                  pl.BlockSpec((None, k, tn), lambda i, j: (layer, 0, j)),
                  pl.BlockSpec((tm, tn), lambda i, j: (i, j)),
                  pl.BlockSpec((1, MOD_ROWS, tn), lambda i, j: (i // tpm, 0, j))],
        out_specs=pl.BlockSpec((tm, tn), lambda i, j: (i, j)),
        out_shape=jax.ShapeDtypeStruct((m, n), F32),
        compiler_params=_cparams(("parallel", "arbitrary")),
        name="mm_res",
    )(a, w, res, mod)


def _final_norm_kernel(x_ref, w_ref, o_ref):
    x = x_ref[...]
    ms = jnp.mean(x * x, axis=-1, keepdims=True)
    o_ref[...] = x * lax.rsqrt(ms + NORM_EPS) * w_ref[...]


def final_norm(x, w, tm=512):
    m, d = x.shape
    tm = _row_tile(m, tm)
    return pl.pallas_call(
        _final_norm_kernel,
        grid=(m // tm,),
        in_specs=[pl.BlockSpec((tm, d), lambda i: (i, 0)), pl.BlockSpec((1, d), lambda i: (0, 0))],
        out_specs=pl.BlockSpec((tm, d), lambda i: (i, 0)),
        out_shape=jax.ShapeDtypeStruct((m, d), F32),
        compiler_params=_cparams(("parallel",)),
        name="final_norm",
    )(x, w)


CONV_ROWS = 256


def _conv_kernel(p_ref, w_ref, o_ref, s_ref, *, seq, rows, cols, pad, mode):
    zeros = jnp.zeros((pad, LANES), F32)
    for t in range(3):
        s_ref[t, 0:pad, :] = zeros
        s_ref[t, pad + seq:pad + seq + pad, :] = zeros
    s_ref[1, pad:pad + seq, :] = p_ref[...]
    rc = min(CONV_ROWS, seq)
    for c in range(seq // rc):
        base = c * rc
        col = (lax.broadcasted_iota(jnp.int32, (rc, 1), 0) + base) & (cols - 1)
        left = s_ref[1, pad - 1 + base:pad - 1 + base + rc, :]
        right = s_ref[1, pad + 1 + base:pad + 1 + base + rc, :]
        s_ref[0, pad + base:pad + base + rc, :] = jnp.where(col == 0, 0.0, left)
        s_ref[2, pad + base:pad + base + rc, :] = jnp.where(col == cols - 1, 0.0, right)
    drs = (0, 1, 2) if rows > 1 else (1,)
    for c in range(seq // rc):
        base = c * rc
        acc = jnp.zeros((rc, LANES), F32)
        for dr in drs:
            for dc in range(3):
                start = pad + base + (dr - 1) * cols
                acc = acc + s_ref[dc, start:start + rc, :] * w_ref[dr * 3 + dc:dr * 3 + dc + 1, :]
        if mode != "raw":
            acc = _silu(acc)
        if mode in ("q", "k"):
            acc = acc * lax.rsqrt(jnp.sum(acc * acc, axis=-1, keepdims=True) + 1e-6)
        if mode == "q":
            acc = acc * (A_DK ** -0.5)
        o_ref[base:base + rc, :] = acc


def grid_conv(p, conv_w, *, batch, seq, rows, col0, ncol, mode):
    cols = seq // rows
    pad = cols if rows > 1 else SUBLANES
    t0 = col0 // LANES
    return pl.pallas_call(
        functools.partial(_conv_kernel, seq=seq, rows=rows, cols=cols, pad=pad, mode=mode),
        grid=(batch, ncol // LANES),
        in_specs=[pl.BlockSpec((seq, LANES), lambda b, j: (b, t0 + j)),
                  pl.BlockSpec((16, LANES), lambda b, j: (0, t0 + j))],
        out_specs=pl.BlockSpec((seq, LANES), lambda b, j: (b, j)),
        out_shape=jax.ShapeDtypeStruct((batch * seq, ncol), F32),
        scratch_shapes=[pltpu.VMEM((3, seq + 2 * pad, LANES), F32)],
        compiler_params=_cparams(("parallel", "parallel")),
        name="grid_conv_" + mode,
    )(p, conv_w)


def _scan_block(seq, want):
    blk = min(want, seq // CHUNK)
    while (seq // CHUNK) % blk:
        blk -= 1
    return blk * CHUNK


def _chunk_masks(c, rev):
    ii = lax.broadcasted_iota(jnp.int32, (c, c), 0)
    jj = lax.broadcasted_iota(jnp.int32, (c, c), 1)
    eye = ii == jj
    incl = (jj >= ii) if rev else (jj <= ii)
    incl_t = (jj <= ii) if rev else (jj >= ii)
    return eye, incl, incl_t


def _gdn_kernel(q_ref, k_ref, v_ref, g_ref, av_ref, dv_ref, s0_ref, o_ref, s_ref, *, rev, d, heads):
    c = CHUNK
    c2 = 2 * c
    dk = A_DK
    nsub = q_ref.shape[0] // c

    @pl.when(pl.program_id(2) == 0)
    def _():
        s_ref[...] = s0_ref[...]

    eye2, incl2, incl2_t, strict2 = _pair_masks(c, rev)
    eye2_f = eye2.astype(F32)
    gl = g_ref[...]
    log_alpha = -jnp.exp(av_ref[...]) * _softplus(gl + dv_ref[...])
    beta_all = _sigmoid(gl)
    top = lax.broadcasted_iota(jnp.int32, (c2, 1), 0) < c
    steps = _nilpotent_steps(c)
    npair = heads // 2
    rows = [slice(ci * c, (ci + 1) * c) for ci in range(nsub)]
    items = [(ci, p) for ci in range(nsub) for p in range(npair)]
    idx = {it: n for n, it in enumerate(items)}
    ids = range(len(items))
    gate_col = lambda mat, ci, i: mat[rows[ci], i:i + 1]

    bcast = lambda col: jnp.broadcast_to(col, (c2, c2))
    stack = lambda ref, ci, p: jnp.concatenate([ref[rows[ci], 2 * p * dk:(2 * p + 1) * dk],
                                                ref[rows[ci], (2 * p + 1) * dk:(2 * p + 2) * dk]], axis=0)
    la0 = [d * A_HEADS + 2 * p for ci, p in items]
    gts = [(gate_col(log_alpha, items[n][0], la0[n]), gate_col(log_alpha, items[n][0], la0[n] + 1)) for n in ids]
    g = [jnp.concatenate(gts[n], axis=0) for n in ids]
    beta = [jnp.concatenate([gate_col(beta_all, items[n][0], 2 * A_HEADS + la0[n]),
                             gate_col(beta_all, items[n][0], 2 * A_HEADS + la0[n] + 1)], axis=0) for n in ids]
    g_row = [jnp.sum(jnp.where(eye2, bcast(g[n]), 0.0), axis=0, keepdims=True) for n in ids]
    gc_row = [jnp.sum(jnp.where(incl2_t, bcast(g[n]), 0.0), axis=0, keepdims=True) for n in ids]
    gc_col = [jnp.sum(jnp.where(incl2, jnp.broadcast_to(g_row[n], (c2, c2)), 0.0), axis=1, keepdims=True) for n in ids]
    gts = [(jnp.sum(gts[n][0], axis=0, keepdims=True), jnp.sum(gts[n][1], axis=0, keepdims=True)) for n in ids]
    gtot_col = [jnp.where(top, gts[n][0], gts[n][1]) for n in ids]
    decay = [jnp.where(incl2, jnp.exp(jnp.where(incl2, gc_col[n] - gc_row[n], 0.0)), 0.0) for n in ids]
    egc = [jnp.exp(gc_col[n]) for n in ids]
    k = [stack(k_ref, ci, p) for ci, p in items]
    kb = [k[n] * beta[n] for n in ids]
    qe, big = [], []
    for n, (ci, p) in enumerate(items):
        q = stack(q_ref, ci, p)
        big.append(_dot_nt(jnp.concatenate([kb[n], q], axis=0), k[n]))
        qe.append(q * egc[n])
    m_neg = [jnp.where(strict2, -(big[n][0:c2] * decay[n]), 0.0) for n in ids]
    qk = [jnp.where(incl2, big[n][c2:2 * c2] * decay[n], 0.0) for n in ids]
    rhs = [jnp.concatenate([stack(v_ref, ci, p) * beta[n], kb[n] * egc[n]], axis=1) for n, (ci, p) in enumerate(items)]
    ktil = [k[n] * jnp.exp(gtot_col[n] - gc_col[n]) for n in ids]
    xs = _unit_lower_inverse_many(m_neg, eye2_f, steps)
    uw = [_dot(xs[n], rhs[n]) for n in ids]
    s_cur = [s_ref[0, h] for h in range(heads)]
    halves = (slice(0, c), slice(c, c2))
    for ci in (reversed(range(nsub)) if rev else range(nsub)):
        ns = [idx[(ci, p)] for p in range(npair)]
        ws_qs = [[_dot(jnp.concatenate([uw[ns[p]][r, dk:2 * dk], qe[ns[p]][r]], axis=0), s_cur[2 * p + j])
                  for j, r in enumerate(halves)] for p in range(npair)]
        v_new = [uw[ns[p]][:, 0:dk] - jnp.concatenate([ws_qs[p][0][0:c], ws_qs[p][1][0:c]], axis=0)
                 for p in range(npair)]
        for p in range(npair):
            o = jnp.concatenate([ws_qs[p][0][c:c2], ws_qs[p][1][c:c2]], axis=0) + _dot(qk[ns[p]], v_new[p])
            o_ref[rows[ci], 2 * p * dk:(2 * p + 1) * dk] = o[0:c]
            o_ref[rows[ci], (2 * p + 1) * dk:(2 * p + 2) * dk] = o[c:c2]
        s_cur = [s_cur[2 * p + j] * jnp.exp(gts[ns[p]][j]) + _dot_tn(ktil[ns[p]][r], v_new[p][r])
                 for p in range(npair) for j, r in enumerate(halves)]
    for h in range(heads):
        s_ref[0, h] = s_cur[h]


def gdn_scan(q, k, v, gates, a_vec, dt_vec, s0, *, batch, seq, rev, d, gate_blk, sub=4):
    blk = _scan_block(seq, sub)
    nc = seq // blk
    heads, ng = A_HEADS, 1
    w = heads * A_DK

    def row(b, c):
        return b * nc + ((nc - 1 - c) if rev else c)

    tok = pl.BlockSpec((blk, w), lambda b, h, c: (row(b, c), h))
    vec = pl.BlockSpec((1, LANES), lambda b, h, c: (0, 0))
    st = pl.BlockSpec((1, heads, A_DK, A_DK), lambda b, h, c: (b, h, 0, 0))
    return pl.pallas_call(
        functools.partial(_gdn_kernel, rev=rev, d=d, heads=heads),
        grid=(batch, ng, nc),
        in_specs=[tok, tok, tok,
                  pl.BlockSpec((blk, LANES), lambda b, h, c: (row(b, c), gate_blk)),
                  vec, vec, st],
        out_specs=[tok, st],
        out_shape=[jax.ShapeDtypeStruct((batch * seq, A_DIM), F32),
                   jax.ShapeDtypeStruct((batch, A_HEADS, A_DK, A_DK), F32)],
        compiler_params=_cparams(("parallel", "parallel", "arbitrary")),
        name="gdn_scan",
    )(q, k, v, gates, a_vec, dt_vec, s0)


def _group_ones(scale):
    ii = lax.broadcasted_iota(jnp.int32, (LANES, LANES), 0)
    jj = lax.broadcasted_iota(jnp.int32, (LANES, LANES), 1)
    sh = int(math.log2(B_N))
    return jnp.where((ii >> sh) == (jj >> sh), scale, 0.0).astype(F32)


def _rw_feat_kernel(ps_ref, kb_ref, wup_ref, aup_ref, gup_ref, w0_ref, a0_ref, kk_w_ref, ka_w_ref,
                    logw_ref, kk_ref, kka_ref, kt_ref, g_ref):
    ps = ps_ref[...]
    kb = kb_ref[...]
    lw = w0_ref[...] + _dot(jnp.tanh(ps[:, 0:2 * B_LORA]), wup_ref[...])
    logw_ref[...] = -B_DECAY_SCALE * _sigmoid(lw)
    a = _sigmoid(a0_ref[...] + _dot(ps[:, 2 * B_LORA:4 * B_LORA], aup_ref[...]))
    g_ref[...] = _dot(_sigmoid(ps[:, 4 * B_LORA:]), gup_ref[...])
    kkw = kb * kk_w_ref[...]
    ones = _group_ones(1.0)
    for t in range(B_DIM // LANES):
        sl = slice(t * LANES, (t + 1) * LANES)
        x = kkw[:, sl]
        ss = _dot_exact_rhs(x * x, ones)
        kk_ref[:, sl] = x * lax.rsqrt(ss + 1e-6)
    kk = kk_ref[...]
    for dd in range(2):
        a_d = a[:, dd * B_DIM:(dd + 1) * B_DIM]
        kka_ref[:, dd * B_DIM:(dd + 1) * B_DIM] = kk * a_d
        kt_ref[:, dd * B_DIM:(dd + 1) * B_DIM] = kb * (1.0 + (a_d - 1.0) * ka_w_ref[...])


def rw_features(ps, rkv, wup, aup, gup, w0, a0, k_k, k_a, tm=256):
    m = ps.shape[0]
    tm = _row_tile(m, tm)
    full = lambda shp: pl.BlockSpec(shp, lambda i: (0, 0))
    two = jax.ShapeDtypeStruct((m, 2 * B_DIM), F32)
    one = jax.ShapeDtypeStruct((m, B_DIM), F32)
    return pl.pallas_call(
        _rw_feat_kernel,
        grid=(m // tm,),
        in_specs=[pl.BlockSpec((tm, 512), lambda i: (i, 0)),
                  pl.BlockSpec((tm, B_DIM), lambda i: (i, 1)),
                  full((2 * B_LORA, 2 * B_DIM)), full((2 * B_LORA, 2 * B_DIM)), full((256, B_DIM)),
                  full((1, 2 * B_DIM)), full((1, 2 * B_DIM)), full((1, B_DIM)), full((1, B_DIM))],
        out_specs=[pl.BlockSpec((tm, 2 * B_DIM), lambda i: (i, 0)),
                   pl.BlockSpec((tm, B_DIM), lambda i: (i, 0)),
                   pl.BlockSpec((tm, 2 * B_DIM), lambda i: (i, 0)),
                   pl.BlockSpec((tm, 2 * B_DIM), lambda i: (i, 0)),
                   pl.BlockSpec((tm, B_DIM), lambda i: (i, 0))],
        out_shape=[two, one, two, two, one],
        compiler_params=_cparams(("parallel",)),
        name="rw_features",
    )(ps, rkv, wup, aup, gup, w0, a0, k_k, k_a)


def _stack_heads(t, lane_lo):
    return jnp.concatenate([jnp.where(lane_lo, t, 0.0), jnp.where(lane_lo, 0.0, t)], axis=0)


def _side_masks(c, rev):
    ti = lax.broadcasted_iota(jnp.int32, (c, 2 * c), 0)
    tj = lax.broadcasted_iota(jnp.int32, (c, 2 * c), 1) & (c - 1)
    incl = (tj >= ti) if rev else (tj <= ti)
    eye = tj == ti
    return eye, incl, jnp.logical_and(incl, jnp.logical_not(eye))


def _side_inverse_many(n_mats, eye_f, lane_lo, steps):
    c = n_mats[0].shape[0]
    xs = [eye_f + n for n in n_mats]
    if steps == 0:
        return xs
    pws = [_dot(p, _stack_heads(p, lane_lo)) for p in n_mats]
    for _ in range(steps - 1):
        both = [_dot(jnp.concatenate([x, p], axis=0), _stack_heads(p, lane_lo)) for x, p in zip(xs, pws)]
        xs = [x + bth[0:c] for x, bth in zip(xs, both)]
        pws = [bth[c:2 * c] for bth in both]
    return [x + _dot(x, _stack_heads(p, lane_lo)) for x, p in zip(xs, pws)]


def _rwkv_kernel(r_ref, lw_ref, kk_ref, kka_ref, v_ref, kt_ref, s0_ref, y_ref, s_ref, *, rev, pairs):
    c = CHUNK
    nsub = r_ref.shape[0] // c

    @pl.when(pl.program_id(2) == 0)
    def _():
        s_ref[...] = s0_ref[...]

    _, incl, _ = _chunk_masks(c, rev)
    incl_f = incl.astype(F32)
    eye_s, incl_s, strict_s = _side_masks(c, rev)
    eye_sf = eye_s.astype(F32)
    lane_lo = lax.broadcasted_iota(jnp.int32, (1, LANES), 1) < B_N
    ii = lax.broadcasted_iota(jnp.int32, (LANES, LANES), 0)
    jj = lax.broadcasted_iota(jnp.int32, (LANES, LANES), 1)
    sh = int(math.log2(B_N))
    same_head = (ii >> sh) == (jj >> sh)
    steps = _nilpotent_steps(c)
    stk = lambda t: _stack_heads(t, lane_lo)
    rng = range(pairs)
    sls = [slice(p * LANES, (p + 1) * LANES) for p in rng]
    rows = [slice(ci * c, (ci + 1) * c) for ci in range(nsub)]
    items = [(ci, p) for ci in range(nsub) for p in rng]
    idx = {it: n for n, it in enumerate(items)}
    ids = range(len(items))

    lw_all = [lw_ref[rows[ci], :] for ci in range(nsub)]
    g_all = [_dot_exact_lhs(incl_f, lw_all[ci]) for ci in range(nsub)]
    gtot_all = [jnp.sum(lw_all[ci], axis=0, keepdims=True) for ci in range(nsub)]

    a_n, r_n, v_n, big, upd_rhs = [], [], [], [], []
    for ci, p in items:
        sl, rw = sls[p], rows[ci]
        g_in = g_all[ci][:, sl]
        e_neg = jnp.exp(-g_in)
        e_end = jnp.exp(gtot_all[ci][:, sl] - g_in)
        kk, kka, kt = kk_ref[rw, sl], kka_ref[rw, sl], kt_ref[rw, sl]
        a_n.append(kk * jnp.exp(g_in - lw_all[ci][:, sl]))
        r_n.append(r_ref[rw, sl] * jnp.exp(g_in))
        v_n.append(v_ref[rw, sl])
        upd_rhs.append(jnp.concatenate([-kka * e_end, kt * e_end], axis=0))
        bk_s = jnp.concatenate([stk(-kka * e_neg), stk(kt * e_neg)], axis=0)
        big.append(_dot_nt(jnp.concatenate([a_n[-1], r_n[-1]], axis=0), bk_s))
    l_ab = [jnp.where(strict_s, big[n][0:c, 0:LANES], 0.0) for n in ids]
    xs = _side_inverse_many(l_ab, eye_sf, lane_lo, steps)
    v_s = [stk(v_n[n]) for n in ids]
    lv = [_dot(jnp.where(strict_s, big[n][0:c, LANES:2 * LANES], 0.0), v_s[n]) for n in ids]
    wu = [_dot(xs[n], jnp.concatenate([stk(a_n[n]), stk(lv[n])], axis=1)) for n in ids]
    rq = [_dot(jnp.where(incl_s, big[n][c:2 * c, 0:LANES], 0.0),
               jnp.concatenate([stk(wu[n][:, 0:LANES]), stk(wu[n][:, LANES:2 * LANES])], axis=1)) for n in ids]
    y0 = [rq[n][:, LANES:2 * LANES] + _dot(jnp.where(incl_s, big[n][c:2 * c, LANES:2 * LANES], 0.0), v_s[n])
          for n in ids]
    rq_n = [r_n[n] + rq[n][:, 0:LANES] for n in ids]
    s_cur = [s_ref[0, p] for p in rng]
    for ci in (reversed(range(nsub)) if rev else range(nsub)):
        ns = [idx[(ci, p)] for p in rng]
        u = [_dot_nt(wu[ns[p]][:, 0:LANES], s_cur[p]) + wu[ns[p]][:, LANES:2 * LANES] for p in rng]
        for p in rng:
            y_ref[rows[ci], sls[p]] = _dot_nt(rq_n[ns[p]], s_cur[p]) + y0[ns[p]]
        add = [_dot_tn(jnp.concatenate([u[p], v_n[ns[p]]], axis=0), upd_rhs[ns[p]]) for p in rng]
        s_cur = [s_cur[p] * jnp.exp(gtot_all[ci][:, sls[p]]) + jnp.where(same_head, add[p], 0.0) for p in rng]
    for p in rng:
        s_ref[0, p] = s_cur[p]


def rwkv_scan(rkv, logw, kk, kka, kt, s0, *, batch, seq, rev, d, pairs=8, sub=4):
    blk = _scan_block(seq, sub)
    nc = seq // blk
    npair = B_DIM // LANES
    ng = npair // pairs
    w = pairs * LANES

    def row(b, c):
        return b * nc + ((nc - 1 - c) if rev else c)

    def tok(off_blocks):
        return pl.BlockSpec((blk, w), lambda b, h, c: (row(b, c), off_blocks + h))

    per_dir = d * ng
    st = pl.BlockSpec((1, pairs, LANES, LANES), lambda b, h, c: (b, h, 0, 0))
    return pl.pallas_call(
        functools.partial(_rwkv_kernel, rev=rev, pairs=pairs),
        grid=(batch, ng, nc),
        in_specs=[tok(0), tok(per_dir), tok(0), tok(per_dir), tok(2 * ng), tok(per_dir), st],
        out_specs=[tok(0), st],
        out_shape=[jax.ShapeDtypeStruct((batch * seq, B_DIM), F32),
                   jax.ShapeDtypeStruct((batch, npair, LANES, LANES), F32)],
        compiler_params=_cparams(("parallel", "parallel", "arbitrary")),
        name="rwkv_scan",
    )(rkv, logw, kk, kka, rkv, kt, s0)


def _ab_out_kernel(of_ref, ob_ref, yf_ref, yb_ref, z_ref, r_ref, v_ref, kt_ref, g_ref, x_ref, mod_ref,
                   gnw_ref, rk_ref, lnw_ref, lnb_ref, w_ref, o_ref, mix_ref, *, gate_idx):
    for h in range(A_HEADS):
        sl = slice(h * A_DK, (h + 1) * A_DK)
        o = of_ref[:, sl] + ob_ref[:, sl]
        ms = jnp.mean(o * o, axis=-1, keepdims=True)
        ya = o * lax.rsqrt(ms + NORM_EPS) * gnw_ref[...]
        mix_ref[:, sl] = (ya * _silu(z_ref[:, sl])).astype(BF16)
    avg = _group_ones(1.0 / B_N)
    ones = _group_ones(1.0)
    for t in range(B_DIM // LANES):
        sl = slice(t * LANES, (t + 1) * LANES)
        y = yf_ref[:, sl] + yb_ref[:, sl]
        mu = _dot_exact_rhs(y, avg)
        dlt = y - mu
        var = _dot_exact_rhs(dlt * dlt, avg)
        yn = dlt * lax.rsqrt(var + B_GN_EPS)
        kt_sum = kt_ref[:, sl] + kt_ref[:, B_DIM + t * LANES:B_DIM + (t + 1) * LANES]
        bonus = _dot_exact_rhs(r_ref[:, sl] * kt_sum * rk_ref[:, sl], ones) * v_ref[:, sl]
        yb = (yn * lnw_ref[:, sl] + lnb_ref[:, sl] + bonus) * g_ref[:, sl]
        mix_ref[:, A_DIM + t * LANES:A_DIM + (t + 1) * LANES] = yb.astype(BF16)
    acc = jnp.dot(mix_ref[...], w_ref[...], preferred_element_type=F32)
    o_ref[...] = x_ref[...] + mod_ref[0, gate_idx:gate_idx + 1, :] * acc


def ab_out(o_f, o_b, y_f, y_b, p_main, rkv, kt, g_out, x, mod, gnw, rk, lnw, lnb, w_out, *,
           gate_idx, rows_per_mod, tm=256):
    m, dm = x.shape
    tm = _row_tile(min(m, rows_per_mod), tm)
    tpm = rows_per_mod // tm
    zblk = CONV_CH // A_DIM
    row = lambda width, blk=0: pl.BlockSpec((tm, width), lambda i: (i, blk))
    full = lambda shp: pl.BlockSpec(shp, lambda i: (0, 0))
    return pl.pallas_call(
        functools.partial(_ab_out_kernel, gate_idx=gate_idx),
        grid=(m // tm,),
        in_specs=[row(A_DIM), row(A_DIM), row(B_DIM), row(B_DIM), row(A_DIM, zblk),
                  row(B_DIM, 0), row(B_DIM, 2), row(2 * B_DIM), row(B_DIM), row(dm),
                  pl.BlockSpec((1, MOD_ROWS, dm), lambda i: (i // tpm, 0, 0)),
                  full((1, A_DK)), full((1, B_DIM)), full((1, B_DIM)), full((1, B_DIM)),
                  pl.BlockSpec((A_DIM + B_DIM, dm), lambda i: (0, 0), pipeline_mode=pl.Buffered(1))],
        out_specs=row(dm),
        out_shape=jax.ShapeDtypeStruct((m, dm), F32),
        scratch_shapes=[pltpu.VMEM((tm, A_DIM + B_DIM), BF16)],
        compiler_params=_cparams(("parallel",)),
        name="ab_out",
    )(o_f, o_b, y_f, y_b, p_main, rkv, rkv, kt, g_out, x, mod, gnw, rk, lnw, lnb, w_out)


def _mlstm_kernel(q_ref, k_ref, v_ref, g_ref, ib_ref, fb_ref, c0_ref, n0_ref, m0_ref,
                  h_ref, c_ref, n_ref, m_ref, *, rev, d, heads):
    c = CHUNK
    nsub = q_ref.shape[0] // c

    @pl.when(pl.program_id(2) == 0)
    def _():
        c_ref[...] = c0_ref[...]
        n_ref[...] = n0_ref[...]
        m_ref[...] = m0_ref[...]

    eye, incl, incl_t = _chunk_masks(c, rev)
    gl = g_ref[...]
    cap = C_GATE_CAP
    ig_all = cap * jnp.tanh((gl + ib_ref[...]) / cap)
    fz = cap * jnp.tanh((gl + fb_ref[...]) / cap)
    logf_all = jnp.minimum(fz, 0.0) - jnp.log(1.0 + jnp.exp(-jnp.abs(fz)))
    rng = range(heads)
    qs = [slice(h * C_DK, (h + 1) * C_DK) for h in rng]
    vs = [slice(h * C_DV, (h + 1) * C_DV) for h in rng]
    rows = [slice(ci * c, (ci + 1) * c) for ci in range(nsub)]
    order = list(reversed(range(nsub))) if rev else list(range(nsub))
    items = [(ci, h) for ci in range(nsub) for h in rng]
    idx = {it: n for n, it in enumerate(items)}
    ids = range(len(items))
    q = [q_ref[rows[ci], qs[h]] * (C_DK ** -0.5) for ci, h in items]
    qk = [_dot_nt(q[n], k_ref[rows[ci], qs[h]]) for n, (ci, h) in enumerate(items)]
    lane = lax.broadcasted_iota(jnp.int32, (1, LANES), 1)
    gate_col = lambda mat, ci, i: jnp.sum(jnp.where(lane == i, mat[rows[ci]], 0.0), axis=1, keepdims=True)
    bcast = lambda col: jnp.broadcast_to(col, (c, c))
    ig =[gate_col(ig_all, ci, d * C_HEADS + h) for ci, h in items]
    logf = [gate_col(logf_all, ci, 2 * C_HEADS + d * C_HEADS + h) for ci, h in items]
    f_row = [jnp.sum(jnp.where(eye, bcast(logf[n]), 0.0), axis=0, keepdims=True) for n in ids]
    ig_row = [jnp.sum(jnp.where(eye, bcast(ig[n]), 0.0), axis=0, keepdims=True) for n in ids]
    b_row = [jnp.sum(jnp.where(incl_t, bcast(logf[n]), 0.0), axis=0, keepdims=True) for n in ids]
    b_col = [jnp.sum(jnp.where(incl, jnp.broadcast_to(f_row[n], (c, c)), 0.0), axis=1, keepdims=True) for n in ids]
    btot = [jnp.sum(logf[n], axis=0, keepdims=True) for n in ids]
    w_end = [btot[n] - b_col[n] + ig[n] for n in ids]
    w_max = [jnp.max(w_end[n], axis=0, keepdims=True) for n in ids]
    dmat = [b_col[n] - b_row[n] + ig_row[n] for n in ids]
    dmax = [jnp.max(jnp.where(incl, dmat[n], -1e30), axis=1, keepdims=True) for n in ids]
    m_st, m_new = [None] * len(items), [None] * len(items)
    m_cur = [m_ref[0, h, 0:1, 0:1] for h in rng]
    for ci in order:
        for h in rng:
            n = idx[(ci, h)]
            m_st[n] = m_cur[h]
            m_new[n] = jnp.maximum(btot[n] + m_cur[h], w_max[n])
            m_cur[h] = m_new[n]
    scale = [jnp.exp(btot[n] + m_st[n] - m_new[n]) for n in ids]
    kw = [k_ref[rows[ci], qs[h]] * jnp.exp(w_end[n] - m_new[n]) for n, (ci, h) in enumerate(items)]
    kw_sum = [jnp.sum(kw[n], axis=0, keepdims=True) for n in ids]
    m_row = [jnp.maximum(b_col[n] + m_st[n], dmax[n]) for n in ids]
    inter = [jnp.exp(b_col[n] + m_st[n] - m_row[n]) for n in ids]
    s = [jnp.where(incl, qk[n] * jnp.exp(jnp.where(incl, dmat[n] - m_row[n], 0.0)), 0.0) for n in ids]
    s_sum = [jnp.sum(s[n], axis=1, keepdims=True) for n in ids]
    floor = [jnp.exp(-m_row[n]) for n in ids]
    sv = [_dot(s[n], v_ref[rows[ci], vs[h]]) for n, (ci, h) in enumerate(items)]
    kv = [_dot_tn(kw[n], v_ref[rows[ci], vs[h]]) for n, (ci, h) in enumerate(items)]
    c_cur = [c_ref[0, h] for h in rng]
    n_cur = [n_ref[0, h, 0:1, :] for h in rng]
    for ci in order:
        ns = [idx[(ci, h)] for h in rng]
        qc = [_dot(q[ns[h]], c_cur[h]) for h in rng]
        for h in rng:
            n = ns[h]
            den = s_sum[n] + inter[n] * jnp.sum(q[n] * n_cur[h], axis=1, keepdims=True)
            num = sv[n] + inter[n] * qc[h]
            h_ref[rows[ci], vs[h]] = num / jnp.maximum(jnp.abs(den), floor[n])
        c_cur = [c_cur[h] * scale[ns[h]] + kv[ns[h]] for h in rng]
        n_cur = [n_cur[h] * scale[ns[h]] + kw_sum[ns[h]] for h in rng]
    for h in rng:
        c_ref[0, h] = c_cur[h]
        n_ref[0, h] = jnp.broadcast_to(n_cur[h], (SUBLANES, C_DK))
        m_ref[0, h] = jnp.broadcast_to(m_cur[h], (SUBLANES, LANES))


def mlstm_scan(p, gates, ib_vec, fb_vec, c0, n0, m0, *, batch, seq, rev, d, sub=4):
    blk = _scan_block(seq, sub)
    nc = seq // blk
    heads, ng = C_HEADS, 1

    def row(b, c):
        return b * nc + ((nc - 1 - c) if rev else c)

    wq, wv = heads * C_DK, heads * C_DV
    qspec = pl.BlockSpec((blk, wq), lambda b, h, c: (row(b, c), h))
    kspec = pl.BlockSpec((blk, wq), lambda b, h, c: (row(b, c), ng + h))
    vspec = pl.BlockSpec((blk, wv), lambda b, h, c: (row(b, c), (2 * C_QK) // wv + h))
    hspec = pl.BlockSpec((blk, wv), lambda b, h, c: (row(b, c), h))
    vec = pl.BlockSpec((1, LANES), lambda b, h, c: (0, 0))
    cst = pl.BlockSpec((1, heads, C_DK, C_DV), lambda b, h, c: (b, h, 0, 0))
    nst = pl.BlockSpec((1, heads, SUBLANES, C_DK), lambda b, h, c: (b, h, 0, 0))
    mst = pl.BlockSpec((1, heads, SUBLANES, LANES), lambda b, h, c: (b, h, 0, 0))
    return pl.pallas_call(
        functools.partial(_mlstm_kernel, rev=rev, d=d, heads=heads),
        grid=(batch, ng, nc),
        in_specs=[qspec, kspec, vspec,
                  pl.BlockSpec((blk, LANES), lambda b, h, c: (row(b, c), 0)),
                  vec, vec, cst, nst, mst],
        out_specs=[hspec, cst, nst, mst],
        out_shape=[jax.ShapeDtypeStruct((batch * seq, C_V), F32),
                   jax.ShapeDtypeStruct((batch, C_HEADS, C_DK, C_DV), F32),
                   jax.ShapeDtypeStruct((batch, C_HEADS, SUBLANES, C_DK), F32),
                   jax.ShapeDtypeStruct((batch, C_HEADS, SUBLANES, LANES), F32)],
        compiler_params=_cparams(("parallel", "parallel", "arbitrary")),
        name="mlstm_scan",
    )(p, p, p, gates, ib_vec, fb_vec, c0, n0, m0)


def _ml_out_kernel(hf_ref, hb_ref, o_ref_in, x_ref, mod_ref, nw_ref, w_ref, out_ref, mix_ref, *, gate_idx):
    for h in range(C_HEADS):
        sl = slice(h * C_DV, (h + 1) * C_DV)
        hh = hf_ref[:, sl] + hb_ref[:, sl]
        ms = jnp.mean(hh * hh, axis=-1, keepdims=True)
        y = hh * lax.rsqrt(ms + NORM_EPS) * nw_ref[:, sl]
        mix_ref[:, sl] = (y * _sigmoid(o_ref_in[:, sl])).astype(BF16)
    acc = jnp.dot(mix_ref[...], w_ref[...], preferred_element_type=F32)
    out_ref[...] = x_ref[...] + mod_ref[0, gate_idx:gate_idx + 1, :] * acc


def ml_out(h_f, h_b, p, x, mod, nw, w_out, *, gate_idx, rows_per_mod, tm=256):
    m, dm = x.shape
    tm = _row_tile(min(m, rows_per_mod), tm)
    tpm = rows_per_mod // tm
    oblk = (2 * C_QK + C_V) // C_V
    row = lambda width, blk=0: pl.BlockSpec((tm, width), lambda i: (i, blk))
    full = lambda shp: pl.BlockSpec(shp, lambda i: (0, 0))
    return pl.pallas_call(
        functools.partial(_ml_out_kernel, gate_idx=gate_idx),
        grid=(m // tm,),
        in_specs=[row(C_V), row(C_V), row(C_V, oblk), row(dm),
                  pl.BlockSpec((1, MOD_ROWS, dm), lambda i: (i // tpm, 0, 0)),
                  full((1, C_V)), pl.BlockSpec((C_V, dm), lambda i: (0, 0), pipeline_mode=pl.Buffered(1))],
        out_specs=row(dm),
        out_shape=jax.ShapeDtypeStruct((m, dm), F32),
        scratch_shapes=[pltpu.VMEM((tm, C_V), BF16)],
        compiler_params=_cparams(("parallel",)),
        name="ml_out",
    )(h_f, h_b, p, x, mod, nw, w_out)


def _lane_vec(values, offset):
    flat = values.reshape(-1).astype(F32)
    return jnp.zeros((1, LANES), F32).at[0, offset:offset + flat.shape[0]].set(flat)


def _ffn(x, mod, nw, w1, w3, w2, layer, rows_per_mod):
    g = ffn_up(x, mod, nw, w1, w3, layer, shift_idx=3, scale_idx=4, rows_per_mod=rows_per_mod)
    return mm_res(g, w2, x, mod, layer, gate_idx=5, rows_per_mod=rows_per_mod)


def _ab_layer(streams, prm, ctx_out):
    w_in = prm["w_in"]
    w_all = w_in.astype(BF16)
    small = w_in[:, CONV_CH + A_DIM + 4 * A_HEADS:]
    gates_w = w_in[:, CONV_CH + A_DIM:CONV_CH + A_DIM + 4 * A_HEADS]
    d_model = w_in.shape[0]
    w_small = jnp.concatenate(
        [small, jnp.zeros((d_model, 512 - small.shape[1]), F32),
         gates_w, jnp.zeros((d_model, LANES - gates_w.shape[1]), F32)], axis=1).astype(BF16)
    conv_w = jnp.concatenate([prm["conv_w"].reshape(9, CONV_CH), jnp.zeros((7, CONV_CH), F32)], axis=0)
    a_vec = _lane_vec(prm["a_log"], 0)
    dt_vec = _lane_vec(prm["dt_bias"], 0)
    zero_up = jnp.zeros((B_LORA, B_DIM), F32)
    blockdiag = lambda u: jnp.concatenate(
        [jnp.concatenate([u[0], zero_up], axis=1), jnp.concatenate([zero_up, u[1]], axis=1)], axis=0).astype(BF16)
    wup, aup = blockdiag(prm["w_up"]), blockdiag(prm["a_up"])
    gup = jnp.concatenate([prm["g_up"], jnp.zeros((256 - B_G_LORA, B_DIM), F32)], axis=0).astype(BF16)
    w0 = prm["w0"].reshape(1, 2 * B_DIM)
    a0 = prm["a0"].reshape(1, 2 * B_DIM)
    k_k = prm["k_k"].reshape(1, B_DIM)
    k_a = prm["k_a"].reshape(1, B_DIM)
    gnw = prm["gdn_norm_w"].reshape(1, A_DK)
    rk = prm["r_k"].reshape(1, B_DIM)
    lnw = prm["ln_w"].reshape(1, B_DIM)
    lnb = prm["ln_b"].reshape(1, B_DIM)
    w_out = prm["w_out"].astype(BF16)

    feats = []
    for st in streams:
        kw = dict(shift_idx=0, scale_idx=1, rows_per_mod=st["rpm"])
        p_main, p_small = modmm(st["x"], st["mod"], prm["norm_w"], w_all, w_small, n_main=CONV_CH + A_DIM, **kw)
        ckw = dict(batch=st["batch"], seq=st["seq"], rows=st["rows"])
        q = grid_conv(p_main, conv_w, col0=0, ncol=A_DIM, mode="q", **ckw)
        k = grid_conv(p_main, conv_w, col0=A_DIM, ncol=A_DIM, mode="k", **ckw)
        v = grid_conv(p_main, conv_w, col0=2 * A_DIM, ncol=A_DIM, mode="v", **ckw)
        rkv = grid_conv(p_main, conv_w, col0=3 * A_DIM, ncol=3 * B_DIM, mode="raw", **ckw)
        logw, kk, kka, kt, g_out = rw_features(p_small, rkv, wup, aup, gup, w0, a0, k_k, k_a)
        feats.append(dict(p_main=p_main, p_small=p_small, q=q, k=k, v=v, rkv=rkv,
                          logw=logw, kk=kk, kka=kka, kt=kt, g_out=g_out))

    nb = streams[-1]["batch"]
    outs = [dict() for _ in streams]
    for d in range(2):
        rev = d == 1
        s_a = jnp.zeros((nb, A_HEADS, A_DK, A_DK), F32)
        s_b = jnp.zeros((nb, B_DIM // LANES, LANES, LANES), F32)
        for si, (st, f) in enumerate(zip(streams, feats)):
            skw = dict(batch=st["batch"], seq=st["seq"], rev=rev, d=d)
            o, s_a = gdn_scan(f["q"], f["k"], f["v"], f["p_small"], a_vec, dt_vec, s_a, gate_blk=4, **skw)
            y, s_b = rwkv_scan(f["rkv"], f["logw"], f["kk"], f["kka"], f["kt"], s_b, **skw)
            outs[si]["o%d" % d] = o
            outs[si]["y%d" % d] = y

    new_x = []
    for si, (st, f) in enumerate(zip(streams, feats)):
        if si == 0 and not ctx_out:
            new_x.append(None)
            continue
        o = outs[si]
        new_x.append(ab_out(o["o0"], o["o1"], o["y0"], o["y1"], f["p_main"], f["rkv"], f["kt"], f["g_out"],
                            st["x"], st["mod"], gnw, rk, lnw, lnb, w_out, gate_idx=2, rows_per_mod=st["rpm"]))
    return new_x


def _ml_layer(streams, prm, ctx_out):
    w_in = prm["w_in"]
    main_cols = 2 * C_QK + 2 * C_V
    w_all = w_in.astype(BF16)
    d_model = w_in.shape[0]
    w_g = jnp.concatenate([w_in[:, main_cols:], jnp.zeros((d_model, LANES - 4 * C_HEADS), F32)], axis=1).astype(BF16)
    ib_vec = _lane_vec(prm["i_bias"], 0)
    fb_vec = _lane_vec(prm["f_bias"], 2 * C_HEADS)
    nw = prm["ml_norm_w"].reshape(1, C_V)
    w_out = prm["w_out"].astype(BF16)

    feats = []
    for st in streams:
        kw = dict(shift_idx=0, scale_idx=1, rows_per_mod=st["rpm"])
        p_main, p_g = modmm(st["x"], st["mod"], prm["norm_w"], w_all, w_g, n_main=main_cols, **kw)
        feats.append(dict(p_main=p_main, p_g=p_g))

    nb = streams[-1]["batch"]
    outs = [dict() for _ in streams]
    for d in range(2):
        rev = d == 1
        c_st = jnp.zeros((nb, C_HEADS, C_DK, C_DV), F32)
        n_st = jnp.zeros((nb, C_HEADS, SUBLANES, C_DK), F32)
        m_st = jnp.zeros((nb, C_HEADS, SUBLANES, LANES), F32)
        for si, (st, f) in enumerate(zip(streams, feats)):
            h, c_st, n_st, m_st = mlstm_scan(f["p_main"], f["p_g"], ib_vec, fb_vec, c_st, n_st, m_st,
                                             batch=st["batch"], seq=st["seq"], rev=rev, d=d)
            outs[si]["h%d" % d] = h

    new_x = []
    for si, (st, f) in enumerate(zip(streams, feats)):
        if si == 0 and not ctx_out:
            new_x.append(None)
            continue
        new_x.append(ml_out(outs[si]["h0"], outs[si]["h1"], f["p_main"], st["x"], st["mod"], nw, w_out,
                            gate_idx=2, rows_per_mod=st["rpm"]))
    return new_x


def kernel(x, c, ctx, c_ctx, ada_w, ada_b, norm_w, ab_w_in, ab_conv_w, gdn_a_log, gdn_dt_bias, gdn_norm_w, rw_w0, rw_w_up, rw_a0, rw_a_up, rw_g_up, rw_k_k, rw_k_a, rw_r_k, rw_ln_w, rw_ln_b, ab_w_out, ml_w_in, ml_i_bias, ml_f_bias, ml_norm_w, ml_w_out, ffn_w1, ffn_w3, ffn_w2, final_norm_w):
    bsz, seq, dm = x.shape
    ctx_len = ctx.shape[1]
    depth = ada_w.shape[0]
    xl = x.reshape(bsz * seq, dm)
    xc = ctx.reshape(bsz * ctx_len, dm)
    cond = jnp.zeros((MOD_ROWS, dm), F32).at[:bsz].set(c).at[bsz].set(c_ctx)

    w2 = ffn_w2.astype(BF16)
    for i in range(depth):
        ctx_out = i < depth - 1
        j = i // 2
        mod_all = adaln(cond, ada_w, ada_b[i].reshape(1, -1), i).reshape(MOD_ROWS, 6, dm)
        mod_all = jnp.pad(mod_all, ((0, 0), (0, MOD_ROWS - 6), (0, 0)))
        streams = [
            dict(x=xc, mod=mod_all[bsz:bsz + 1], batch=bsz, seq=ctx_len, rows=1, rpm=bsz * ctx_len),
            dict(x=xl, mod=mod_all[:bsz], batch=bsz, seq=seq, rows=seq // GRID_W, rpm=seq),
        ]
        if i % 2 == 0:
            prm = dict(w_in=ab_w_in[j], conv_w=ab_conv_w[j], a_log=gdn_a_log[j], dt_bias=gdn_dt_bias[j],
                       gdn_norm_w=gdn_norm_w[j], w0=rw_w0[j], w_up=rw_w_up[j], a0=rw_a0[j], a_up=rw_a_up[j],
                       g_up=rw_g_up[j], k_k=rw_k_k[j], k_a=rw_k_a[j], r_k=rw_r_k[j], ln_w=rw_ln_w[j],
                       ln_b=rw_ln_b[j], w_out=ab_w_out[j], norm_w=norm_w[i, 0].reshape(1, dm))
            xc_new, xl = _ab_layer(streams, prm, ctx_out)
        else:
            prm = dict(w_in=ml_w_in[j], i_bias=ml_i_bias[j], f_bias=ml_f_bias[j], ml_norm_w=ml_norm_w[j],
                       w_out=ml_w_out[j], norm_w=norm_w[i, 0].reshape(1, dm))
            xc_new, xl = _ml_layer(streams, prm, ctx_out)
        nw2 =norm_w[i, 1].reshape(1, dm)
        xl = _ffn(xl, streams[1]["mod"], nw2, ffn_w1, ffn_w3, w2, i, streams[1]["rpm"])
        if ctx_out:
            xc = _ffn(xc_new, streams[0]["mod"], nw2, ffn_w1, ffn_w3, w2, i, streams[0]["rpm"])
    return final_norm(xl, final_norm_w.reshape(1, dm)).reshape(bsz, seq, dm)
```

```python
import functools
import math

import jax
import jax.numpy as jnp
from jax import lax
from jax.experimental import pallas as pl
from jax.experimental.pallas import tpu as pltpu

F32 = jnp.float32
BF16 = jnp.bfloat16

NORM_EPS = 1e-6
GRID_W = 64
LANES = 128
SUBLANES = 8
VMEM_LIMIT = 56 * 1024 * 1024

A_HEADS, A_DK = 8, 128
A_DIM = A_HEADS * A_DK
B_HEADS, B_N = 16, 64
B_DIM = B_HEADS * B_N
B_LORA = 64
B_G_LORA = 160
B_GN_EPS = 64e-5
B_DECAY_SCALE = math.exp(-0.5)
C_HEADS, C_DK, C_DV = 8, 128, 256
C_QK = C_HEADS * C_DK
C_V = C_HEADS * C_DV
C_GATE_CAP = 15.0
CONV_CH = 3 * A_DIM + 3 * B_DIM
CHUNK = 64
MOD_ROWS = 8


def _cparams(sem):
    return pltpu.CompilerParams(dimension_semantics=sem, vmem_limit_bytes=VMEM_LIMIT)


def _dot(a, b):
    return jnp.dot(a.astype(BF16), b.astype(BF16), preferred_element_type=F32)


def _dot_nt(a, b):
    return lax.dot_general(a.astype(BF16), b.astype(BF16), (((1,), (1,)), ((), ())),
                           preferred_element_type=F32)


def _dot_tn(a, b):
    return lax.dot_general(a.astype(BF16), b.astype(BF16), (((0,), (0,)), ((), ())),
                           preferred_element_type=F32)


def _split2(a):
    hi = a.astype(BF16)
    lo = (a - hi.astype(F32)).astype(BF16)
    return hi, lo


def _split3(a):
    hi = a.astype(BF16)
    r = a - hi.astype(F32)
    mid = r.astype(BF16)
    lo = (r - mid.astype(F32)).astype(BF16)
    return hi, mid, lo


def _dot_exact_rhs(a, b_exact):
    hi, lo = _split2(a)
    b = b_exact.astype(BF16)
    d = lambda t: jnp.dot(t, b, preferred_element_type=F32)
    return d(hi) + d(lo)


def _dot_exact_lhs(a_exact, b):
    hi, mid, lo = _split3(b)
    a = a_exact.astype(BF16)
    d = lambda t: jnp.dot(a, t, preferred_element_type=F32)
    return d(hi) + d(mid) + d(lo)


def _dot3(a, b):
    ah, al = _split2(a)
    bh, bl = _split2(b)
    d = lambda s, t: jnp.dot(s, t, preferred_element_type=F32)
    return d(ah, bh) + d(al, bh) + d(ah, bl)


def _sigmoid(t):
    return 1.0 / (1.0 + jnp.exp(-t))


def _silu(t):
    return t * _sigmoid(t)


def _softplus(t):
    return jnp.maximum(t, 0.0) + jnp.log(1.0 + jnp.exp(-jnp.abs(t)))


def _nilpotent_steps(c):
    return max(int(math.ceil(math.log2(c))) - 1, 0)


def _adaln_kernel(c_ref, w_ref, b_ref, o_ref):
    o_ref[...] = _dot3(_silu(c_ref[...]), w_ref[...]) + b_ref[...]


def adaln(cond, w, b, layer):
    m, d = cond.shape
    e = w.shape[2]
    tn = _col_tile(e, 512)
    return pl.pallas_call(
        _adaln_kernel,
        grid=(e // tn,),
        in_specs=[pl.BlockSpec((m, d), lambda j: (0, 0)),
                  pl.BlockSpec((None, d, tn), lambda j: (layer, 0, j)),
                  pl.BlockSpec((1, tn), lambda j: (0, j))],
        out_specs=pl.BlockSpec((m, tn), lambda j: (0, j)),
        out_shape=jax.ShapeDtypeStruct((m, e), F32),
        compiler_params=_cparams(("parallel",)),
        name="adaln",
    )(cond, w, b)


def _modulated(x_ref, mod_ref, nw_ref, shift_idx, scale_idx):
    x = x_ref[...]
    ms = jnp.mean(x * x, axis=-1, keepdims=True)
    gain = nw_ref[...] * (1.0 + mod_ref[0, scale_idx:scale_idx + 1, :])
    return x * lax.rsqrt(ms + NORM_EPS) * gain + mod_ref[0, shift_idx:shift_idx + 1, :]


def _modmm_kernel(x_ref, mod_ref, nw_ref, w_ref, ws_ref, o_ref, os_ref, h_ref, *, shift_idx, scale_idx, nmain):
    j = pl.program_id(1)

    @pl.when(j == 0)
    def _():
        h_ref[...] = _modulated(x_ref, mod_ref, nw_ref, shift_idx, scale_idx).astype(BF16)

    @pl.when(j < nmain)
    def _():
        o_ref[...] = jnp.dot(h_ref[...], w_ref[...], preferred_element_type=F32)

    @pl.when(j == nmain)
    def _():
        os_ref[...] = jnp.dot(h_ref[...], ws_ref[...], preferred_element_type=F32)


def _row_tile(m, want):
    t = min(want, m)
    while m % t:
        t //= 2
    return t


def _col_tile(n, want):
    t = min(want, n)
    while n % t or t % LANES:
        t -= LANES
    return t


def modmm(x, mod, nw, w, w_small, *, n_main, shift_idx, scale_idx, rows_per_mod, tm=1024, tn=1024):
    m, d = x.shape
    ns = w_small.shape[1]
    tm = _row_tile(min(m, rows_per_mod), tm)
    tn = _col_tile(n_main, tn)
    tpm = rows_per_mod // tm
    nmain = n_main // tn
    main_col = lambda j: jnp.minimum(j, nmain - 1)
    return pl.pallas_call(
        functools.partial(_modmm_kernel, shift_idx=shift_idx, scale_idx=scale_idx, nmain=nmain),
        grid=(m // tm, nmain + 1),
        in_specs=[pl.BlockSpec((tm, d), lambda i, j: (i, 0)),
                  pl.BlockSpec((1, MOD_ROWS, d), lambda i, j: (i // tpm, 0, 0)),
                  pl.BlockSpec((1, d), lambda i, j: (0, 0)),
                  pl.BlockSpec((d, tn), lambda i, j: (0, main_col(j))),
                  pl.BlockSpec((d, ns), lambda i, j: (0, 0))],
        out_specs=[pl.BlockSpec((tm, tn), lambda i, j: (i, main_col(j))),
                   pl.BlockSpec((tm, ns), lambda i, j: (i, 0))],
        out_shape=[jax.ShapeDtypeStruct((m, n_main), F32), jax.ShapeDtypeStruct((m, ns), F32)],
        scratch_shapes=[pltpu.VMEM((tm, d), BF16)],
        compiler_params=_cparams(("parallel", "arbitrary")),
        name="modmm",
    )(x, mod, nw, w, w_small)


def _ffn_up_kernel(x_ref, mod_ref, nw_ref, w1_ref, w3_ref, o_ref, h_ref, *, shift_idx, scale_idx):
    @pl.when(pl.program_id(1) == 0)
    def _():
        h_ref[...] = _modulated(x_ref, mod_ref, nw_ref, shift_idx, scale_idx).astype(BF16)

    h = h_ref[...]
    a = jnp.dot(h, w1_ref[...].astype(BF16), preferred_element_type=F32)
    b = jnp.dot(h, w3_ref[...].astype(BF16), preferred_element_type=F32)
    o_ref[...] = (_silu(a) * b).astype(BF16)


def ffn_up(x, mod, nw, w1, w3, layer, *, shift_idx, scale_idx, rows_per_mod, tm=1024, tn=512):
    m, d = x.shape
    n = w1.shape[2]
    tm = _row_tile(min(m, rows_per_mod), tm)
    tn = _col_tile(n, tn)
    tpm = rows_per_mod // tm
    return pl.pallas_call(
        functools.partial(_ffn_up_kernel, shift_idx=shift_idx, scale_idx=scale_idx),
        grid=(m // tm, n // tn),
        in_specs=[pl.BlockSpec((tm, d), lambda i, j: (i, 0)),
                  pl.BlockSpec((1, MOD_ROWS, d), lambda i, j: (i // tpm, 0, 0)),
                  pl.BlockSpec((1, d), lambda i, j: (0, 0)),
                  pl.BlockSpec((None, d, tn), lambda i, j: (layer, 0, j)),
                  pl.BlockSpec((None, d, tn), lambda i, j: (layer, 0, j))],
        out_specs=pl.BlockSpec((tm, tn), lambda i, j: (i, j)),
        out_shape=jax.ShapeDtypeStruct((m, n), BF16),
        scratch_shapes=[pltpu.VMEM((tm, d), BF16)],
        compiler_params=_cparams(("parallel", "arbitrary")),
        name="ffn_up",
    )(x, mod, nw, w1, w3)


def _mm_res_kernel(a_ref, w_ref, res_ref, mod_ref, o_ref, *, gate_idx):
    acc = jnp.dot(a_ref[...], w_ref[...], preferred_element_type=F32)
    o_ref[...] = res_ref[...] + mod_ref[0, gate_idx:gate_idx + 1, :] * acc


def mm_res(a, w, res, mod, layer, *, gate_idx, rows_per_mod, tm=1024, tn=512):
    m, k = a.shape
    n = w.shape[2]
    tm = _row_tile(min(m, rows_per_mod), tm)
    tn = _col_tile(n, tn)
    tpm = rows_per_mod // tm
    return pl.pallas_call(
        functools.partial(_mm_res_kernel, gate_idx=gate_idx),
        grid=(m // tm, n // tn),
        in_specs=[pl.BlockSpec((tm, k), lambda i, j: (i, 0)),
                  pl.BlockSpec((None, k, tn), lambda i, j: (layer, 0, j)),
                  pl.BlockSpec((tm, tn), lambda i, j: (i, j)),
                  pl.BlockSpec((1, MOD_ROWS, tn), lambda i, j: (i // tpm, 0, j))],
        out_specs=pl.BlockSpec((tm, tn), lambda i, j: (i, j)),
        out_shape=jax.ShapeDtypeStruct((m, n), F32),
        compiler_params=_cparams(("parallel", "arbitrary")),
        name="mm_res",
    )(a, w, res, mod)


def _final_norm_kernel(x_ref, w_ref, o_ref):
    x = x_ref[...]
    ms = jnp.mean(x * x, axis=-1, keepdims=True)
    o_ref[...] = x * lax.rsqrt(ms + NORM_EPS) * w_ref[...]


def final_norm(x, w, tm=512):
    m, d = x.shape
    tm = _row_tile(m, tm)
    return pl.pallas_call(
        _final_norm_kernel,
        grid=(m // tm,),
        in_specs=[pl.BlockSpec((tm, d), lambda i: (i, 0)), pl.BlockSpec((1, d), lambda i: (0, 0))],
        out_specs=pl.BlockSpec((tm, d), lambda i: (i, 0)),
        out_shape=jax.ShapeDtypeStruct((m, d), F32),
        compiler_params=_cparams(("parallel",)),
        name="final_norm",
    )(x, w)


CONV_ROWS = 256


def _conv_kernel(p_ref, w_ref, o_ref, s_ref, *, seq, rows, cols, pad, mode):
    zeros = jnp.zeros((pad, LANES), F32)
    for t in range(3):
        s_ref[t, 0:pad, :] = zeros
        s_ref[t, pad + seq:pad + seq + pad, :] = zeros
    s_ref[1, pad:pad + seq, :] = p_ref[...]
    rc = min(CONV_ROWS, seq)
    for c in range(seq // rc):
        base = c * rc
        col = (lax.broadcasted_iota(jnp.int32, (rc, 1), 0) + base) & (cols - 1)
        left = s_ref[1, pad - 1 + base:pad - 1 + base + rc, :]
        right = s_ref[1, pad + 1 + base:pad + 1 + base + rc, :]
        s_ref[0, pad + base:pad + base + rc, :] = jnp.where(col == 0, 0.0, left)
        s_ref[2, pad + base:pad + base + rc, :] = jnp.where(col == cols - 1, 0.0, right)
    drs = (0, 1, 2) if rows > 1 else (1,)
    for c in range(seq // rc):
        base = c * rc
        acc = jnp.zeros((rc, LANES), F32)
        for dr in drs:
            for dc in range(3):
                start = pad + base + (dr - 1) * cols
                acc = acc + s_ref[dc, start:start + rc, :] * w_ref[dr * 3 + dc:dr * 3 + dc + 1, :]
        if mode != "raw":
            acc = _silu(acc)
        if mode in ("q", "k"):
            acc = acc * lax.rsqrt(jnp.sum(acc * acc, axis=-1, keepdims=True) + 1e-6)
        if mode == "q":
            acc = acc * (A_DK ** -0.5)
        o_ref[base:base + rc, :] = acc


def grid_conv(p, conv_w, *, batch, seq, rows, col0, ncol, mode):
    cols = seq // rows
    pad = cols if rows > 1 else SUBLANES
    t0 = col0 // LANES
    return pl.pallas_call(
        functools.partial(_conv_kernel, seq=seq, rows=rows, cols=cols, pad=pad, mode=mode),
        grid=(batch, ncol // LANES),
        in_specs=[pl.BlockSpec((seq, LANES), lambda b, j: (b, t0 + j)),
                  pl.BlockSpec((16, LANES), lambda b, j: (0, t0 + j))],
        out_specs=pl.BlockSpec((seq, LANES), lambda b, j: (b, j)),
        out_shape=jax.ShapeDtypeStruct((batch * seq, ncol), F32),
        scratch_shapes=[pltpu.VMEM((3, seq + 2 * pad, LANES), F32)],
        compiler_params=_cparams(("parallel", "parallel")),
        name="grid_conv_" + mode,
    )(p, conv_w)


def _scan_block(seq, want):
    blk = min(want, seq // CHUNK)
    while (seq // CHUNK) % blk:
        blk -= 1
    return blk * CHUNK


def _chunk_masks(c, rev):
    ii = lax.broadcasted_iota(jnp.int32, (c, c), 0)
    jj = lax.broadcasted_iota(jnp.int32, (c, c), 1)
    eye = ii == jj
    incl = (jj >= ii) if rev else (jj <= ii)
    incl_t = (jj <= ii) if rev else (jj >= ii)
    return eye, incl, incl_t


def _gdn_kernel(q_ref, k_ref, v_ref, g_ref, av_ref, dv_ref, s0_ref, o_ref, s_ref, *, rev, d, heads):
    c = CHUNK
    c2 = 2 * c
    dk = A_DK
    nsub = q_ref.shape[0] // c

    @pl.when(pl.program_id(2) == 0)
    def _():
        s_ref[...] = s0_ref[...]

    eye_s, incl_s, strict_s = _side_masks(c, rev)
    ti = lax.broadcasted_iota(jnp.int32, (c, c2), 0)
    tj = lax.broadcasted_iota(jnp.int32, (c, c2), 1) & (c - 1)
    incl_t_s = (ti >= tj) if rev else (ti <= tj)
    eye_sf = eye_s.astype(F32)
    lane_lo = lax.broadcasted_iota(jnp.int32, (1, c2), 1) < c
    lo_s = jnp.logical_and(incl_s, lane_lo)
    hi_s = jnp.logical_and(incl_s, jnp.logical_not(lane_lo))
    first = lax.broadcasted_iota(jnp.int32, (1, 2 * dk), 1) < dk
    gl = g_ref[...]
    log_alpha = -jnp.exp(av_ref[...]) * _softplus(gl + dv_ref[...])
    beta_all = _sigmoid(gl)
    steps = _nilpotent_steps(c)
    npair = heads // 2
    rows = [slice(ci * c, (ci + 1) * c) for ci in range(nsub)]
    items = [(ci, p) for ci in range(nsub) for p in range(npair)]
    idx = {it: n for n, it in enumerate(items)}
    ids = range(len(items))
    gate_col = lambda mat, ci, i: mat[rows[ci], i:i + 1]
    side = lambda x0, x1: jnp.where(lane_lo, x0, x1)
    wide = lambda x0, x1: jnp.concatenate([jnp.broadcast_to(x0, (c, dk)), jnp.broadcast_to(x1, (c, dk))], axis=1)
    pair = lambda ref, ci, p: ref[rows[ci], 2 * p * dk:(2 * p + 2) * dk]
    blockdiag = lambda t: jnp.concatenate([jnp.where(first, t, 0.0), jnp.where(first, 0.0, t)], axis=0)
    zeros = jnp.zeros((c, 2 * dk), F32)

    la0 = [d * A_HEADS + 2 * p for ci, p in items]
    g0 = [gate_col(log_alpha, items[n][0], la0[n]) for n in ids]
    g1 = [gate_col(log_alpha, items[n][0], la0[n] + 1) for n in ids]
    beta = [wide(gate_col(beta_all, items[n][0], 2 * A_HEADS + la0[n]),
                 gate_col(beta_all, items[n][0], 2 * A_HEADS + la0[n] + 1)) for n in ids]
    g_side = [side(g0[n], g1[n]) for n in ids]
    g_row = [jnp.sum(jnp.where(eye_s, g_side[n], 0.0), axis=0, keepdims=True) for n in ids]
    gc_row = [jnp.sum(jnp.where(incl_t_s, g_side[n], 0.0), axis=0, keepdims=True) for n in ids]
    g_row_b = [jnp.broadcast_to(g_row[n], (c, c2)) for n in ids]
    gc0 = [jnp.sum(jnp.where(lo_s, g_row_b[n], 0.0), axis=1, keepdims=True) for n in ids]
    gc1 = [jnp.sum(jnp.where(hi_s, g_row_b[n], 0.0), axis=1, keepdims=True) for n in ids]
    gts = [(jnp.sum(g0[n], axis=0, keepdims=True), jnp.sum(g1[n], axis=0, keepdims=True)) for n in ids]
    decay = [jnp.where(incl_s, jnp.exp(jnp.where(incl_s, side(gc0[n], gc1[n]) - gc_row[n], 0.0)), 0.0) for n in ids]
    egc = [wide(jnp.exp(gc0[n]), jnp.exp(gc1[n])) for n in ids]
    k = [pair(k_ref, ci, p) for ci, p in items]
    kb = [k[n] * beta[n] for n in ids]
    qe, big = [], []
    for n, (ci, p) in enumerate(items):
        q = pair(q_ref, ci, p)
        big.append(_dot_nt(jnp.concatenate([kb[n], q], axis=0), blockdiag(k[n])))
        qe.append(q * egc[n])
    m_neg = [jnp.where(strict_s, -(big[n][0:c] * decay[n]), 0.0) for n in ids]
    qk = [jnp.where(incl_s, big[n][c:c2] * decay[n], 0.0) for n in ids]
    ktil = [k[n] * wide(jnp.exp(gts[n][0] - gc0[n]), jnp.exp(gts[n][1] - gc1[n])) for n in ids]
    xs = _side_inverse_many(m_neg, eye_sf, lane_lo, steps)
    uw = []
    for n, (ci, p) in enumerate(items):
        vb = pair(v_ref, ci, p) * beta[n]
        ke = kb[n] * egc[n]
        r0 = jnp.concatenate([vb[:, 0:dk], ke[:, 0:dk], zeros], axis=1)
        r1 = jnp.concatenate([zeros, vb[:, dk:2 * dk], ke[:, dk:2 * dk]], axis=1)
        uw.append(_dot(xs[n], jnp.concatenate([r0, r1], axis=0)))
    s_cur = [s_ref[0, h] for h in range(heads)]
    for ci in (reversed(range(nsub)) if rev else range(nsub)):
        ns = [idx[(ci, p)] for p in range(npair)]
        ws_qs = [[_dot(jnp.concatenate([uw[ns[p]][:, (2 * j + 1) * dk:(2 * j + 2) * dk],
                                        qe[ns[p]][:, j * dk:(j + 1) * dk]], axis=0), s_cur[2 * p + j])
                  for j in range(2)] for p in range(npair)]
        v_new = [[uw[ns[p]][:, 2 * j * dk:(2 * j + 1) * dk] - ws_qs[p][j][0:c] for j in range(2)] for p in range(npair)]
        for p in range(npair):
            inner = _dot(qk[ns[p]], blockdiag(jnp.concatenate(v_new[p], axis=1)))
            o_ref[rows[ci], 2 * p * dk:(2 * p + 2) * dk] = (
                jnp.concatenate([ws_qs[p][0][c:c2], ws_qs[p][1][c:c2]], axis=1) + inner)
        s_cur = [s_cur[2 * p + j] * jnp.exp(gts[ns[p]][j])
                 + _dot_tn(ktil[ns[p]][:, j * dk:(j + 1) * dk], v_new[p][j])
                 for p in range(npair) for j in range(2)]
    for h in range(heads):
        s_ref[0, h] = s_cur[h]


def gdn_scan(q, k, v, gates, a_vec, dt_vec, s0, *, batch, seq, rev, d, gate_blk, sub=4):
    blk = _scan_block(seq, sub)
    nc = seq // blk
    heads, ng = A_HEADS, 1
    w = heads * A_DK

    def row(b, c):
        return b * nc + ((nc - 1 - c) if rev else c)

    tok = pl.BlockSpec((blk, w), lambda b, h, c: (row(b, c), h))
    vec = pl.BlockSpec((1, LANES), lambda b, h, c: (0, 0))
    st = pl.BlockSpec((1, heads, A_DK, A_DK), lambda b, h, c: (b, h, 0, 0))
    return pl.pallas_call(
        functools.partial(_gdn_kernel, rev=rev, d=d, heads=heads),
        grid=(batch, ng, nc),
        in_specs=[tok, tok, tok,
                  pl.BlockSpec((blk, LANES), lambda b, h, c: (row(b, c), gate_blk)),
                  vec, vec, st],
        out_specs=[tok, st],
        out_shape=[jax.ShapeDtypeStruct((batch * seq, A_DIM), F32),
                   jax.ShapeDtypeStruct((batch, A_HEADS, A_DK, A_DK), F32)],
        compiler_params=_cparams(("parallel", "parallel", "arbitrary")),
        name="gdn_scan",
    )(q, k, v, gates, a_vec, dt_vec, s0)


def _group_ones(scale):
    ii = lax.broadcasted_iota(jnp.int32, (LANES, LANES), 0)
    jj = lax.broadcasted_iota(jnp.int32, (LANES, LANES), 1)
    sh = int(math.log2(B_N))
    return jnp.where((ii >> sh) == (jj >> sh), scale, 0.0).astype(F32)


def _rw_feat_kernel(ps_ref, kb_ref, wup_ref, aup_ref, gup_ref, w0_ref, a0_ref, kk_w_ref, ka_w_ref,
                    logw_ref, kk_ref, kka_ref, kt_ref, g_ref):
    ps = ps_ref[...]
    kb = kb_ref[...]
    lw = w0_ref[...] + _dot(jnp.tanh(ps[:, 0:2 * B_LORA]), wup_ref[...])
    logw_ref[...] = -B_DECAY_SCALE * _sigmoid(lw)
    a = _sigmoid(a0_ref[...] + _dot(ps[:, 2 * B_LORA:4 * B_LORA], aup_ref[...]))
    g_ref[...] = _dot(_sigmoid(ps[:, 4 * B_LORA:]), gup_ref[...])
    kkw = kb * kk_w_ref[...]
    ones = _group_ones(1.0)
    for t in range(B_DIM // LANES):
        sl = slice(t * LANES, (t + 1) * LANES)
        x = kkw[:, sl]
        ss = _dot_exact_rhs(x * x, ones)
        kk_ref[:, sl] = x * lax.rsqrt(ss + 1e-6)
    kk = kk_ref[...]
    for dd in range(2):
        a_d = a[:, dd * B_DIM:(dd + 1) * B_DIM]
        kka_ref[:, dd * B_DIM:(dd + 1) * B_DIM] = kk * a_d
        kt_ref[:, dd * B_DIM:(dd + 1) * B_DIM] = kb * (1.0 + (a_d - 1.0) * ka_w_ref[...])


def rw_features(ps, rkv, wup, aup, gup, w0, a0, k_k, k_a, tm=256):
    m = ps.shape[0]
    tm = _row_tile(m, tm)
    full = lambda shp: pl.BlockSpec(shp, lambda i: (0, 0))
    two = jax.ShapeDtypeStruct((m, 2 * B_DIM), F32)
    one = jax.ShapeDtypeStruct((m, B_DIM), F32)
    return pl.pallas_call(
        _rw_feat_kernel,
        grid=(m // tm,),
        in_specs=[pl.BlockSpec((tm, 512), lambda i: (i, 0)),
                  pl.BlockSpec((tm, B_DIM), lambda i: (i, 1)),
                  full((2 * B_LORA, 2 * B_DIM)), full((2 * B_LORA, 2 * B_DIM)), full((256, B_DIM)),
                  full((1, 2 * B_DIM)), full((1, 2 * B_DIM)), full((1, B_DIM)), full((1, B_DIM))],
        out_specs=[pl.BlockSpec((tm, 2 * B_DIM), lambda i: (i, 0)),
                   pl.BlockSpec((tm, B_DIM), lambda i: (i, 0)),
                   pl.BlockSpec((tm, 2 * B_DIM), lambda i: (i, 0)),
                   pl.BlockSpec((tm, 2 * B_DIM), lambda i: (i, 0)),
                   pl.BlockSpec((tm, B_DIM), lambda i: (i, 0))],
        out_shape=[two, one, two, two, one],
        compiler_params=_cparams(("parallel",)),
        name="rw_features",
    )(ps, rkv, wup, aup, gup, w0, a0, k_k, k_a)


def _stack_heads(t, lane_lo):
    return jnp.concatenate([jnp.where(lane_lo, t, 0.0), jnp.where(lane_lo, 0.0, t)], axis=0)


def _side_masks(c, rev):
    ti = lax.broadcasted_iota(jnp.int32, (c, 2 * c), 0)
    tj = lax.broadcasted_iota(jnp.int32, (c, 2 * c), 1) & (c - 1)
    incl = (tj >= ti) if rev else (tj <= ti)
    eye = tj == ti
    return eye, incl, jnp.logical_and(incl, jnp.logical_not(eye))


def _side_inverse_many(n_mats, eye_f, lane_lo, steps):
    c = n_mats[0].shape[0]
    xs = [eye_f + n for n in n_mats]
    if steps == 0:
        return xs
    pws = [_dot(p, _stack_heads(p, lane_lo)) for p in n_mats]
    for _ in range(steps - 1):
        both = [_dot(jnp.concatenate([x, p], axis=0), _stack_heads(p, lane_lo)) for x, p in zip(xs, pws)]
        xs = [x + bth[0:c] for x, bth in zip(xs, both)]
        pws = [bth[c:2 * c] for bth in both]
    return [x + _dot(x, _stack_heads(p, lane_lo)) for x, p in zip(xs, pws)]


def _rwkv_kernel(r_ref, lw_ref, kk_ref, kka_ref, v_ref, kt_ref, s0_ref, y_ref, s_ref, *, rev, pairs):
    c = CHUNK
    nsub = r_ref.shape[0] // c

    @pl.when(pl.program_id(2) == 0)
    def _():
        s_ref[...] = s0_ref[...]

    _, incl, _ = _chunk_masks(c, rev)
    incl_f = incl.astype(F32)
    eye_s, incl_s, strict_s = _side_masks(c, rev)
    eye_sf = eye_s.astype(F32)
    lane_lo = lax.broadcasted_iota(jnp.int32, (1, LANES), 1) < B_N
    ii = lax.broadcasted_iota(jnp.int32, (LANES, LANES), 0)
    jj = lax.broadcasted_iota(jnp.int32, (LANES, LANES), 1)
    sh = int(math.log2(B_N))
    same_head = (ii >> sh) == (jj >> sh)
    steps = _nilpotent_steps(c)
    stk = lambda t: _stack_heads(t, lane_lo)
    rng = range(pairs)
    sls = [slice(p * LANES, (p + 1) * LANES) for p in rng]
    rows = [slice(ci * c, (ci + 1) * c) for ci in range(nsub)]
    items = [(ci, p) for ci in range(nsub) for p in rng]
    idx = {it: n for n, it in enumerate(items)}
    ids = range(len(items))

    lw_all = [lw_ref[rows[ci], :] for ci in range(nsub)]
    g_all = [_dot_exact_lhs(incl_f, lw_all[ci]) for ci in range(nsub)]
    gtot_all = [jnp.sum(lw_all[ci], axis=0, keepdims=True) for ci in range(nsub)]

    a_n, r_n, v_n, big, upd_rhs = [], [], [], [], []
    for ci, p in items:
        sl, rw = sls[p], rows[ci]
        g_in = g_all[ci][:, sl]
        e_neg = jnp.exp(-g_in)
        e_end = jnp.exp(gtot_all[ci][:, sl] - g_in)
        kk, kka, kt = kk_ref[rw, sl], kka_ref[rw, sl], kt_ref[rw, sl]
        a_n.append(kk * jnp.exp(g_in - lw_all[ci][:, sl]))
        r_n.append(r_ref[rw, sl] * jnp.exp(g_in))
        v_n.append(v_ref[rw, sl])
        upd_rhs.append(jnp.concatenate([-kka * e_end, kt * e_end], axis=0))
        bk_s = jnp.concatenate([stk(-kka * e_neg), stk(kt * e_neg)], axis=0)
        big.append(_dot_nt(jnp.concatenate([a_n[-1], r_n[-1]], axis=0), bk_s))
    l_ab = [jnp.where(strict_s, big[n][0:c, 0:LANES], 0.0) for n in ids]
    xs = _side_inverse_many(l_ab, eye_sf, lane_lo, steps)
    v_s = [stk(v_n[n]) for n in ids]
    lv = [_dot(jnp.where(strict_s, big[n][0:c, LANES:2 * LANES], 0.0), v_s[n]) for n in ids]
    wu = [_dot(xs[n], jnp.concatenate([stk(a_n[n]), stk(lv[n])], axis=1)) for n in ids]
    rq = [_dot(jnp.where(incl_s, big[n][c:2 * c, 0:LANES], 0.0),
               jnp.concatenate([stk(wu[n][:, 0:LANES]), stk(wu[n][:, LANES:2 * LANES])], axis=1)) for n in ids]
    y0 = [rq[n][:, LANES:2 * LANES] + _dot(jnp.where(incl_s, big[n][c:2 * c, LANES:2 * LANES], 0.0), v_s[n])
          for n in ids]
    rq_n = [r_n[n] + rq[n][:, 0:LANES] for n in ids]
    s_cur = [s_ref[0, p] for p in rng]
    for ci in (reversed(range(nsub)) if rev else range(nsub)):
        ns = [idx[(ci, p)] for p in rng]
        u = [_dot_nt(wu[ns[p]][:, 0:LANES], s_cur[p]) + wu[ns[p]][:, LANES:2 * LANES] for p in rng]
        for p in rng:
            y_ref[rows[ci], sls[p]] = _dot_nt(rq_n[ns[p]], s_cur[p]) + y0[ns[p]]
        add = [_dot_tn(jnp.concatenate([u[p], v_n[ns[p]]], axis=0), upd_rhs[ns[p]]) for p in rng]
        s_cur = [s_cur[p] * jnp.exp(gtot_all[ci][:, sls[p]]) + jnp.where(same_head, add[p], 0.0) for p in rng]
    for p in rng:
        s_ref[0, p] = s_cur[p]


def rwkv_scan(rkv, logw, kk, kka, kt, s0, *, batch, seq, rev, d, pairs=8, sub=4):
    blk = _scan_block(seq, sub)
    nc = seq // blk
    npair = B_DIM // LANES
    ng = npair // pairs
    w = pairs * LANES

    def row(b, c):
        return b * nc + ((nc - 1 - c) if rev else c)

    def tok(off_blocks):
        return pl.BlockSpec((blk, w), lambda b, h, c: (row(b, c), off_blocks + h))

    per_dir = d * ng
    st = pl.BlockSpec((1, pairs, LANES, LANES), lambda b, h, c: (b, h, 0, 0))
    return pl.pallas_call(
        functools.partial(_rwkv_kernel, rev=rev, pairs=pairs),
        grid=(batch, ng, nc),
        in_specs=[tok(0), tok(per_dir), tok(0), tok(per_dir), tok(2 * ng), tok(per_dir), st],
        out_specs=[tok(0), st],
        out_shape=[jax.ShapeDtypeStruct((batch * seq, B_DIM), F32),
                   jax.ShapeDtypeStruct((batch, npair, LANES, LANES), F32)],
        compiler_params=_cparams(("parallel", "parallel", "arbitrary")),
        name="rwkv_scan",
    )(rkv, logw, kk, kka, rkv, kt, s0)


def _ab_out_kernel(of_ref, ob_ref, yf_ref, yb_ref, z_ref, r_ref, v_ref, kt_ref, g_ref, x_ref, mod_ref,
                   gnw_ref, rk_ref, lnw_ref, lnb_ref, w_ref, o_ref, mix_ref, *, gate_idx):
    for h in range(A_HEADS):
        sl = slice(h * A_DK, (h + 1) * A_DK)
        o = of_ref[:, sl] + ob_ref[:, sl]
        ms = jnp.mean(o * o, axis=-1, keepdims=True)
        ya = o * lax.rsqrt(ms + NORM_EPS) * gnw_ref[...]
        mix_ref[:, sl] = (ya * _silu(z_ref[:, sl])).astype(BF16)
    avg = _group_ones(1.0 / B_N)
    ones = _group_ones(1.0)
    for t in range(B_DIM // LANES):
        sl = slice(t * LANES, (t + 1) * LANES)
        y = yf_ref[:, sl] + yb_ref[:, sl]
        mu = _dot_exact_rhs(y, avg)
        dlt = y - mu
        var = _dot_exact_rhs(dlt * dlt, avg)
        yn = dlt * lax.rsqrt(var + B_GN_EPS)
        kt_sum = kt_ref[:, sl] + kt_ref[:, B_DIM + t * LANES:B_DIM + (t + 1) * LANES]
        bonus = _dot_exact_rhs(r_ref[:, sl] * kt_sum * rk_ref[:, sl], ones) * v_ref[:, sl]
        yb = (yn * lnw_ref[:, sl] + lnb_ref[:, sl] + bonus) * g_ref[:, sl]
        mix_ref[:, A_DIM + t * LANES:A_DIM + (t + 1) * LANES] = yb.astype(BF16)
    acc = jnp.dot(mix_ref[...], w_ref[...], preferred_element_type=F32)
    o_ref[...] = x_ref[...] + mod_ref[0, gate_idx:gate_idx + 1, :] * acc


def ab_out(o_f, o_b, y_f, y_b, p_main, rkv, kt, g_out, x, mod, gnw, rk, lnw, lnb, w_out, *,
           gate_idx, rows_per_mod, tm=256):
    m, dm = x.shape
    tm = _row_tile(min(m, rows_per_mod), tm)
    tpm = rows_per_mod // tm
    zblk = CONV_CH // A_DIM
    row = lambda width, blk=0: pl.BlockSpec((tm, width), lambda i: (i, blk))
    full = lambda shp: pl.BlockSpec(shp, lambda i: (0, 0))
    return pl.pallas_call(
        functools.partial(_ab_out_kernel, gate_idx=gate_idx),
        grid=(m // tm,),
        in_specs=[row(A_DIM), row(A_DIM), row(B_DIM), row(B_DIM), row(A_DIM, zblk),
                  row(B_DIM, 0), row(B_DIM, 2), row(2 * B_DIM), row(B_DIM), row(dm),
                  pl.BlockSpec((1, MOD_ROWS, dm), lambda i: (i // tpm, 0, 0)),
                  full((1, A_DK)), full((1, B_DIM)), full((1, B_DIM)), full((1, B_DIM)),
                  pl.BlockSpec((A_DIM + B_DIM, dm), lambda i: (0, 0), pipeline_mode=pl.Buffered(1))],
        out_specs=row(dm),
        out_shape=jax.ShapeDtypeStruct((m, dm), F32),
        scratch_shapes=[pltpu.VMEM((tm, A_DIM + B_DIM), BF16)],
        compiler_params=_cparams(("parallel",)),
        name="ab_out",
    )(o_f, o_b, y_f, y_b, p_main, rkv, rkv, kt, g_out, x, mod, gnw, rk, lnw, lnb, w_out)


def _mlstm_kernel(q_ref, k_ref, v_ref, g_ref, ib_ref, fb_ref, c0_ref, n0_ref, m0_ref,
                  h_ref, c_ref, n_ref, m_ref, *, rev, d, heads):
    c = CHUNK
    nsub = q_ref.shape[0] // c

    @pl.when(pl.program_id(2) == 0)
    def _():
        c_ref[...] = c0_ref[...]
        n_ref[...] = n0_ref[...]
        m_ref[...] = m0_ref[...]

    eye, incl, incl_t = _chunk_masks(c, rev)
    gl = g_ref[...]
    cap = C_GATE_CAP
    ig_all = cap * jnp.tanh((gl + ib_ref[...]) / cap)
    fz = cap * jnp.tanh((gl + fb_ref[...]) / cap)
    logf_all = jnp.minimum(fz, 0.0) - jnp.log(1.0 + jnp.exp(-jnp.abs(fz)))
    rng = range(heads)
    qs = [slice(h * C_DK, (h + 1) * C_DK) for h in rng]
    vs = [slice(h * C_DV, (h + 1) * C_DV) for h in rng]
    rows = [slice(ci * c, (ci + 1) * c) for ci in range(nsub)]
    order = list(reversed(range(nsub))) if rev else list(range(nsub))
    items = [(ci, h) for ci in range(nsub) for h in rng]
    idx = {it: n for n, it in enumerate(items)}
    ids = range(len(items))
    q = [q_ref[rows[ci], qs[h]] * (C_DK ** -0.5) for ci, h in items]
    qk = [_dot_nt(q[n], k_ref[rows[ci], qs[h]]) for n, (ci, h) in enumerate(items)]
    lane = lax.broadcasted_iota(jnp.int32, (1, LANES), 1)
    gate_col = lambda mat, ci, i: jnp.sum(jnp.where(lane == i, mat[rows[ci]], 0.0), axis=1, keepdims=True)
    bcast = lambda col: jnp.broadcast_to(col, (c, c))
    ig =[gate_col(ig_all, ci, d * C_HEADS + h) for ci, h in items]
    logf = [gate_col(logf_all, ci, 2 * C_HEADS + d * C_HEADS + h) for ci, h in items]
    f_row = [jnp.sum(jnp.where(eye, bcast(logf[n]), 0.0), axis=0, keepdims=True) for n in ids]
    ig_row = [jnp.sum(jnp.where(eye, bcast(ig[n]), 0.0), axis=0, keepdims=True) for n in ids]
    b_row = [jnp.sum(jnp.where(incl_t, bcast(logf[n]), 0.0), axis=0, keepdims=True) for n in ids]
    b_col = [jnp.sum(jnp.where(incl, jnp.broadcast_to(f_row[n], (c, c)), 0.0), axis=1, keepdims=True) for n in ids]
    btot = [jnp.sum(logf[n], axis=0, keepdims=True) for n in ids]
    w_end = [btot[n] - b_col[n] + ig[n] for n in ids]
    w_max = [jnp.max(w_end[n], axis=0, keepdims=True) for n in ids]
    dmat = [b_col[n] - b_row[n] + ig_row[n] for n in ids]
    dmax = [jnp.max(jnp.where(incl, dmat[n], -1e30), axis=1, keepdims=True) for n in ids]
    m_st, m_new = [None] * len(items), [None] * len(items)
    m_cur = [m_ref[0, h, 0:1, 0:1] for h in rng]
    for ci in order:
        for h in rng:
            n = idx[(ci, h)]
            m_st[n] = m_cur[h]
            m_new[n] = jnp.maximum(btot[n] + m_cur[h], w_max[n])
            m_cur[h] = m_new[n]
    scale = [jnp.exp(btot[n] + m_st[n] - m_new[n]) for n in ids]
    kw = [k_ref[rows[ci], qs[h]] * jnp.exp(w_end[n] - m_new[n]) for n, (ci, h) in enumerate(items)]
    kw_sum = [jnp.sum(kw[n], axis=0, keepdims=True) for n in ids]
    m_row = [jnp.maximum(b_col[n] + m_st[n], dmax[n]) for n in ids]
    inter = [jnp.exp(b_col[n] + m_st[n] - m_row[n]) for n in ids]
    s = [jnp.where(incl, qk[n] * jnp.exp(jnp.where(incl, dmat[n] - m_row[n], 0.0)), 0.0) for n in ids]
    s_sum = [jnp.sum(s[n], axis=1, keepdims=True) for n in ids]
    floor = [jnp.exp(-m_row[n]) for n in ids]
    sv = [_dot(s[n], v_ref[rows[ci], vs[h]]) for n, (ci, h) in enumerate(items)]
    kv = [_dot_tn(kw[n], v_ref[rows[ci], vs[h]]) for n, (ci, h) in enumerate(items)]
    c_cur = [c_ref[0, h] for h in rng]
    n_cur = [n_ref[0, h, 0:1, :] for h in rng]
    for ci in order:
        ns = [idx[(ci, h)] for h in rng]
        qc = [_dot(q[ns[h]], c_cur[h]) for h in rng]
        for h in rng:
            n = ns[h]
            den = s_sum[n] + inter[n] * jnp.sum(q[n] * n_cur[h], axis=1, keepdims=True)
            num = sv[n] + inter[n] * qc[h]
            h_ref[rows[ci], vs[h]] = num / jnp.maximum(jnp.abs(den), floor[n])
        c_cur = [c_cur[h] * scale[ns[h]] + kv[ns[h]] for h in rng]
        n_cur = [n_cur[h] * scale[ns[h]] + kw_sum[ns[h]] for h in rng]
    for h in rng:
        c_ref[0, h] = c_cur[h]
        n_ref[0, h] = jnp.broadcast_to(n_cur[h], (SUBLANES, C_DK))
        m_ref[0, h] = jnp.broadcast_to(m_cur[h], (SUBLANES, LANES))


def mlstm_scan(p, gates, ib_vec, fb_vec, c0, n0, m0, *, batch, seq, rev, d, sub=4):
    blk = _scan_block(seq, sub)
    nc = seq // blk
    heads, ng = C_HEADS, 1

    def row(b, c):
        return b * nc + ((nc - 1 - c) if rev else c)

    wq, wv = heads * C_DK, heads * C_DV
    qspec = pl.BlockSpec((blk, wq), lambda b, h, c: (row(b, c), h))
    kspec = pl.BlockSpec((blk, wq), lambda b, h, c: (row(b, c), ng + h))
    vspec = pl.BlockSpec((blk, wv), lambda b, h, c: (row(b, c), (2 * C_QK) // wv + h))
    hspec = pl.BlockSpec((blk, wv), lambda b, h, c: (row(b, c), h))
    vec = pl.BlockSpec((1, LANES), lambda b, h, c: (0, 0))
    cst = pl.BlockSpec((1, heads, C_DK, C_DV), lambda b, h, c: (b, h, 0, 0))
    nst = pl.BlockSpec((1, heads, SUBLANES, C_DK), lambda b, h, c: (b, h, 0, 0))
    mst = pl.BlockSpec((1, heads, SUBLANES, LANES), lambda b, h, c: (b, h, 0, 0))
    return pl.pallas_call(
        functools.partial(_mlstm_kernel, rev=rev, d=d, heads=heads),
        grid=(batch, ng, nc),
        in_specs=[qspec, kspec, vspec,
                  pl.BlockSpec((blk, LANES), lambda b, h, c: (row(b, c), 0)),
                  vec, vec, cst, nst, mst],
        out_specs=[hspec, cst, nst, mst],
        out_shape=[jax.ShapeDtypeStruct((batch * seq, C_V), F32),
                   jax.ShapeDtypeStruct((batch, C_HEADS, C_DK, C_DV), F32),
                   jax.ShapeDtypeStruct((batch, C_HEADS, SUBLANES, C_DK), F32),
                   jax.ShapeDtypeStruct((batch, C_HEADS, SUBLANES, LANES), F32)],
        compiler_params=_cparams(("parallel", "parallel", "arbitrary")),
        name="mlstm_scan",
    )(p, p, p, gates, ib_vec, fb_vec, c0, n0, m0)


def _ml_out_kernel(hf_ref, hb_ref, o_ref_in, x_ref, mod_ref, nw_ref, w_ref, out_ref, mix_ref, *, gate_idx):
    for h in range(C_HEADS):
        sl = slice(h * C_DV, (h + 1) * C_DV)
        hh = hf_ref[:, sl] + hb_ref[:, sl]
        ms = jnp.mean(hh * hh, axis=-1, keepdims=True)
        y = hh * lax.rsqrt(ms + NORM_EPS) * nw_ref[:, sl]
        mix_ref[:, sl] = (y * _sigmoid(o_ref_in[:, sl])).astype(BF16)
    acc = jnp.dot(mix_ref[...], w_ref[...], preferred_element_type=F32)
    out_ref[...] = x_ref[...] + mod_ref[0, gate_idx:gate_idx + 1, :] * acc


def ml_out(h_f, h_b, p, x, mod, nw, w_out, *, gate_idx, rows_per_mod, tm=256):
    m, dm = x.shape
    tm = _row_tile(min(m, rows_per_mod), tm)
    tpm = rows_per_mod // tm
    oblk = (2 * C_QK + C_V) // C_V
    row = lambda width, blk=0: pl.BlockSpec((tm, width), lambda i: (i, blk))
    full = lambda shp: pl.BlockSpec(shp, lambda i: (0, 0))
    return pl.pallas_call(
        functools.partial(_ml_out_kernel, gate_idx=gate_idx),
        grid=(m // tm,),
        in_specs=[row(C_V), row(C_V), row(C_V, oblk), row(dm),
                  pl.BlockSpec((1, MOD_ROWS, dm), lambda i: (i // tpm, 0, 0)),
                  full((1, C_V)), pl.BlockSpec((C_V, dm), lambda i: (0, 0), pipeline_mode=pl.Buffered(1))],
        out_specs=row(dm),
        out_shape=jax.ShapeDtypeStruct((m, dm), F32),
        scratch_shapes=[pltpu.VMEM((tm, C_V), BF16)],
        compiler_params=_cparams(("parallel",)),
        name="ml_out",
    )(h_f, h_b, p, x, mod, nw, w_out)


def _lane_vec(values, offset):
    flat = values.reshape(-1).astype(F32)
    return jnp.zeros((1, LANES), F32).at[0, offset:offset + flat.shape[0]].set(flat)


def _ffn(x, mod, nw, w1, w3, w2, layer, rows_per_mod):
    g = ffn_up(x, mod, nw, w1, w3, layer, shift_idx=3, scale_idx=4, rows_per_mod=rows_per_mod)
    return mm_res(g, w2, x, mod, layer, gate_idx=5, rows_per_mod=rows_per_mod)


def _ab_layer(streams, prm, ctx_out):
    w_in = prm["w_in"]
    w_all = w_in.astype(BF16)
    small = w_in[:, CONV_CH + A_DIM + 4 * A_HEADS:]
    gates_w = w_in[:, CONV_CH + A_DIM:CONV_CH + A_DIM + 4 * A_HEADS]
    d_model = w_in.shape[0]
    w_small = jnp.concatenate(
        [small, jnp.zeros((d_model, 512 - small.shape[1]), F32),
         gates_w, jnp.zeros((d_model, LANES - gates_w.shape[1]), F32)], axis=1).astype(BF16)
    conv_w = jnp.concatenate([prm["conv_w"].reshape(9, CONV_CH), jnp.zeros((7, CONV_CH), F32)], axis=0)
    a_vec = _lane_vec(prm["a_log"], 0)
    dt_vec = _lane_vec(prm["dt_bias"], 0)
    zero_up = jnp.zeros((B_LORA, B_DIM), F32)
    blockdiag = lambda u: jnp.concatenate(
        [jnp.concatenate([u[0], zero_up], axis=1), jnp.concatenate([zero_up, u[1]], axis=1)], axis=0).astype(BF16)
    wup, aup = blockdiag(prm["w_up"]), blockdiag(prm["a_up"])
    gup = jnp.concatenate([prm["g_up"], jnp.zeros((256 - B_G_LORA, B_DIM), F32)], axis=0).astype(BF16)
    w0 = prm["w0"].reshape(1, 2 * B_DIM)
    a0 = prm["a0"].reshape(1, 2 * B_DIM)
    k_k = prm["k_k"].reshape(1, B_DIM)
    k_a = prm["k_a"].reshape(1, B_DIM)
    gnw = prm["gdn_norm_w"].reshape(1, A_DK)
    rk = prm["r_k"].reshape(1, B_DIM)
    lnw = prm["ln_w"].reshape(1, B_DIM)
    lnb = prm["ln_b"].reshape(1, B_DIM)
    w_out = prm["w_out"].astype(BF16)

    feats = []
    for st in streams:
        kw = dict(shift_idx=0, scale_idx=1, rows_per_mod=st["rpm"])
        p_main, p_small = modmm(st["x"], st["mod"], prm["norm_w"], w_all, w_small, n_main=CONV_CH + A_DIM, **kw)
        ckw = dict(batch=st["batch"], seq=st["seq"], rows=st["rows"])
        q = grid_conv(p_main, conv_w, col0=0, ncol=A_DIM, mode="q", **ckw)
        k = grid_conv(p_main, conv_w, col0=A_DIM, ncol=A_DIM, mode="k", **ckw)
        v = grid_conv(p_main, conv_w, col0=2 * A_DIM, ncol=A_DIM, mode="v", **ckw)
        rkv = grid_conv(p_main, conv_w, col0=3 * A_DIM, ncol=3 * B_DIM, mode="raw", **ckw)
        logw, kk, kka, kt, g_out = rw_features(p_small, rkv, wup, aup, gup, w0, a0, k_k, k_a)
        feats.append(dict(p_main=p_main, p_small=p_small, q=q, k=k, v=v, rkv=rkv,
                          logw=logw, kk=kk, kka=kka, kt=kt, g_out=g_out))

    nb = streams[-1]["batch"]
    outs = [dict() for _ in streams]
    for d in range(2):
        rev = d == 1
        s_a = jnp.zeros((nb, A_HEADS, A_DK, A_DK), F32)
        s_b = jnp.zeros((nb, B_DIM // LANES, LANES, LANES), F32)
        for si, (st, f) in enumerate(zip(streams, feats)):
            skw = dict(batch=st["batch"], seq=st["seq"], rev=rev, d=d)
            o, s_a = gdn_scan(f["q"], f["k"], f["v"], f["p_small"], a_vec, dt_vec, s_a, gate_blk=4, **skw)
            y, s_b = rwkv_scan(f["rkv"], f["logw"], f["kk"], f["kka"], f["kt"], s_b, **skw)
            outs[si]["o%d" % d] = o
            outs[si]["y%d" % d] = y

    new_x = []
    for si, (st, f) in enumerate(zip(streams, feats)):
        if si == 0 and not ctx_out:
            new_x.append(None)
            continue
        o = outs[si]
        new_x.append(ab_out(o["o0"], o["o1"], o["y0"], o["y1"], f["p_main"], f["rkv"], f["kt"], f["g_out"],
                            st["x"], st["mod"], gnw, rk, lnw, lnb, w_out, gate_idx=2, rows_per_mod=st["rpm"]))
    return new_x


def _ml_layer(streams, prm, ctx_out):
    w_in = prm["w_in"]
    main_cols = 2 * C_QK + 2 * C_V
    w_all = w_in.astype(BF16)
    d_model = w_in.shape[0]
    w_g = jnp.concatenate([w_in[:, main_cols:], jnp.zeros((d_model, LANES - 4 * C_HEADS), F32)], axis=1).astype(BF16)
    ib_vec = _lane_vec(prm["i_bias"], 0)
    fb_vec = _lane_vec(prm["f_bias"], 2 * C_HEADS)
    nw = prm["ml_norm_w"].reshape(1, C_V)
    w_out = prm["w_out"].astype(BF16)

    feats = []
    for st in streams:
        kw = dict(shift_idx=0, scale_idx=1, rows_per_mod=st["rpm"])
        p_main, p_g = modmm(st["x"], st["mod"], prm["norm_w"], w_all, w_g, n_main=main_cols, **kw)
        feats.append(dict(p_main=p_main, p_g=p_g))

    nb = streams[-1]["batch"]
    outs = [dict() for _ in streams]
    for d in range(2):
        rev = d == 1
        c_st = jnp.zeros((nb, C_HEADS, C_DK, C_DV), F32)
        n_st = jnp.zeros((nb, C_HEADS, SUBLANES, C_DK), F32)
        m_st = jnp.zeros((nb, C_HEADS, SUBLANES, LANES), F32)
        for si, (st, f) in enumerate(zip(streams, feats)):
            h, c_st, n_st, m_st = mlstm_scan(f["p_main"], f["p_g"], ib_vec, fb_vec, c_st, n_st, m_st,
                                             batch=st["batch"], seq=st["seq"], rev=rev, d=d)
            outs[si]["h%d" % d] = h

    new_x = []
    for si, (st, f) in enumerate(zip(streams, feats)):
        if si == 0 and not ctx_out:
            new_x.append(None)
            continue
        new_x.append(ml_out(outs[si]["h0"], outs[si]["h1"], f["p_main"], st["x"], st["mod"], nw, w_out,
                            gate_idx=2, rows_per_mod=st["rpm"]))
    return new_x


def kernel(x, c, ctx, c_ctx, ada_w, ada_b, norm_w, ab_w_in, ab_conv_w, gdn_a_log, gdn_dt_bias, gdn_norm_w, rw_w0, rw_w_up, rw_a0, rw_a_up, rw_g_up, rw_k_k, rw_k_a, rw_r_k, rw_ln_w, rw_ln_b, ab_w_out, ml_w_in, ml_i_bias, ml_f_bias, ml_norm_w, ml_w_out, ffn_w1, ffn_w3, ffn_w2, final_norm_w):
    bsz, seq, dm = x.shape
    ctx_len = ctx.shape[1]
    depth = ada_w.shape[0]
    xl = x.reshape(bsz * seq, dm)
    xc = ctx.reshape(bsz * ctx_len, dm)
    cond = jnp.zeros((MOD_ROWS, dm), F32).at[:bsz].set(c).at[bsz].set(c_ctx)

    w2 = ffn_w2.astype(BF16)
    for i in range(depth):
        ctx_out = i < depth - 1
        j = i // 2
        mod_all = adaln(cond, ada_w, ada_b[i].reshape(1, -1), i).reshape(MOD_ROWS, 6, dm)
        mod_all = jnp.pad(mod_all, ((0, 0), (0, MOD_ROWS - 6), (0, 0)))
        streams = [
            dict(x=xc, mod=mod_all[bsz:bsz + 1], batch=bsz, seq=ctx_len, rows=1, rpm=bsz * ctx_len),
            dict(x=xl, mod=mod_all[:bsz], batch=bsz, seq=seq, rows=seq // GRID_W, rpm=seq),
        ]
        if i % 2 == 0:
            prm = dict(w_in=ab_w_in[j], conv_w=ab_conv_w[j], a_log=gdn_a_log[j], dt_bias=gdn_dt_bias[j],
                       gdn_norm_w=gdn_norm_w[j], w0=rw_w0[j], w_up=rw_w_up[j], a0=rw_a0[j], a_up=rw_a_up[j],
                       g_up=rw_g_up[j], k_k=rw_k_k[j], k_a=rw_k_a[j], r_k=rw_r_k[j], ln_w=rw_ln_w[j],
                       ln_b=rw_ln_b[j], w_out=ab_w_out[j], norm_w=norm_w[i, 0].reshape(1, dm))
            xc_new, xl = _ab_layer(streams, prm, ctx_out)
        else:
            prm = dict(w_in=ml_w_in[j], i_bias=ml_i_bias[j], f_bias=ml_f_bias[j], ml_norm_w=ml_norm_w[j],
                       w_out=ml_w_out[j], norm_w=norm_w[i, 0].reshape(1, dm))
            xc_new, xl = _ml_layer(streams, prm, ctx_out)
        nw2 =norm_w[i, 1].reshape(1, dm)
        xl = _ffn(xl, streams[1]["mod"], nw2, ffn_w1, ffn_w3, w2, i, streams[1]["rpm"])
        if ctx_out:
            xc = _ffn(xc_new, streams[0]["mod"], nw2, ffn_w1, ffn_w3, w2, i, streams[0]["rpm"])
    return final_norm(xl, final_norm_w.reshape(1, dm)).reshape(bsz, seq, dm)
```

```python
import functools
import math

import jax
import jax.numpy as jnp
from jax import lax
from jax.experimental import pallas as pl
from jax.experimental.pallas import tpu as pltpu

F32 = jnp.float32
BF16 = jnp.bfloat16

NORM_EPS = 1e-6
GRID_W = 64
LANES = 128
SUBLANES = 8
VMEM_LIMIT = 56 * 1024 * 1024

A_HEADS, A_DK = 8, 128
A_DIM = A_HEADS * A_DK
B_HEADS, B_N = 16, 64
B_DIM = B_HEADS * B_N
B_LORA = 64
B_G_LORA = 160
B_GN_EPS = 64e-5
B_DECAY_SCALE = math.exp(-0.5)
C_HEADS, C_DK, C_DV = 8, 128, 256
C_QK = C_HEADS * C_DK
C_V = C_HEADS * C_DV
C_GATE_CAP = 15.0
CONV_CH = 3 * A_DIM + 3 * B_DIM
CHUNK = 64
MOD_ROWS = 8


def _cparams(sem):
    return pltpu.CompilerParams(dimension_semantics=sem, vmem_limit_bytes=VMEM_LIMIT)


def _dot(a, b):
    return jnp.dot(a.astype(BF16), b.astype(BF16), preferred_element_type=F32)


def _dot_nt(a, b):
    return lax.dot_general(a.astype(BF16), b.astype(BF16), (((1,), (1,)), ((), ())),
                           preferred_element_type=F32)


def _dot_tn(a, b):
    return lax.dot_general(a.astype(BF16), b.astype(BF16), (((0,), (0,)), ((), ())),
                           preferred_element_type=F32)


def _split2(a):
    hi = a.astype(BF16)
    lo = (a - hi.astype(F32)).astype(BF16)
    return hi, lo


def _split3(a):
    hi = a.astype(BF16)
    r = a - hi.astype(F32)
    mid = r.astype(BF16)
    lo = (r - mid.astype(F32)).astype(BF16)
    return hi, mid, lo


def _dot_exact_rhs(a, b_exact):
    hi, lo = _split2(a)
    b = b_exact.astype(BF16)
    d = lambda t: jnp.dot(t, b, preferred_element_type=F32)
    return d(hi) + d(lo)


def _dot_exact_lhs(a_exact, b):
    hi, mid, lo = _split3(b)
    a = a_exact.astype(BF16)
    d = lambda t: jnp.dot(a, t, preferred_element_type=F32)
    return d(hi) + d(mid) + d(lo)


def _dot3(a, b):
    ah, al = _split2(a)
    bh, bl = _split2(b)
    d = lambda s, t: jnp.dot(s, t, preferred_element_type=F32)
    return d(ah, bh) + d(al, bh) + d(ah, bl)


def _sigmoid(t):
    return 1.0 / (1.0 + jnp.exp(-t))


def _silu(t):
    return t * _sigmoid(t)


def _softplus(t):
    return jnp.maximum(t, 0.0) + jnp.log(1.0 + jnp.exp(-jnp.abs(t)))


def _nilpotent_steps(c):
    return max(int(math.ceil(math.log2(c))) - 1, 0)


def _adaln_kernel(c_ref, w_ref, b_ref, o_ref):
    o_ref[...] = _dot3(_silu(c_ref[...]), w_ref[...]) + b_ref[...]


def adaln(cond, w, b, layer):
    m, d = cond.shape
    e = w.shape[2]
    tn = _col_tile(e, 512)
    return pl.pallas_call(
        _adaln_kernel,
        grid=(e // tn,),
        in_specs=[pl.BlockSpec((m, d), lambda j: (0, 0)),
                  pl.BlockSpec((None, d, tn), lambda j: (layer, 0, j)),
                  pl.BlockSpec((1, tn), lambda j: (0, j))],
        out_specs=pl.BlockSpec((m, tn), lambda j: (0, j)),
        out_shape=jax.ShapeDtypeStruct((m, e), F32),
        compiler_params=_cparams(("parallel",)),
        name="adaln",
    )(cond, w, b)


def _modulated(x_ref, mod_ref, nw_ref, shift_idx, scale_idx):
    x = x_ref[...]
    ms = jnp.mean(x * x, axis=-1, keepdims=True)
    gain = nw_ref[...] * (1.0 + mod_ref[0, scale_idx:scale_idx + 1, :])
    return x * lax.rsqrt(ms + NORM_EPS) * gain + mod_ref[0, shift_idx:shift_idx + 1, :]


def _modmm_kernel(x_ref, mod_ref, nw_ref, w_ref, ws_ref, o_ref, os_ref, h_ref, *, shift_idx, scale_idx, nmain):
    j = pl.program_id(1)

    @pl.when(j == 0)
    def _():
        h_ref[...] = _modulated(x_ref, mod_ref, nw_ref, shift_idx, scale_idx).astype(BF16)

    @pl.when(j < nmain)
    def _():
        o_ref[...] = jnp.dot(h_ref[...], w_ref[...], preferred_element_type=F32)

    @pl.when(j == nmain)
    def _():
        os_ref[...] = jnp.dot(h_ref[...], ws_ref[...], preferred_element_type=F32)


def _row_tile(m, want):
    t = min(want, m)
    while m % t:
        t //= 2
    return t


def _col_tile(n, want):
    t = min(want, n)
    while n % t or t % LANES:
        t -= LANES
    return t


def modmm(x, mod, nw, w, w_small, *, n_main, shift_idx, scale_idx, rows_per_mod, tm=1024, tn=1024):
    m, d = x.shape
    ns = w_small.shape[1]
    tm = _row_tile(min(m, rows_per_mod), tm)
    tn = _col_tile(n_main, tn)
    tpm = rows_per_mod // tm
    nmain = n_main // tn
    main_col = lambda j: jnp.minimum(j, nmain - 1)
    return pl.pallas_call(
        functools.partial(_modmm_kernel, shift_idx=shift_idx, scale_idx=scale_idx, nmain=nmain),
        grid=(m // tm, nmain + 1),
        in_specs=[pl.BlockSpec((tm, d), lambda i, j: (i, 0)),
                  pl.BlockSpec((1, MOD_ROWS, d), lambda i, j: (i // tpm, 0, 0)),
                  pl.BlockSpec((1, d), lambda i, j: (0, 0)),
                  pl.BlockSpec((d, tn), lambda i, j: (0, main_col(j))),
                  pl.BlockSpec((d, ns), lambda i, j: (0, 0))],
        out_specs=[pl.BlockSpec((tm, tn), lambda i, j: (i, main_col(j))),
                   pl.BlockSpec((tm, ns), lambda i, j: (i, 0))],
        out_shape=[jax.ShapeDtypeStruct((m, n_main), F32), jax.ShapeDtypeStruct((m, ns), F32)],
        scratch_shapes=[pltpu.VMEM((tm, d), BF16)],
        compiler_params=_cparams(("parallel", "arbitrary")),
        name="modmm",
    )(x, mod, nw, w, w_small)


def _ffn_up_kernel(x_ref, mod_ref, nw_ref, w1_ref, w3_ref, o_ref, h_ref, *, shift_idx, scale_idx):
    @pl.when(pl.program_id(1) == 0)
    def _():
        h_ref[...] = _modulated(x_ref, mod_ref, nw_ref, shift_idx, scale_idx).astype(BF16)

    h = h_ref[...]
    a = jnp.dot(h, w1_ref[...].astype(BF16), preferred_element_type=F32)
    b = jnp.dot(h, w3_ref[...].astype(BF16), preferred_element_type=F32)
    o_ref[...] = (_silu(a) * b).astype(BF16)


def ffn_up(x, mod, nw, w1, w3, layer, *, shift_idx, scale_idx, rows_per_mod, tm=1024, tn=512):
    m, d = x.shape
    n = w1.shape[2]
    tm = _row_tile(min(m, rows_per_mod), tm)
    tn = _col_tile(n, tn)
    tpm = rows_per_mod // tm
    return pl.pallas_call(
        functools.partial(_ffn_up_kernel, shift_idx=shift_idx, scale_idx=scale_idx),
        grid=(m // tm, n // tn),
        in_specs=[pl.BlockSpec((tm, d), lambda i, j: (i, 0)),
                  pl.BlockSpec((1, MOD_ROWS, d), lambda i, j: (i // tpm, 0, 0)),
                  pl.BlockSpec((1, d), lambda i, j: (0, 0)),
                  pl.BlockSpec((None, d, tn), lambda i, j: (layer, 0, j)),
                  pl.BlockSpec((None, d, tn), lambda i, j: (layer, 0, j))],
        out_specs=pl.BlockSpec((tm, tn), lambda i, j: (i, j)),
        out_shape=jax.ShapeDtypeStruct((m, n), BF16),
        scratch_shapes=[pltpu.VMEM((tm, d), BF16)],
        compiler_params=_cparams(("parallel", "arbitrary")),
        name="ffn_up",
    )(x, mod, nw, w1, w3)


def _mm_res_kernel(a_ref, w_ref, res_ref, mod_ref, o_ref, *, gate_idx):
    acc = jnp.dot(a_ref[...], w_ref[...], preferred_element_type=F32)
    o_ref[...] = res_ref[...] + mod_ref[0, gate_idx:gate_idx + 1, :] * acc


def mm_res(a, w, res, mod, layer, *, gate_idx, rows_per_mod, tm=1024, tn=512):
    m, k = a.shape
    n = w.shape[2]
    tm = _row_tile(min(m, rows_per_mod), tm)
    tn = _col_tile(n, tn)
    tpm = rows_per_mod // tm
    return pl.pallas_call(
        functools.partial(_mm_res_kernel, gate_idx=gate_idx),
        grid=(m // tm, n // tn),
        in_specs=[pl.BlockSpec((tm, k), lambda i, j: (i, 0)),
                  pl.BlockSpec((None, k, tn), lambda i, j: (layer, 0, j)),
                  pl.BlockSpec((tm, tn), lambda i, j: (i, j)),
                  pl.BlockSpec((1, MOD_ROWS, tn), lambda i, j: (i // tpm, 0, j))],
        out_specs=pl.BlockSpec((tm, tn), lambda i, j: (i, j)),
        out_shape=jax.ShapeDtypeStruct((m, n), F32),
        compiler_params=_cparams(("parallel", "arbitrary")),
        name="mm_res",
    )(a, w, res, mod)


def _final_norm_kernel(x_ref, w_ref, o_ref):
    x = x_ref[...]
    ms = jnp.mean(x * x, axis=-1, keepdims=True)
    o_ref[...] = x * lax.rsqrt(ms + NORM_EPS) * w_ref[...]


def final_norm(x, w, tm=512):
    m, d = x.shape
    tm = _row_tile(m, tm)
    return pl.pallas_call(
        _final_norm_kernel,
        grid=(m // tm,),
        in_specs=[pl.BlockSpec((tm, d), lambda i: (i, 0)), pl.BlockSpec((1, d), lambda i: (0, 0))],
        out_specs=pl.BlockSpec((tm, d), lambda i: (i, 0)),
        out_shape=jax.ShapeDtypeStruct((m, d), F32),
        compiler_params=_cparams(("parallel",)),
        name="final_norm",
    )(x, w)


CONV_ROWS = 256


def _conv_kernel(p_ref, w_ref, o_ref, s_ref, *, seq, rows, cols, pad, mode):
    zeros = jnp.zeros((pad, LANES), F32)
    for t in range(3):
        s_ref[t, 0:pad, :] = zeros
        s_ref[t, pad + seq:pad + seq + pad, :] = zeros
    s_ref[1, pad:pad + seq, :] = p_ref[...]
    rc = min(CONV_ROWS, seq)
    for c in range(seq // rc):
        base = c * rc
        col = (lax.broadcasted_iota(jnp.int32, (rc, 1), 0) + base) & (cols - 1)
        left = s_ref[1, pad - 1 + base:pad - 1 + base + rc, :]
        right = s_ref[1, pad + 1 + base:pad + 1 + base + rc, :]
        s_ref[0, pad + base:pad + base + rc, :] = jnp.where(col == 0, 0.0, left)
        s_ref[2, pad + base:pad + base + rc, :] = jnp.where(col == cols - 1, 0.0, right)
    drs = (0, 1, 2) if rows > 1 else (1,)
    for c in range(seq // rc):
        base = c * rc
        acc = jnp.zeros((rc, LANES), F32)
        for dr in drs:
            for dc in range(3):
                start = pad + base + (dr - 1) * cols
                acc = acc + s_ref[dc, start:start + rc, :] * w_ref[dr * 3 + dc:dr * 3 + dc + 1, :]
        if mode != "raw":
            acc = _silu(acc)
        if mode in ("q", "k"):
            acc = acc * lax.rsqrt(jnp.sum(acc * acc, axis=-1, keepdims=True) + 1e-6)
        if mode == "q":
            acc = acc * (A_DK ** -0.5)
        o_ref[base:base + rc, :] = acc


def grid_conv(p, conv_w, *, batch, seq, rows, col0, ncol, mode):
    cols = seq // rows
    pad = cols if rows > 1 else SUBLANES
    t0 = col0 // LANES
    return pl.pallas_call(
        functools.partial(_conv_kernel, seq=seq, rows=rows, cols=cols, pad=pad, mode=mode),
        grid=(batch, ncol // LANES),
        in_specs=[pl.BlockSpec((seq, LANES), lambda b, j: (b, t0 + j)),
                  pl.BlockSpec((16, LANES), lambda b, j: (0, t0 + j))],
        out_specs=pl.BlockSpec((seq, LANES), lambda b, j: (b, j)),
        out_shape=jax.ShapeDtypeStruct((batch * seq, ncol), F32),
        scratch_shapes=[pltpu.VMEM((3, seq + 2 * pad, LANES), F32)],
        compiler_params=_cparams(("parallel", "parallel")),
        name="grid_conv_" + mode,
    )(p, conv_w)


def _scan_block(seq, want):
    blk = min(want, seq // CHUNK)
    while (seq // CHUNK) % blk:
        blk -= 1
    return blk * CHUNK


def _chunk_masks(c, rev):
    ii = lax.broadcasted_iota(jnp.int32, (c, c), 0)
    jj = lax.broadcasted_iota(jnp.int32, (c, c), 1)
    eye = ii == jj
    incl = (jj >= ii) if rev else (jj <= ii)
    incl_t = (jj <= ii) if rev else (jj >= ii)
    return eye, incl, incl_t


def _gdn_kernel(q_ref, k_ref, v_ref, g_ref, av_ref, dv_ref, s0_ref, o_ref, s_ref, *, rev, d, heads):
    c = CHUNK
    c2 = 2 * c
    dk = A_DK
    nsub = q_ref.shape[0] // c

    @pl.when(pl.program_id(2) == 0)
    def _():
        s_ref[...] = s0_ref[...]

    eye_s, incl_s, strict_s = _side_masks(c, rev)
    ti = lax.broadcasted_iota(jnp.int32, (c, c2), 0)
    tj = lax.broadcasted_iota(jnp.int32, (c, c2), 1) & (c - 1)
    incl_t_s = (ti >= tj) if rev else (ti <= tj)
    eye_sf = eye_s.astype(F32)
    lane_lo = lax.broadcasted_iota(jnp.int32, (1, c2), 1) < c
    lo_s = jnp.logical_and(incl_s, lane_lo)
    hi_s = jnp.logical_and(incl_s, jnp.logical_not(lane_lo))
    first = lax.broadcasted_iota(jnp.int32, (1, 2 * dk), 1) < dk
    gl = g_ref[...]
    log_alpha = -jnp.exp(av_ref[...]) * _softplus(gl + dv_ref[...])
    beta_all = _sigmoid(gl)
    steps = _nilpotent_steps(c)
    npair = heads // 2
    rows = [slice(ci * c, (ci + 1) * c) for ci in range(nsub)]
    items = [(ci, p) for ci in range(nsub) for p in range(npair)]
    idx = {it: n for n, it in enumerate(items)}
    ids = range(len(items))
    gate_col = lambda mat, ci, i: mat[rows[ci], i:i + 1]
    side = lambda x0, x1: jnp.where(lane_lo, x0, x1)
    wide = lambda x0, x1: jnp.concatenate([jnp.broadcast_to(x0, (c, dk)), jnp.broadcast_to(x1, (c, dk))], axis=1)
    pair = lambda ref, ci, p: ref[rows[ci], 2 * p * dk:(2 * p + 2) * dk]
    blockdiag = lambda t: jnp.concatenate([jnp.where(first, t, 0.0), jnp.where(first, 0.0, t)], axis=0)
    zeros = jnp.zeros((c, 2 * dk), F32)

    la0 = [d * A_HEADS + 2 * p for ci, p in items]
    g0 = [gate_col(log_alpha, items[n][0], la0[n]) for n in ids]
    g1 = [gate_col(log_alpha, items[n][0], la0[n] + 1) for n in ids]
    beta = [wide(gate_col(beta_all, items[n][0], 2 * A_HEADS + la0[n]),
                 gate_col(beta_all, items[n][0], 2 * A_HEADS + la0[n] + 1)) for n in ids]
    g_side = [side(g0[n], g1[n]) for n in ids]
    g_row = [jnp.sum(jnp.where(eye_s, g_side[n], 0.0), axis=0, keepdims=True) for n in ids]
    gc_row = [jnp.sum(jnp.where(incl_t_s, g_side[n], 0.0), axis=0, keepdims=True) for n in ids]
    g_row_b = [jnp.broadcast_to(g_row[n], (c, c2)) for n in ids]
    gc0 = [jnp.sum(jnp.where(lo_s, g_row_b[n], 0.0), axis=1, keepdims=True) for n in ids]
    gc1 = [jnp.sum(jnp.where(hi_s, g_row_b[n], 0.0), axis=1, keepdims=True) for n in ids]
    gts = [(jnp.sum(g0[n], axis=0, keepdims=True), jnp.sum(g1[n], axis=0, keepdims=True)) for n in ids]
    decay = [jnp.where(incl_s, jnp.exp(jnp.where(incl_s, side(gc0[n], gc1[n]) - gc_row[n], 0.0)), 0.0) for n in ids]
    egc = [wide(jnp.exp(gc0[n]), jnp.exp(gc1[n])) for n in ids]
    k = [pair(k_ref, ci, p) for ci, p in items]
    kb = [k[n] * beta[n] for n in ids]
    qe, big = [], []
    for n, (ci, p) in enumerate(items):
        q = pair(q_ref, ci, p)
        big.append(_dot_nt(jnp.concatenate([kb[n], q], axis=0), blockdiag(k[n])))
        qe.append(q * egc[n])
    m_neg = [jnp.where(strict_s, -(big[n][0:c] * decay[n]), 0.0) for n in ids]
    qk = [jnp.where(incl_s, big[n][c:c2] * decay[n], 0.0) for n in ids]
    ktil = [k[n] * wide(jnp.exp(gts[n][0] - gc0[n]), jnp.exp(gts[n][1] - gc1[n])) for n in ids]
    xs = _side_inverse_many(m_neg, eye_sf, lane_lo, steps)
    resid = [eye_sf - xs[n] + _dot3(m_neg[n], _stack_heads(xs[n], lane_lo)) for n in ids]
    xs = [xs[n] + _dot(xs[n], _stack_heads(resid[n], lane_lo)) for n in ids]
    uw = []
    for n, (ci, p) in enumerate(items):
        vb = pair(v_ref, ci, p) * beta[n]
        ke = kb[n] * egc[n]
        r0 = jnp.concatenate([vb[:, 0:dk], ke[:, 0:dk], zeros], axis=1)
        r1 = jnp.concatenate([zeros, vb[:, dk:2 * dk], ke[:, dk:2 * dk]], axis=1)
        uw.append(_dot(xs[n], jnp.concatenate([r0, r1], axis=0)))
    s_cur = [s_ref[0, h] for h in range(heads)]
    for ci in (reversed(range(nsub)) if rev else range(nsub)):
        ns = [idx[(ci, p)] for p in range(npair)]
        ws_qs = [[_dot(jnp.concatenate([uw[ns[p]][:, (2 * j + 1) * dk:(2 * j + 2) * dk],
                                        qe[ns[p]][:, j * dk:(j + 1) * dk]], axis=0), s_cur[2 * p + j])
                  for j in range(2)] for p in range(npair)]
        v_new = [[uw[ns[p]][:, 2 * j * dk:(2 * j + 1) * dk] - ws_qs[p][j][0:c] for j in range(2)] for p in range(npair)]
        for p in range(npair):
            inner = _dot(qk[ns[p]], blockdiag(jnp.concatenate(v_new[p], axis=1)))
            o_ref[rows[ci], 2 * p * dk:(2 * p + 2) * dk] = (
                jnp.concatenate([ws_qs[p][0][c:c2], ws_qs[p][1][c:c2]], axis=1) + inner)
        s_cur = [s_cur[2 * p + j] * jnp.exp(gts[ns[p]][j])
                 + _dot_tn(ktil[ns[p]][:, j * dk:(j + 1) * dk], v_new[p][j])
                 for p in range(npair) for j in range(2)]
    for h in range(heads):
        s_ref[0, h] = s_cur[h]


def gdn_scan(q, k, v, gates, a_vec, dt_vec, s0, *, batch, seq, rev, d, gate_blk, sub=4):
    blk = _scan_block(seq, sub)
    nc = seq // blk
    heads, ng = A_HEADS, 1
    w = heads * A_DK

    def row(b, c):
        return b * nc + ((nc - 1 - c) if rev else c)

    tok = pl.BlockSpec((blk, w), lambda b, h, c: (row(b, c), h))
    vec = pl.BlockSpec((1, LANES), lambda b, h, c: (0, 0))
    st = pl.BlockSpec((1, heads, A_DK, A_DK), lambda b, h, c: (b, h, 0, 0))
    return pl.pallas_call(
        functools.partial(_gdn_kernel, rev=rev, d=d, heads=heads),
        grid=(batch, ng, nc),
        in_specs=[tok, tok, tok,
                  pl.BlockSpec((blk, LANES), lambda b, h, c: (row(b, c), gate_blk)),
                  vec, vec, st],
        out_specs=[tok, st],
        out_shape=[jax.ShapeDtypeStruct((batch * seq, A_DIM), F32),
                   jax.ShapeDtypeStruct((batch, A_HEADS, A_DK, A_DK), F32)],
        compiler_params=_cparams(("parallel", "parallel", "arbitrary")),
        name="gdn_scan",
    )(q, k, v, gates, a_vec, dt_vec, s0)


def _group_ones(scale):
    ii = lax.broadcasted_iota(jnp.int32, (LANES, LANES), 0)
    jj = lax.broadcasted_iota(jnp.int32, (LANES, LANES), 1)
    sh = int(math.log2(B_N))
    return jnp.where((ii >> sh) == (jj >> sh), scale, 0.0).astype(F32)


def _rw_feat_kernel(ps_ref, kb_ref, wup_ref, aup_ref, gup_ref, w0_ref, a0_ref, kk_w_ref, ka_w_ref,
                    logw_ref, kk_ref, kka_ref, kt_ref, g_ref):
    ps = ps_ref[...]
    kb = kb_ref[...]
    lw = w0_ref[...] + _dot(jnp.tanh(ps[:, 0:2 * B_LORA]), wup_ref[...])
    logw_ref[...] = -B_DECAY_SCALE * _sigmoid(lw)
    a = _sigmoid(a0_ref[...] + _dot(ps[:, 2 * B_LORA:4 * B_LORA], aup_ref[...]))
    g_ref[...] = _dot(_sigmoid(ps[:, 4 * B_LORA:]), gup_ref[...])
    kkw = kb * kk_w_ref[...]
    ones = _group_ones(1.0)
    for t in range(B_DIM // LANES):
        sl = slice(t * LANES, (t + 1) * LANES)
        x = kkw[:, sl]
        ss = _dot_exact_rhs(x * x, ones)
        kk_ref[:, sl] = x * lax.rsqrt(ss + 1e-6)
    kk = kk_ref[...]
    for dd in range(2):
        a_d = a[:, dd * B_DIM:(dd + 1) * B_DIM]
        kka_ref[:, dd * B_DIM:(dd + 1) * B_DIM] = kk * a_d
        kt_ref[:, dd * B_DIM:(dd + 1) * B_DIM] = kb * (1.0 + (a_d - 1.0) * ka_w_ref[...])


def rw_features(ps, rkv, wup, aup, gup, w0, a0, k_k, k_a, tm=256):
    m = ps.shape[0]
    tm = _row_tile(m, tm)
    full = lambda shp: pl.BlockSpec(shp, lambda i: (0, 0))
    two = jax.ShapeDtypeStruct((m, 2 * B_DIM), F32)
    one = jax.ShapeDtypeStruct((m, B_DIM), F32)
    return pl.pallas_call(
        _rw_feat_kernel,
        grid=(m // tm,),
        in_specs=[pl.BlockSpec((tm, 512), lambda i: (i, 0)),
                  pl.BlockSpec((tm, B_DIM), lambda i: (i, 1)),
                  full((2 * B_LORA, 2 * B_DIM)), full((2 * B_LORA, 2 * B_DIM)), full((256, B_DIM)),
                  full((1, 2 * B_DIM)), full((1, 2 * B_DIM)), full((1, B_DIM)), full((1, B_DIM))],
        out_specs=[pl.BlockSpec((tm, 2 * B_DIM), lambda i: (i, 0)),
                   pl.BlockSpec((tm, B_DIM), lambda i: (i, 0)),
                   pl.BlockSpec((tm, 2 * B_DIM), lambda i: (i, 0)),
                   pl.BlockSpec((tm, 2 * B_DIM), lambda i: (i, 0)),
                   pl.BlockSpec((tm, B_DIM), lambda i: (i, 0))],
        out_shape=[two, one, two, two, one],
        compiler_params=_cparams(("parallel",)),
        name="rw_features",
    )(ps, rkv, wup, aup, gup, w0, a0, k_k, k_a)


def _stack_heads(t, lane_lo):
    return jnp.concatenate([jnp.where(lane_lo, t, 0.0), jnp.where(lane_lo, 0.0, t)], axis=0)


def _side_masks(c, rev):
    ti = lax.broadcasted_iota(jnp.int32, (c, 2 * c), 0)
    tj = lax.broadcasted_iota(jnp.int32, (c, 2 * c), 1) & (c - 1)
    incl = (tj >= ti) if rev else (tj <= ti)
    eye = tj == ti
    return eye, incl, jnp.logical_and(incl, jnp.logical_not(eye))


def _side_inverse_many(n_mats, eye_f, lane_lo, steps):
    c = n_mats[0].shape[0]
    xs = [eye_f + n for n in n_mats]
    if steps == 0:
        return xs
    pws = [_dot(p, _stack_heads(p, lane_lo)) for p in n_mats]
    for _ in range(steps - 1):
        both = [_dot(jnp.concatenate([x, p], axis=0), _stack_heads(p, lane_lo)) for x, p in zip(xs, pws)]
        xs = [x + bth[0:c] for x, bth in zip(xs, both)]
        pws = [bth[c:2 * c] for bth in both]
    return [x + _dot(x, _stack_heads(p, lane_lo)) for x, p in zip(xs, pws)]


def _rwkv_kernel(r_ref, lw_ref, kk_ref, kka_ref, v_ref, kt_ref, s0_ref, y_ref, s_ref, *, rev, pairs):
    c = CHUNK
    nsub = r_ref.shape[0] // c

    @pl.when(pl.program_id(2) == 0)
    def _():
        s_ref[...] = s0_ref[...]

    _, incl, _ = _chunk_masks(c, rev)
    incl_f = incl.astype(F32)
    eye_s, incl_s, strict_s = _side_masks(c, rev)
    eye_sf = eye_s.astype(F32)
    lane_lo = lax.broadcasted_iota(jnp.int32, (1, LANES), 1) < B_N
    ii = lax.broadcasted_iota(jnp.int32, (LANES, LANES), 0)
    jj = lax.broadcasted_iota(jnp.int32, (LANES, LANES), 1)
    sh = int(math.log2(B_N))
    same_head = (ii >> sh) == (jj >> sh)
    steps = _nilpotent_steps(c)
    stk = lambda t: _stack_heads(t, lane_lo)
    rng = range(pairs)
    sls = [slice(p * LANES, (p + 1) * LANES) for p in rng]
    rows = [slice(ci * c, (ci + 1) * c) for ci in range(nsub)]
    items = [(ci, p) for ci in range(nsub) for p in rng]
    idx = {it: n for n, it in enumerate(items)}
    ids = range(len(items))

    lw_all = [lw_ref[rows[ci], :] for ci in range(nsub)]
    g_all = [_dot_exact_lhs(incl_f, lw_all[ci]) for ci in range(nsub)]
    gtot_all = [jnp.sum(lw_all[ci], axis=0, keepdims=True) for ci in range(nsub)]

    a_n, r_n, v_n, big, upd_rhs = [], [], [], [], []
    for ci, p in items:
        sl, rw = sls[p], rows[ci]
        g_in = g_all[ci][:, sl]
        e_neg = jnp.exp(-g_in)
        e_end = jnp.exp(gtot_all[ci][:, sl] - g_in)
        kk, kka, kt = kk_ref[rw, sl], kka_ref[rw, sl], kt_ref[rw, sl]
        a_n.append(kk * jnp.exp(g_in - lw_all[ci][:, sl]))
        r_n.append(r_ref[rw, sl] * jnp.exp(g_in))
        v_n.append(v_ref[rw, sl])
        upd_rhs.append(jnp.concatenate([-kka * e_end, kt * e_end], axis=0))
        bk_s = jnp.concatenate([stk(-kka * e_neg), stk(kt * e_neg)], axis=0)
        big.append(_dot_nt(jnp.concatenate([a_n[-1], r_n[-1]], axis=0), bk_s))
    l_ab = [jnp.where(strict_s, big[n][0:c, 0:LANES], 0.0) for n in ids]
    xs = _side_inverse_many(l_ab, eye_sf, lane_lo, steps)
    v_s = [stk(v_n[n]) for n in ids]
    lv = [_dot(jnp.where(strict_s, big[n][0:c, LANES:2 * LANES], 0.0), v_s[n]) for n in ids]
    wu = [_dot(xs[n], jnp.concatenate([stk(a_n[n]), stk(lv[n])], axis=1)) for n in ids]
    rq = [_dot(jnp.where(incl_s, big[n][c:2 * c, 0:LANES], 0.0),
               jnp.concatenate([stk(wu[n][:, 0:LANES]), stk(wu[n][:, LANES:2 * LANES])], axis=1)) for n in ids]
    y0 = [rq[n][:, LANES:2 * LANES] + _dot(jnp.where(incl_s, big[n][c:2 * c, LANES:2 * LANES], 0.0), v_s[n])
          for n in ids]
    rq_n = [r_n[n] + rq[n][:, 0:LANES] for n in ids]
    s_cur = [s_ref[0, p] for p in rng]
    for ci in (reversed(range(nsub)) if rev else range(nsub)):
        ns = [idx[(ci, p)] for p in rng]
        u = [_dot_nt(wu[ns[p]][:, 0:LANES], s_cur[p]) + wu[ns[p]][:, LANES:2 * LANES] for p in rng]
        for p in rng:
            y_ref[rows[ci], sls[p]] = _dot_nt(rq_n[ns[p]], s_cur[p]) + y0[ns[p]]
        add = [_dot_tn(jnp.concatenate([u[p], v_n[ns[p]]], axis=0), upd_rhs[ns[p]]) for p in rng]
        s_cur = [s_cur[p] * jnp.exp(gtot_all[ci][:, sls[p]]) + jnp.where(same_head, add[p], 0.0) for p in rng]
    for p in rng:
        s_ref[0, p] = s_cur[p]


def rwkv_scan(rkv, logw, kk, kka, kt, s0, *, batch, seq, rev, d, pairs=8, sub=4):
    blk = _scan_block(seq, sub)
    nc = seq // blk
    npair = B_DIM // LANES
    ng = npair // pairs
    w = pairs * LANES

    def row(b, c):
        return b * nc + ((nc - 1 - c) if rev else c)

    def tok(off_blocks):
        return pl.BlockSpec((blk, w), lambda b, h, c: (row(b, c), off_blocks + h))

    per_dir = d * ng
    st = pl.BlockSpec((1, pairs, LANES, LANES), lambda b, h, c: (b, h, 0, 0))
    return pl.pallas_call(
        functools.partial(_rwkv_kernel, rev=rev, pairs=pairs),
        grid=(batch, ng, nc),
        in_specs=[tok(0), tok(per_dir), tok(0), tok(per_dir), tok(2 * ng), tok(per_dir), st],
        out_specs=[tok(0), st],
        out_shape=[jax.ShapeDtypeStruct((batch * seq, B_DIM), F32),
                   jax.ShapeDtypeStruct((batch, npair, LANES, LANES), F32)],
        compiler_params=_cparams(("parallel", "parallel", "arbitrary")),
        name="rwkv_scan",
    )(rkv, logw, kk, kka, rkv, kt, s0)


def _ab_out_kernel(of_ref, ob_ref, yf_ref, yb_ref, z_ref, r_ref, v_ref, kt_ref, g_ref, x_ref, mod_ref,
                   gnw_ref, rk_ref, lnw_ref, lnb_ref, w_ref, o_ref, mix_ref, *, gate_idx):
    for h in range(A_HEADS):
        sl = slice(h * A_DK, (h + 1) * A_DK)
        o = of_ref[:, sl] + ob_ref[:, sl]
        ms = jnp.mean(o * o, axis=-1, keepdims=True)
        ya = o * lax.rsqrt(ms + NORM_EPS) * gnw_ref[...]
        mix_ref[:, sl] = (ya * _silu(z_ref[:, sl])).astype(BF16)
    avg = _group_ones(1.0 / B_N)
    ones = _group_ones(1.0)
    for t in range(B_DIM // LANES):
        sl = slice(t * LANES, (t + 1) * LANES)
        y = yf_ref[:, sl] + yb_ref[:, sl]
        mu = _dot_exact_rhs(y, avg)
        dlt = y - mu
        var = _dot_exact_rhs(dlt * dlt, avg)
        yn = dlt * lax.rsqrt(var + B_GN_EPS)
        kt_sum = kt_ref[:, sl] + kt_ref[:, B_DIM + t * LANES:B_DIM + (t + 1) * LANES]
        bonus = _dot_exact_rhs(r_ref[:, sl] * kt_sum * rk_ref[:, sl], ones) * v_ref[:, sl]
        yb = (yn * lnw_ref[:, sl] + lnb_ref[:, sl] + bonus) * g_ref[:, sl]
        mix_ref[:, A_DIM + t * LANES:A_DIM + (t + 1) * LANES] = yb.astype(BF16)
    acc = jnp.dot(mix_ref[...], w_ref[...], preferred_element_type=F32)
    o_ref[...] = x_ref[...] + mod_ref[0, gate_idx:gate_idx + 1, :] * acc


def ab_out(o_f, o_b, y_f, y_b, p_main, rkv, kt, g_out, x, mod, gnw, rk, lnw, lnb, w_out, *,
           gate_idx, rows_per_mod, tm=256):
    m, dm = x.shape
    tm = _row_tile(min(m, rows_per_mod), tm)
    tpm = rows_per_mod // tm
    zblk = CONV_CH // A_DIM
    row = lambda width, blk=0: pl.BlockSpec((tm, width), lambda i: (i, blk))
    full = lambda shp: pl.BlockSpec(shp, lambda i: (0, 0))
    return pl.pallas_call(
        functools.partial(_ab_out_kernel, gate_idx=gate_idx),
        grid=(m // tm,),
        in_specs=[row(A_DIM), row(A_DIM), row(B_DIM), row(B_DIM), row(A_DIM, zblk),
                  row(B_DIM, 0), row(B_DIM, 2), row(2 * B_DIM), row(B_DIM), row(dm),
                  pl.BlockSpec((1, MOD_ROWS, dm), lambda i: (i // tpm, 0, 0)),
                  full((1, A_DK)), full((1, B_DIM)), full((1, B_DIM)), full((1, B_DIM)),
                  pl.BlockSpec((A_DIM + B_DIM, dm), lambda i: (0, 0), pipeline_mode=pl.Buffered(1))],
        out_specs=row(dm),
        out_shape=jax.ShapeDtypeStruct((m, dm), F32),
        scratch_shapes=[pltpu.VMEM((tm, A_DIM + B_DIM), BF16)],
        compiler_params=_cparams(("parallel",)),
        name="ab_out",
    )(o_f, o_b, y_f, y_b, p_main, rkv, rkv, kt, g_out, x, mod, gnw, rk, lnw, lnb, w_out)


def _mlstm_kernel(q_ref, k_ref, v_ref, g_ref, ib_ref, fb_ref, c0_ref, n0_ref, m0_ref,
                  h_ref, c_ref, n_ref, m_ref, *, rev, d, heads):
    c = CHUNK
    nsub = q_ref.shape[0] // c

    @pl.when(pl.program_id(2) == 0)
    def _():
        c_ref[...] = c0_ref[...]
        n_ref[...] = n0_ref[...]
        m_ref[...] = m0_ref[...]

    eye, incl, incl_t = _chunk_masks(c, rev)
    gl = g_ref[...]
    cap = C_GATE_CAP
    ig_all = cap * jnp.tanh((gl + ib_ref[...]) / cap)
    fz = cap * jnp.tanh((gl + fb_ref[...]) / cap)
    logf_all = jnp.minimum(fz, 0.0) - jnp.log(1.0 + jnp.exp(-jnp.abs(fz)))
    rng = range(heads)
    qs = [slice(h * C_DK, (h + 1) * C_DK) for h in rng]
    vs = [slice(h * C_DV, (h + 1) * C_DV) for h in rng]
    rows = [slice(ci * c, (ci + 1) * c) for ci in range(nsub)]
    order = list(reversed(range(nsub))) if rev else list(range(nsub))
    items = [(ci, h) for ci in range(nsub) for h in rng]
    idx = {it: n for n, it in enumerate(items)}
    ids = range(len(items))
    q = [q_ref[rows[ci], qs[h]] * (C_DK ** -0.5) for ci, h in items]
    qk = [_dot_nt(q[n], k_ref[rows[ci], qs[h]]) for n, (ci, h) in enumerate(items)]
    lane = lax.broadcasted_iota(jnp.int32, (1, LANES), 1)
    gate_col = lambda mat, ci, i: jnp.sum(jnp.where(lane == i, mat[rows[ci]], 0.0), axis=1, keepdims=True)
    bcast = lambda col: jnp.broadcast_to(col, (c, c))
    ig =[gate_col(ig_all, ci, d * C_HEADS + h) for ci, h in items]
    logf = [gate_col(logf_all, ci, 2 * C_HEADS + d * C_HEADS + h) for ci, h in items]
    f_row = [jnp.sum(jnp.where(eye, bcast(logf[n]), 0.0), axis=0, keepdims=True) for n in ids]
    ig_row = [jnp.sum(jnp.where(eye, bcast(ig[n]), 0.0), axis=0, keepdims=True) for n in ids]
    b_row = [jnp.sum(jnp.where(incl_t, bcast(logf[n]), 0.0), axis=0, keepdims=True) for n in ids]
    b_col = [jnp.sum(jnp.where(incl, jnp.broadcast_to(f_row[n], (c, c)), 0.0), axis=1, keepdims=True) for n in ids]
    btot = [jnp.sum(logf[n], axis=0, keepdims=True) for n in ids]
    w_end = [btot[n] - b_col[n] + ig[n] for n in ids]
    w_max = [jnp.max(w_end[n], axis=0, keepdims=True) for n in ids]
    dmat = [b_col[n] - b_row[n] + ig_row[n] for n in ids]
    dmax = [jnp.max(jnp.where(incl, dmat[n], -1e30), axis=1, keepdims=True) for n in ids]
    m_st, m_new = [None] * len(items), [None] * len(items)
    m_cur = [m_ref[0, h, 0:1, 0:1] for h in rng]
    for ci in order:
        for h in rng:
            n = idx[(ci, h)]
            m_st[n] = m_cur[h]
            m_new[n] = jnp.maximum(btot[n] + m_cur[h], w_max[n])
            m_cur[h] = m_new[n]
    scale = [jnp.exp(btot[n] + m_st[n] - m_new[n]) for n in ids]
    kw = [k_ref[rows[ci], qs[h]] * jnp.exp(w_end[n] - m_new[n]) for n, (ci, h) in enumerate(items)]
    kw_sum = [jnp.sum(kw[n], axis=0, keepdims=True) for n in ids]
    m_row = [jnp.maximum(b_col[n] + m_st[n], dmax[n]) for n in ids]
    inter = [jnp.exp(b_col[n] + m_st[n] - m_row[n]) for n in ids]
    s = [jnp.where(incl, qk[n] * jnp.exp(jnp.where(incl, dmat[n] - m_row[n], 0.0)), 0.0) for n in ids]
    s_sum = [jnp.sum(s[n], axis=1, keepdims=True) for n in ids]
    floor = [jnp.exp(-m_row[n]) for n in ids]
    sv = [_dot(s[n], v_ref[rows[ci], vs[h]]) for n, (ci, h) in enumerate(items)]
    kv = [_dot_tn(kw[n], v_ref[rows[ci], vs[h]]) for n, (ci, h) in enumerate(items)]
    c_cur = [c_ref[0, h] for h in rng]
    n_cur = [n_ref[0, h, 0:1, :] for h in rng]
    for ci in order:
        ns = [idx[(ci, h)] for h in rng]
        qc = [_dot(q[ns[h]], c_cur[h]) for h in rng]
        for h in rng:
            n = ns[h]
            den = s_sum[n] + inter[n] * jnp.sum(q[n] * n_cur[h], axis=1, keepdims=True)
            num = sv[n] + inter[n] * qc[h]
            h_ref[rows[ci], vs[h]] = num / jnp.maximum(jnp.abs(den), floor[n])
        c_cur = [c_cur[h] * scale[ns[h]] + kv[ns[h]] for h in rng]
        n_cur = [n_cur[h] * scale[ns[h]] + kw_sum[ns[h]] for h in rng]
    for h in rng:
        c_ref[0, h] = c_cur[h]
        n_ref[0, h] = jnp.broadcast_to(n_cur[h], (SUBLANES, C_DK))
        m_ref[0, h] = jnp.broadcast_to(m_cur[h], (SUBLANES, LANES))


def mlstm_scan(p, gates, ib_vec, fb_vec, c0, n0, m0, *, batch, seq, rev, d, sub=4):
    blk = _scan_block(seq, sub)
    nc = seq // blk
    heads, ng = C_HEADS, 1

    def row(b, c):
        return b * nc + ((nc - 1 - c) if rev else c)

    wq, wv = heads * C_DK, heads * C_DV
    qspec = pl.BlockSpec((blk, wq), lambda b, h, c: (row(b, c), h))
    kspec = pl.BlockSpec((blk, wq), lambda b, h, c: (row(b, c), ng + h))
    vspec = pl.BlockSpec((blk, wv), lambda b, h, c: (row(b, c), (2 * C_QK) // wv + h))
    hspec = pl.BlockSpec((blk, wv), lambda b, h, c: (row(b, c), h))
    vec = pl.BlockSpec((1, LANES), lambda b, h, c: (0, 0))
    cst = pl.BlockSpec((1, heads, C_DK, C_DV), lambda b, h, c: (b, h, 0, 0))
    nst = pl.BlockSpec((1, heads, SUBLANES, C_DK), lambda b, h, c: (b, h, 0, 0))
    mst = pl.BlockSpec((1, heads, SUBLANES, LANES), lambda b, h, c: (b, h, 0, 0))
    return pl.pallas_call(
        functools.partial(_mlstm_kernel, rev=rev, d=d, heads=heads),
        grid=(batch, ng, nc),
        in_specs=[qspec, kspec, vspec,
                  pl.BlockSpec((blk, LANES), lambda b, h, c: (row(b, c), 0)),
                  vec, vec, cst, nst, mst],
        out_specs=[hspec, cst, nst, mst],
        out_shape=[jax.ShapeDtypeStruct((batch * seq, C_V), F32),
                   jax.ShapeDtypeStruct((batch, C_HEADS, C_DK, C_DV), F32),
                   jax.ShapeDtypeStruct((batch, C_HEADS, SUBLANES, C_DK), F32),
                   jax.ShapeDtypeStruct((batch, C_HEADS, SUBLANES, LANES), F32)],
        compiler_params=_cparams(("parallel", "parallel", "arbitrary")),
        name="mlstm_scan",
    )(p, p, p, gates, ib_vec, fb_vec, c0, n0, m0)


def _ml_out_kernel(hf_ref, hb_ref, o_ref_in, x_ref, mod_ref, nw_ref, w_ref, out_ref, mix_ref, *, gate_idx):
    for h in range(C_HEADS):
        sl = slice(h * C_DV, (h + 1) * C_DV)
        hh = hf_ref[:, sl] + hb_ref[:, sl]
        ms = jnp.mean(hh * hh, axis=-1, keepdims=True)
        y = hh * lax.rsqrt(ms + NORM_EPS) * nw_ref[:, sl]
        mix_ref[:, sl] = (y * _sigmoid(o_ref_in[:, sl])).astype(BF16)
    acc = jnp.dot(mix_ref[...], w_ref[...], preferred_element_type=F32)
    out_ref[...] = x_ref[...] + mod_ref[0, gate_idx:gate_idx + 1, :] * acc


def ml_out(h_f, h_b, p, x, mod, nw, w_out, *, gate_idx, rows_per_mod, tm=256):
    m, dm = x.shape
    tm = _row_tile(min(m, rows_per_mod), tm)
    tpm = rows_per_mod // tm
    oblk = (2 * C_QK + C_V) // C_V
    row = lambda width, blk=0: pl.BlockSpec((tm, width), lambda i: (i, blk))
    full = lambda shp: pl.BlockSpec(shp, lambda i: (0, 0))
    return pl.pallas_call(
        functools.partial(_ml_out_kernel, gate_idx=gate_idx),
        grid=(m // tm,),
        in_specs=[row(C_V), row(C_V), row(C_V, oblk), row(dm),
                  pl.BlockSpec((1, MOD_ROWS, dm), lambda i: (i // tpm, 0, 0)),
                  full((1, C_V)), pl.BlockSpec((C_V, dm), lambda i: (0, 0), pipeline_mode=pl.Buffered(1))],
        out_specs=row(dm),
        out_shape=jax.ShapeDtypeStruct((m, dm), F32),
        scratch_shapes=[pltpu.VMEM((tm, C_V), BF16)],
        compiler_params=_cparams(("parallel",)),
        name="ml_out",
    )(h_f, h_b, p, x, mod, nw, w_out)


def _lane_vec(values, offset):
    flat = values.reshape(-1).astype(F32)
    return jnp.zeros((1, LANES), F32).at[0, offset:offset + flat.shape[0]].set(flat)


def _ffn(x, mod, nw, w1, w3, w2, layer, rows_per_mod):
    g = ffn_up(x, mod, nw, w1, w3, layer, shift_idx=3, scale_idx=4, rows_per_mod=rows_per_mod)
    return mm_res(g, w2, x, mod, layer, gate_idx=5, rows_per_mod=rows_per_mod)


def _ab_layer(streams, prm, ctx_out):
    w_in = prm["w_in"]
    w_all = w_in.astype(BF16)
    small = w_in[:, CONV_CH + A_DIM + 4 * A_HEADS:]
    gates_w = w_in[:, CONV_CH + A_DIM:CONV_CH + A_DIM + 4 * A_HEADS]
    d_model = w_in.shape[0]
    w_small = jnp.concatenate(
        [small, jnp.zeros((d_model, 512 - small.shape[1]), F32),
         gates_w, jnp.zeros((d_model, LANES - gates_w.shape[1]), F32)], axis=1).astype(BF16)
    conv_w = jnp.concatenate([prm["conv_w"].reshape(9, CONV_CH), jnp.zeros((7, CONV_CH), F32)], axis=0)
    a_vec = _lane_vec(prm["a_log"], 0)
    dt_vec = _lane_vec(prm["dt_bias"], 0)
    zero_up = jnp.zeros((B_LORA, B_DIM), F32)
    blockdiag = lambda u: jnp.concatenate(
        [jnp.concatenate([u[0], zero_up], axis=1), jnp.concatenate([zero_up, u[1]], axis=1)], axis=0).astype(BF16)
    wup, aup = blockdiag(prm["w_up"]), blockdiag(prm["a_up"])
    gup = jnp.concatenate([prm["g_up"], jnp.zeros((256 - B_G_LORA, B_DIM), F32)], axis=0).astype(BF16)
    w0 = prm["w0"].reshape(1, 2 * B_DIM)
    a0 = prm["a0"].reshape(1, 2 * B_DIM)
    k_k = prm["k_k"].reshape(1, B_DIM)
    k_a = prm["k_a"].reshape(1, B_DIM)
    gnw = prm["gdn_norm_w"].reshape(1, A_DK)
    rk = prm["r_k"].reshape(1, B_DIM)
    lnw = prm["ln_w"].reshape(1, B_DIM)
    lnb = prm["ln_b"].reshape(1, B_DIM)
    w_out = prm["w_out"].astype(BF16)

    feats = []
    for st in streams:
        kw = dict(shift_idx=0, scale_idx=1, rows_per_mod=st["rpm"])
        p_main, p_small = modmm(st["x"], st["mod"], prm["norm_w"], w_all, w_small, n_main=CONV_CH + A_DIM, **kw)
        ckw = dict(batch=st["batch"], seq=st["seq"], rows=st["rows"])
        q = grid_conv(p_main, conv_w, col0=0, ncol=A_DIM, mode="q", **ckw)
        k = grid_conv(p_main, conv_w, col0=A_DIM, ncol=A_DIM, mode="k", **ckw)
        v = grid_conv(p_main, conv_w, col0=2 * A_DIM, ncol=A_DIM, mode="v", **ckw)
        rkv = grid_conv(p_main, conv_w, col0=3 * A_DIM, ncol=3 * B_DIM, mode="raw", **ckw)
        logw, kk, kka, kt, g_out = rw_features(p_small, rkv, wup, aup, gup, w0, a0, k_k, k_a)
        feats.append(dict(p_main=p_main, p_small=p_small, q=q, k=k, v=v, rkv=rkv,
                          logw=logw, kk=kk, kka=kka, kt=kt, g_out=g_out))

    nb = streams[-1]["batch"]
    outs = [dict() for _ in streams]
    for d in range(2):
        rev = d == 1
        s_a = jnp.zeros((nb, A_HEADS, A_DK, A_DK), F32)
        s_b = jnp.zeros((nb, B_DIM // LANES, LANES, LANES), F32)
        for si, (st, f) in enumerate(zip(streams, feats)):
            skw = dict(batch=st["batch"], seq=st["seq"], rev=rev, d=d)
            o, s_a = gdn_scan(f["q"], f["k"], f["v"], f["p_small"], a_vec, dt_vec, s_a, gate_blk=4, **skw)
            y, s_b = rwkv_scan(f["rkv"], f["logw"], f["kk"], f["kka"], f["kt"], s_b, **skw)
            outs[si]["o%d" % d] = o
            outs[si]["y%d" % d] = y

    new_x = []
    for si, (st, f) in enumerate(zip(streams, feats)):
        if si == 0 and not ctx_out:
            new_x.append(None)
            continue
        o = outs[si]
        new_x.append(ab_out(o["o0"], o["o1"], o["y0"], o["y1"], f["p_main"], f["rkv"], f["kt"], f["g_out"],
                            st["x"], st["mod"], gnw, rk, lnw, lnb, w_out, gate_idx=2, rows_per_mod=st["rpm"]))
    return new_x


def _ml_layer(streams, prm, ctx_out):
    w_in = prm["w_in"]
    main_cols = 2 * C_QK + 2 * C_V
    w_all = w_in.astype(BF16)
    d_model = w_in.shape[0]
    w_g = jnp.concatenate([w_in[:, main_cols:], jnp.zeros((d_model, LANES - 4 * C_HEADS), F32)], axis=1).astype(BF16)
    ib_vec = _lane_vec(prm["i_bias"], 0)
    fb_vec = _lane_vec(prm["f_bias"], 2 * C_HEADS)
    nw = prm["ml_norm_w"].reshape(1, C_V)
    w_out = prm["w_out"].astype(BF16)

    feats = []
    for st in streams:
        kw = dict(shift_idx=0, scale_idx=1, rows_per_mod=st["rpm"])
        p_main, p_g = modmm(st["x"], st["mod"], prm["norm_w"], w_all, w_g, n_main=main_cols, **kw)
        feats.append(dict(p_main=p_main, p_g=p_g))

    nb = streams[-1]["batch"]
    outs = [dict() for _ in streams]
    for d in range(2):
        rev = d == 1
        c_st = jnp.zeros((nb, C_HEADS, C_DK, C_DV), F32)
        n_st = jnp.zeros((nb, C_HEADS, SUBLANES, C_DK), F32)
        m_st = jnp.zeros((nb, C_HEADS, SUBLANES, LANES), F32)
        for si, (st, f) in enumerate(zip(streams, feats)):
            h, c_st, n_st, m_st = mlstm_scan(f["p_main"], f["p_g"], ib_vec, fb_vec, c_st, n_st, m_st,
                                             batch=st["batch"], seq=st["seq"], rev=rev, d=d)
            outs[si]["h%d" % d] = h

    new_x = []
    for si, (st, f) in enumerate(zip(streams, feats)):
        if si == 0 and not ctx_out:
            new_x.append(None)
            continue
        new_x.append(ml_out(outs[si]["h0"], outs[si]["h1"], f["p_main"], st["x"], st["mod"], nw, w_out,
                            gate_idx=2, rows_per_mod=st["rpm"]))
    return new_x


def kernel(x, c, ctx, c_ctx, ada_w, ada_b, norm_w, ab_w_in, ab_conv_w, gdn_a_log, gdn_dt_bias, gdn_norm_w, rw_w0, rw_w_up, rw_a0, rw_a_up, rw_g_up, rw_k_k, rw_k_a, rw_r_k, rw_ln_w, rw_ln_b, ab_w_out, ml_w_in, ml_i_bias, ml_f_bias, ml_norm_w, ml_w_out, ffn_w1, ffn_w3, ffn_w2, final_norm_w):
    bsz, seq, dm = x.shape
    ctx_len = ctx.shape[1]
    depth = ada_w.shape[0]
    xl = x.reshape(bsz * seq, dm)
    xc = ctx.reshape(bsz * ctx_len, dm)
    cond = jnp.zeros((MOD_ROWS, dm), F32).at[:bsz].set(c).at[bsz].set(c_ctx)

    w2 = ffn_w2.astype(BF16)
    for i in range(depth):
        ctx_out = i < depth - 1
        j = i // 2
        mod_all = adaln(cond, ada_w, ada_b[i].reshape(1, -1), i).reshape(MOD_ROWS, 6, dm)
        mod_all = jnp.pad(mod_all, ((0, 0), (0, MOD_ROWS - 6), (0, 0)))
        streams = [
            dict(x=xc, mod=mod_all[bsz:bsz + 1], batch=bsz, seq=ctx_len, rows=1, rpm=bsz * ctx_len),
            dict(x=xl, mod=mod_all[:bsz], batch=bsz, seq=seq, rows=seq // GRID_W, rpm=seq),
        ]
        if i % 2 == 0:
            prm = dict(w_in=ab_w_in[j], conv_w=ab_conv_w[j], a_log=gdn_a_log[j], dt_bias=gdn_dt_bias[j],
                       gdn_norm_w=gdn_norm_w[j], w0=rw_w0[j], w_up=rw_w_up[j], a0=rw_a0[j], a_up=rw_a_up[j],
                       g_up=rw_g_up[j], k_k=rw_k_k[j], k_a=rw_k_a[j], r_k=rw_r_k[j], ln_w=rw_ln_w[j],
                       ln_b=rw_ln_b[j], w_out=ab_w_out[j], norm_w=norm_w[i, 0].reshape(1, dm))
            xc_new, xl = _ab_layer(streams, prm, ctx_out)
        else:
            prm = dict(w_in=ml_w_in[j], i_bias=ml_i_bias[j], f_bias=ml_f_bias[j], ml_norm_w=ml_norm_w[j],
                       w_out=ml_w_out[j], norm_w=norm_w[i, 0].reshape(1, dm))
            xc_new, xl = _ml_layer(streams, prm, ctx_out)
        nw2 =norm_w[i, 1].reshape(1, dm)
        xl = _ffn(xl, streams[1]["mod"], nw2, ffn_w1, ffn_w3, w2, i, streams[1]["rpm"])
        if ctx_out:
            xc = _ffn(xc_new, streams[0]["mod"], nw2, ffn_w1, ffn_w3, w2, i, streams[0]["rpm"])
    return final_norm(xl, final_norm_w.reshape(1, dm)).reshape(bsz, seq, dm)
```
